```python
import jax, jax.numpy as jnp
from jax import lax
import numpy as np

D_MODEL = 1024
BATCH = 8
SEQ = 2048
DEPTH = 1
DEC_BATCH = 128
DEC_SEQ = 4
PAST_LEN = 16384
PAGE_SIZE = 128

MLA_HEADS = 8
QK_NOPE_DIM = 64
QK_ROPE_DIM = 32
V_HEAD_DIM = 64
Q_LORA_RANK = 384
KV_LORA_RANK = 256
KV_CACHE_DIM = KV_LORA_RANK + QK_ROPE_DIM
ROPE_THETA = 10000.0
MLA_SCALE = (QK_NOPE_DIM + QK_ROPE_DIM) ** -0.5
Q_BLOCK = 128
LRU_WIDTH = D_MODEL // 2
LRU_BLOCKS = 8
LRU_BLOCK_DIM = LRU_WIDTH // LRU_BLOCKS
LRU_C = 8.0
CONV_WIDTH = 4
MIX_OUT = MLA_HEADS * V_HEAD_DIM + LRU_WIDTH
IN_PROJ_DIM = Q_LORA_RANK + KV_LORA_RANK + QK_ROPE_DIM + 2 * LRU_WIDTH
IN_SPLITS = (Q_LORA_RANK, Q_LORA_RANK + KV_LORA_RANK, Q_LORA_RANK + KV_LORA_RANK + QK_ROPE_DIM,
             Q_LORA_RANK + KV_LORA_RANK + QK_ROPE_DIM + LRU_WIDTH)
N_MEM = 256
MEM_HEADS = 4
MEM_HEAD_DIM = D_MODEL // MEM_HEADS
D_FF = 4 * D_MODEL
EPS = 1e-6
F32 = jnp.float32

kernel_name = "hybrid_mla_rglru_memory_decode_step"


def rmsnorm(x, g):
    xf = x.astype(F32)
    y = xf * lax.rsqrt(jnp.mean(xf * xf, axis=-1, keepdims=True) + EPS)
    return (y * g.astype(F32)).astype(x.dtype)


def rope_angles(pos):
    inv = ROPE_THETA ** (-jnp.arange(0, QK_ROPE_DIM, 2, dtype=F32) / QK_ROPE_DIM)
    ang = pos.astype(F32)[:, None] * inv[None, :]
    return jnp.cos(ang), jnp.sin(ang)


def apply_rope(x, cos, sin):
    x1, x2 = jnp.split(x.astype(F32), 2, axis=-1)
    return jnp.concatenate([x1 * cos - x2 * sin, x1 * sin + x2 * cos], axis=-1).astype(x.dtype)


def mixer_in(h, pos, lp):
    B, S = h.shape[:2]
    z = h @ lp["w_in"]
    q_lat, kv_lat, k_rope, lru_x, lru_gate = jnp.split(z, IN_SPLITS, axis=-1)
    cos, sin = rope_angles(pos)
    q = (rmsnorm(q_lat, lp["q_norm_g"]) @ lp["w_q_up"]).reshape(B, S, MLA_HEADS, QK_NOPE_DIM + QK_ROPE_DIM)
    q_nope = q[..., :QK_NOPE_DIM]
    q_rope = apply_rope(q[..., QK_NOPE_DIM:], cos[:, None, :], sin[:, None, :])
    rows = jnp.concatenate([rmsnorm(kv_lat, lp["kv_norm_g"]), apply_rope(k_rope, cos, sin)], axis=-1)
    return q_nope, q_rope, rows, lru_x, lru_gate


def mla_prompt(q_nope, q_rope, rows, lp):
    B, S = rows.shape[:2]
    c, kr = rows[..., :KV_LORA_RANK], rows[..., KV_LORA_RANK:]
    k_nope = jnp.einsum('bsc,chd->bshd', c, lp["w_uk"])
    v = jnp.einsum('bsc,chd->bshd', c, lp["w_uv"])
    nb = S // Q_BLOCK
    qn = q_nope.reshape(B, nb, Q_BLOCK, MLA_HEADS, QK_NOPE_DIM).transpose(1, 0, 2, 3, 4)
    qr = q_rope.reshape(B, nb, Q_BLOCK, MLA_HEADS, QK_ROPE_DIM).transpose(1, 0, 2, 3, 4)
    kpos = jnp.arange(S)

    def block(args):
        i, qn_b, qr_b = args
        s = (jnp.einsum('bqhd,bkhd->bhqk', qn_b, k_nope) + jnp.einsum('bqhr,bkr->bhqk', qr_b, kr)).astype(F32) * MLA_SCALE
        qpos = i * Q_BLOCK + jnp.arange(Q_BLOCK)
        s = jnp.where(kpos[None, :] <= qpos[:, None], s, -jnp.inf)
        p = jax.nn.softmax(s, axis=-1).astype(v.dtype)
        return jnp.einsum('bhqk,bkhd->bqhd', p, v)

    o = lax.map(block, (jnp.arange(nb), qn, qr))
    return o.transpose(1, 0, 2, 3, 4).reshape(B, S, MLA_HEADS * V_HEAD_DIM)


def mla_sample(q_nope, q_rope, new_rows, cache_l, page_table, lp):
    Bd, T = new_rows.shape[:2]
    past = cache_l[page_table].reshape(Bd, -1, KV_CACHE_DIM)
    q_lat = jnp.einsum('bthd,chd->bthc', q_nope, lp["w_uk"])
    pc, pr = past[..., :KV_LORA_RANK], past[..., KV_LORA_RANK:]
    nc, nr = new_rows[..., :KV_LORA_RANK], new_rows[..., KV_LORA_RANK:]
    s_past = jnp.einsum('bthc,bkc->bhtk', q_lat, pc) + jnp.einsum('bthr,bkr->bhtk', q_rope, pr)
    s_new = jnp.einsum('bthc,bkc->bhtk', q_lat, nc) + jnp.einsum('bthr,bkr->bhtk', q_rope, nr)
    s_new = jnp.where(jnp.tril(jnp.ones((T, T), dtype=bool)), s_new.astype(F32), -jnp.inf)
    s = jnp.concatenate([s_past.astype(F32), s_new], axis=-1) * MLA_SCALE
    p = jax.nn.softmax(s, axis=-1).astype(new_rows.dtype)
    n_past = past.shape[1]
    ctx = jnp.einsum('bhtk,bkc->bthc', p[..., :n_past], pc) + jnp.einsum('bhtk,bkc->bthc', p[..., n_past:], nc)
    o = jnp.einsum('bthc,chd->bthd', ctx, lp["w_uv"])
    return o.reshape(Bd, T, MLA_HEADS * V_HEAD_DIM)


def causal_conv(x, buf, w, b):
    T = x.shape[1]
    xp = jnp.concatenate([buf.astype(x.dtype), x], axis=1)
    y = b + sum(xp[:, j:j + T] * w[j] for j in range(CONV_WIDTH))
    return y, xp[:, -(CONV_WIDTH - 1):]


def rg_lru(x, h0, lp):
    B, T, _ = x.shape
    xb = x.reshape(B, T, LRU_BLOCKS, LRU_BLOCK_DIM)
    r = jax.nn.sigmoid(jnp.einsum('btnd,nde->btne', xb, lp["lru_w_a"]).reshape(B, T, LRU_WIDTH) + lp["lru_b_a"])
    i = jax.nn.sigmoid(jnp.einsum('btnd,nde->btne', xb, lp["lru_w_x"]).reshape(B, T, LRU_WIDTH) + lp["lru_b_x"])
    log_a = -LRU_C * r.astype(F32) * jax.nn.softplus(-lp["lru_lambda"].astype(F32))
    a = jnp.exp(log_a)
    u = (x * i).astype(F32) * jnp.sqrt(-jnp.expm1(2.0 * log_a))

    def step(h, au):
        a_t, u_t = au
        h = a_t * h + u_t
        return h, h

    hT, hs = lax.scan(step, h0.astype(F32), (a.transpose(1, 0, 2), u.transpose(1, 0, 2)))
    return hs.transpose(1, 0, 2).astype(x.dtype), hT.astype(x.dtype)


def recurrent_branch(lru_x, lru_gate, conv_buf, h0, lp):
    xc, new_buf = causal_conv(lru_x, conv_buf, lp["conv_w"], lp["conv_b"])
    y, hT = rg_lru(xc, h0, lp)
    return y * jax.nn.gelu(lru_gate), hT, new_buf


def mixer_out(attn, rec, lp):
    return jnp.concatenate([attn, rec], axis=-1) @ lp["w_out"]


def mem_kv(mem, lp):
    B = mem.shape[0]
    m = rmsnorm(mem, lp["mem_norm_g"])
    k = (m @ lp["w_mem_k"]).reshape(B, N_MEM, MEM_HEADS, MEM_HEAD_DIM)
    v = (m @ lp["w_mem_v"]).reshape(B, N_MEM, MEM_HEADS, MEM_HEAD_DIM)
    return k, v


def mem_attend(h, k, v, lp):
    B, S = h.shape[:2]
    q = (h @ lp["w_mem_q"]).reshape(B, S, MEM_HEADS, MEM_HEAD_DIM)
    s = jnp.einsum('bshd,bmhd->bhsm', q, k).astype(F32) * (MEM_HEAD_DIM ** -0.5)
    p = jax.nn.softmax(s, axis=-1).astype(v.dtype)
    o = jnp.einsum('bhsm,bmhd->bshd', p, v).reshape(B, S, D_MODEL)
    return o @ lp["w_mem_o"]


def sq_relu_mlp(h, lp):
    return jnp.square(jax.nn.relu(h @ lp["w_up"])) @ lp["w_down"]


def setup_inputs(seed: int = 0) -> dict:
    key = jax.random.key(seed)
    ks = iter(jax.random.split(key, 64))

    def nrm(shape, scale):
        return jax.random.normal(next(ks), shape, F32) * scale

    def gain(shape):
        return 1.0 + 0.02 * jax.random.normal(next(ks), shape, F32)

    n_pages = PAST_LEN // PAGE_SIZE
    n_used = DEC_BATCH * n_pages
    n_pool = n_used + n_used // 4
    x_prompt = nrm((BATCH, SEQ, D_MODEL), 1.0)
    x_sample = nrm((DEC_BATCH, DEC_SEQ, D_MODEL), 1.0)
    cache_mla = nrm((DEPTH, n_pool, PAGE_SIZE, KV_CACHE_DIM), 1.0)
    cache_mem_k = nrm((DEPTH, DEC_BATCH, N_MEM, MEM_HEADS, MEM_HEAD_DIM), 1.0)
    cache_mem_v = nrm((DEPTH, DEC_BATCH, N_MEM, MEM_HEADS, MEM_HEAD_DIM), 1.0)
    state_lru_h = nrm((DEPTH, DEC_BATCH, LRU_WIDTH), 0.5)
    state_conv = nrm((DEPTH, DEC_BATCH, CONV_WIDTH - 1, LRU_WIDTH), 1.0)
    page_table = jax.random.permutation(next(ks), n_pool)[:n_used].reshape(DEC_BATCH, n_pages).astype(jnp.int32)
    mem_prompt = nrm((BATCH, N_MEM, D_MODEL), 1.0)
    a0 = jax.random.uniform(next(ks), (DEPTH, LRU_WIDTH), F32, 0.9, 0.999)
    base = a0 ** (1.0 / LRU_C)
    lru_lambda = jnp.log(base) - jnp.log1p(-base)
    return {
        "x_prompt": x_prompt,
        "x_sample": x_sample,
        "cache_mla": cache_mla,
        "cache_mem_k": cache_mem_k,
        "cache_mem_v": cache_mem_v,
        "state_lru_h": state_lru_h,
        "state_conv": state_conv,
        "page_table": page_table,
        "mem_prompt": mem_prompt,
        "norm_mix_g": gain((DEPTH, D_MODEL)),
        "w_in": nrm((DEPTH, D_MODEL, IN_PROJ_DIM), D_MODEL ** -0.5),
        "q_norm_g": gain((DEPTH, Q_LORA_RANK)),
        "w_q_up": nrm((DEPTH, Q_LORA_RANK, MLA_HEADS * (QK_NOPE_DIM + QK_ROPE_DIM)), Q_LORA_RANK ** -0.5),
        "kv_norm_g": gain((DEPTH, KV_LORA_RANK)),
        "w_uk": nrm((DEPTH, KV_LORA_RANK, MLA_HEADS, QK_NOPE_DIM), KV_LORA_RANK ** -0.5),
        "w_uv": nrm((DEPTH, KV_LORA_RANK, MLA_HEADS, V_HEAD_DIM), KV_LORA_RANK ** -0.5),
        "conv_w": nrm((DEPTH, CONV_WIDTH, LRU_WIDTH), CONV_WIDTH ** -0.5),
        "conv_b": nrm((DEPTH, LRU_WIDTH), 0.01),
        "lru_w_a": nrm((DEPTH, LRU_BLOCKS, LRU_BLOCK_DIM, LRU_BLOCK_DIM), LRU_BLOCK_DIM ** -0.5),
        "lru_b_a": nrm((DEPTH, LRU_WIDTH), 0.01),
        "lru_w_x": nrm((DEPTH, LRU_BLOCKS, LRU_BLOCK_DIM, LRU_BLOCK_DIM), LRU_BLOCK_DIM ** -0.5),
        "lru_b_x": nrm((DEPTH, LRU_WIDTH), 0.01),
        "lru_lambda": lru_lambda,
        "w_out": nrm((DEPTH, MIX_OUT, D_MODEL), MIX_OUT ** -0.5),
        "norm_mem_g": gain((DEPTH, D_MODEL)),
        "mem_norm_g": gain((DEPTH, D_MODEL)),
        "w_mem_q": nrm((DEPTH, D_MODEL, D_MODEL), D_MODEL ** -0.5),
        "w_mem_k": nrm((DEPTH, D_MODEL, D_MODEL), D_MODEL ** -0.5),
        "w_mem_v": nrm((DEPTH, D_MODEL, D_MODEL), D_MODEL ** -0.5),
        "w_mem_o": nrm((DEPTH, D_MODEL, D_MODEL), D_MODEL ** -0.5),
        "norm_mlp_g": gain((DEPTH, D_MODEL)),
        "w_up": nrm((DEPTH, D_MODEL, D_FF), D_MODEL ** -0.5),
        "w_down": nrm((DEPTH, D_FF, D_MODEL), D_FF ** -0.5),
        "final_norm_g": gain((D_MODEL,)),
    }


def reference(x_prompt, x_sample, cache_mla, cache_mem_k, cache_mem_v, state_lru_h, state_conv, page_table,
              mem_prompt, norm_mix_g, w_in, q_norm_g, w_q_up, kv_norm_g, w_uk, w_uv, conv_w, conv_b,
              lru_w_a, lru_b_a, lru_w_x, lru_b_x, lru_lambda, w_out, norm_mem_g, mem_norm_g,
              w_mem_q, w_mem_k, w_mem_v, w_mem_o, norm_mlp_g, w_up, w_down, final_norm_g):
    xp, xs = x_prompt, x_sample
    B, S = xp.shape[:2]
    Bd, T = xs.shape[:2]
    pos_p = jnp.arange(S)
    pos_s = PAST_LEN + jnp.arange(T)
    rows_p, rows_s, memk_p, memv_p, h_p, h_s, conv_p, conv_s = [], [], [], [], [], [], [], []
    for l in range(DEPTH):
        lp = dict(w_in=w_in[l], q_norm_g=q_norm_g[l], w_q_up=w_q_up[l], kv_norm_g=kv_norm_g[l],
                  w_uk=w_uk[l], w_uv=w_uv[l], conv_w=conv_w[l], conv_b=conv_b[l],
                  lru_w_a=lru_w_a[l], lru_b_a=lru_b_a[l], lru_w_x=lru_w_x[l], lru_b_x=lru_b_x[l],
                  lru_lambda=lru_lambda[l], w_out=w_out[l], mem_norm_g=mem_norm_g[l],
                  w_mem_q=w_mem_q[l], w_mem_k=w_mem_k[l], w_mem_v=w_mem_v[l], w_mem_o=w_mem_o[l],
                  w_up=w_up[l], w_down=w_down[l])
        h = rmsnorm(xp, norm_mix_g[l])
        qn, qr, rows, lx, lg = mixer_in(h, pos_p, lp)
        attn = mla_prompt(qn, qr, rows, lp)
        rec, hT, cbuf = recurrent_branch(lx, lg, jnp.zeros((B, CONV_WIDTH - 1, LRU_WIDTH), xp.dtype),
                                         jnp.zeros((B, LRU_WIDTH), xp.dtype), lp)
        xp = xp + mixer_out(attn, rec, lp)
        mk, mv = mem_kv(mem_prompt, lp)
        xp = xp + mem_attend(rmsnorm(xp, norm_mem_g[l]), mk, mv, lp)
        xp = xp + sq_relu_mlp(rmsnorm(xp, norm_mlp_g[l]), lp)
        rows_p.append(rows); memk_p.append(mk); memv_p.append(mv); h_p.append(hT); conv_p.append(cbuf)
        h = rmsnorm(xs, norm_mix_g[l])
        qn, qr, rows, lx, lg = mixer_in(h, pos_s, lp)
        attn = mla_sample(qn, qr, rows, cache_mla[l], page_table, lp)
        rec, hT, cbuf = recurrent_branch(lx, lg, state_conv[l], state_lru_h[l], lp)
        xs = xs + mixer_out(attn, rec, lp)
        xs = xs + mem_attend(rmsnorm(xs, norm_mem_g[l]), cache_mem_k[l], cache_mem_v[l], lp)
        xs = xs + sq_relu_mlp(rmsnorm(xs, norm_mlp_g[l]), lp)
        rows_s.append(rows); h_s.append(hT); conv_s.append(cbuf)
    y_prompt = rmsnorm(xp, final_norm_g)
    y_sample = rmsnorm(xs, final_norm_g)
    return (y_prompt, y_sample, jnp.stack(rows_p), jnp.stack(rows_s), jnp.stack(memk_p), jnp.stack(memv_p),
            jnp.stack(h_p), jnp.stack(h_s), jnp.stack(conv_p), jnp.stack(conv_s))
```

```python
import functools
import math

import jax
import jax.numpy as jnp
from jax import lax
from jax.experimental import pallas as pl
from jax.experimental.pallas import tpu as pltpu

F32 = jnp.float32
BF16 = jnp.bfloat16

D_MODEL = 1024
PAST_LEN = 16384
PAGE_SIZE = 128
MLA_HEADS = 8
QK_NOPE_DIM = 64
QK_ROPE_DIM = 32
V_HEAD_DIM = 64
Q_LORA_RANK = 384
KV_LORA_RANK = 256
KV_CACHE_DIM = KV_LORA_RANK + QK_ROPE_DIM
ROPE_THETA = 10000.0
MLA_SCALE = (QK_NOPE_DIM + QK_ROPE_DIM) ** -0.5
LRU_WIDTH = D_MODEL // 2
LRU_BLOCKS = 8
LRU_C = 8.0
CONV_WIDTH = 4
MEM_HEADS = 4
MEM_HEAD_DIM = D_MODEL // MEM_HEADS
D_FF = 4 * D_MODEL
EPS = 1e-6

LANES = 128
HEAD_PAD = LANES
LOG2E = math.log2(math.e)
VMEM_LIMIT = 48 * 1024 * 1024

_C_QLAT = 0
_C_KV = _C_QLAT + Q_LORA_RANK
_C_LX = _C_KV + KV_LORA_RANK
_C_LG = _C_LX + LRU_WIDTH
_C_KR = _C_LG + LRU_WIDTH
_C_KRR = _C_KR + LANES
IN_EXT = _C_KRR + LANES


def _rms(x, g):
    ms = jnp.mean(x * x, axis=-1, keepdims=True)
    return x * lax.rsqrt(ms + EPS) * g


def _dot(a, b):
    return jnp.dot(a, b, preferred_element_type=F32)


def _dot_nt(a, b):
    return lax.dot_general(a, b, (((1,), (1,)), ((), ())), preferred_element_type=F32)


def _const_spec(shape):
    nd = len(shape)
    return pl.BlockSpec(shape, lambda *_: (0,) * nd, pipeline_mode=pl.Buffered(1))


def _params(sem):
    return pltpu.CompilerParams(dimension_semantics=sem, vmem_limit_bytes=VMEM_LIMIT)


def _in_proj_kernel(x_ref, g_ref, win_ref, qg_ref, wq_ref, wqr_ref, kvg_ref,
                    cq_ref, sq_ref, ck_ref, sk_ref, *rest, prompt):
    if prompt:
        wuk_ref, tile_ref, wuv_ref, q_ref, rows_ref, lx_ref, lg_ref, k_ref, v_ref = rest
    else:
        wabs_ref, q_ref, rows_ref, lx_ref, lg_ref, qabs_ref = rest
    h = _rms(x_ref[...], g_ref[...]).astype(BF16)
    z = _dot(h, win_ref[...])
    qa = _rms(z[:, _C_QLAT:_C_KV], qg_ref[...]).astype(BF16)
    q1 = _dot(qa, wq_ref[...])
    q2 = _dot(qa, wqr_ref[...])
    cq = jnp.tile(cq_ref[...], (1, MLA_HEADS))
    sq = jnp.tile(sq_ref[...], (1, MLA_HEADS))
    qb = (q1 * cq + q2 * sq).astype(BF16)
    q_ref[...] = qb
    c = _rms(z[:, _C_KV:_C_LX], kvg_ref[...])
    kr = z[:, _C_KR:_C_KRR] * ck_ref[...] + z[:, _C_KRR:IN_EXT] * sk_ref[...]
    rows_ref[:, 0:KV_LORA_RANK] = c
    rows_ref[:, KV_LORA_RANK:KV_CACHE_DIM] = kr[:, 0:QK_ROPE_DIM]
    lx_ref[...] = z[:, _C_LX:_C_LG]
    lg_ref[...] = z[:, _C_LG:_C_KR]
    if prompt:
        cb = c.astype(BF16)
        k_ref[...] = (_dot(cb, wuk_ref[...]) + _dot(kr.astype(BF16), tile_ref[...])).astype(BF16)
        v_ref[...] = _dot(cb, wuv_ref[...]).astype(BF16)
    else:
        qabs_ref[...] = _dot(qb, wabs_ref[...]).astype(BF16)


def _in_proj(x, g, wts, tabs, *, prompt, tn, pos_blocks):
    n = x.shape[0]
    cq, sq, ck, sk = tabs
    tab_spec = pl.BlockSpec((tn, LANES), lambda i: (i % pos_blocks, 0))
    row_spec = lambda w: pl.BlockSpec((tn, w), lambda i: (i, 0))
    hp = MLA_HEADS * HEAD_PAD
    in_specs = [row_spec(D_MODEL), _const_spec(g.shape)]
    in_specs += [_const_spec(wts[k].shape) for k in ("win", "qg", "wq", "wqr", "kvg")]
    in_specs += [tab_spec] * 4
    args = [x, g, wts["win"], wts["qg"], wts["wq"], wts["wqr"], wts["kvg"], cq, sq, ck, sk]
    out_shape = [jax.ShapeDtypeStruct((n, hp), BF16), jax.ShapeDtypeStruct((n, KV_CACHE_DIM), F32),
                 jax.ShapeDtypeStruct((n, LRU_WIDTH), F32), jax.ShapeDtypeStruct((n, LRU_WIDTH), F32)]
    out_specs = [row_spec(hp), row_spec(KV_CACHE_DIM), row_spec(LRU_WIDTH), row_spec(LRU_WIDTH)]
    if prompt:
        extra = ("wuk", "tile", "wuv")
        out_shape += [jax.ShapeDtypeStruct((n, hp), BF16)] * 2
        out_specs += [row_spec(hp)] * 2
    else:
        extra = ("wabs",)
        out_shape += [jax.ShapeDtypeStruct((n, MLA_HEADS * KV_LORA_RANK), BF16)]
        out_specs += [row_spec(MLA_HEADS * KV_LORA_RANK)]
    in_specs += [_const_spec(wts[k].shape) for k in extra]
    args += [wts[k] for k in extra]
    return pl.pallas_call(
        functools.partial(_in_proj_kernel, prompt=prompt),
        grid=(n // tn,), in_specs=in_specs, out_specs=out_specs, out_shape=out_shape,
        compiler_params=_params(("parallel",)), name="in_proj_p" if prompt else "in_proj_s",
    )(*args)


def _attn_prompt_kernel(q_ref, k_ref, v_ref, o_ref, *, tq):
    qi = pl.program_id(2)
    q = q_ref[...]
    c_exp = MLA_SCALE * LOG2E

    def block(j, carry, masked):
        m, l, acc = carry
        off = pl.multiple_of(j * tq, tq)
        kb = k_ref[pl.ds(off, tq), :]
        vb = v_ref[pl.ds(off, tq), :]
        s = _dot_nt(q, kb)
        if masked:
            row = lax.broadcasted_iota(jnp.int32, s.shape, 0)
            col = lax.broadcasted_iota(jnp.int32, s.shape, 1)
            s = jnp.where(col <= row, s, -jnp.inf)
        m_new = jnp.maximum(m, jnp.max(s, axis=-1, keepdims=True))
        alpha = jnp.exp2((m - m_new) * c_exp)
        p = jnp.exp2((s - m_new) * c_exp)
        l = alpha * l + jnp.sum(p, axis=-1, keepdims=True)
        acc = alpha * acc + _dot(p.astype(BF16), vb)
        return m_new, l, acc

    init = (jnp.full((tq, 1), -jnp.inf, F32), jnp.zeros((tq, 1), F32), jnp.zeros((tq, HEAD_PAD), F32))
    carry = lax.fori_loop(0, qi, lambda j, c: block(j, c, False), init)
    _, l, acc = block(qi, carry, True)
    o_ref[...] = (acc / l).astype(BF16)


def _attn_prompt(q, k, v, *, batch, seq, tq):
    n = q.shape[0]
    nq = seq // tq
    return pl.pallas_call(
        functools.partial(_attn_prompt_kernel, tq=tq),
        grid=(batch, MLA_HEADS, nq),
        in_specs=[pl.BlockSpec((tq, HEAD_PAD), lambda b, h, i: (b * nq + i, h)),
                  pl.BlockSpec((seq, HEAD_PAD), lambda b, h, i: (b, h)),
                  pl.BlockSpec((seq, HEAD_PAD), lambda b, h, i: (b, h))],
        out_specs=pl.BlockSpec((tq, HEAD_PAD), lambda b, h, i: (b * nq + i, h)),
        out_shape=jax.ShapeDtypeStruct((n, MLA_HEADS * HEAD_PAD), BF16),
        compiler_params=_params(("parallel", "parallel", "arbitrary")), name="attn_prompt",
    )(q, k, v)


def _attn_sample_kernel(pt_ref, qa_ref, qr_ref, new_ref, cache_ref, ctx_ref, buf, sem, *, pages, n_chunks):
    b = pl.program_id(0)
    nb = pl.num_programs(0)
    c_exp = MLA_SCALE * LOG2E
    keys = pages * PAGE_SIZE

    def fetch(bb, chunk, slot):
        for i in range(pages):
            pid = pt_ref[bb, chunk * pages + i]
            pltpu.make_async_copy(cache_ref.at[pid], buf.at[slot, i], sem.at[slot]).start()

    def wait(slot):
        for i in range(pages):
            pltpu.make_async_copy(cache_ref.at[0], buf.at[slot, i], sem.at[slot]).wait()

    @pl.when(b == 0)
    def _():
        fetch(0, 0, 0)

    qa = qa_ref[0]
    qr = qr_ref[0]

    def process(slot, carry):
        m, l, acc = carry
        kc = buf[slot, :, :, 0:KV_LORA_RANK].reshape(keys, KV_LORA_RANK).astype(BF16)
        kr = buf[slot, :, :, KV_LORA_RANK:KV_CACHE_DIM].reshape(keys, QK_ROPE_DIM).astype(BF16)
        s = _dot_nt(qa, kc) + _dot_nt(qr, kr)
        m_new = jnp.maximum(m, jnp.max(s, axis=-1, keepdims=True))
        alpha = jnp.exp2((m - m_new) * c_exp)
        p = jnp.exp2((s - m_new) * c_exp)
        l = alpha * l + jnp.sum(p, axis=-1, keepdims=True)
        acc = alpha * acc + _dot(p.astype(BF16), kc)
        return m_new, l, acc

    half = n_chunks // 2

    def body(c2, carry):
        fetch(b, 2 * c2 + 1, 1)
        wait(0)
        carry = process(0, carry)

        @pl.when(c2 + 1 < half)
        def _():
            fetch(b, 2 * c2 + 2, 0)

        @pl.when(jnp.logical_and(c2 + 1 == half, b + 1 < nb))
        def _():
            fetch(b + 1, 0, 0)

        wait(1)
        return process(1, carry)

    rows = qa.shape[0]
    init = (jnp.full((rows, 1), -jnp.inf, F32), jnp.zeros((rows, 1), F32), jnp.zeros((rows, KV_LORA_RANK), F32))
    m, l, acc = lax.fori_loop(0, half, body, init)

    new = new_ref[0]
    n_new = new.shape[0]
    nc = new[:, 0:KV_LORA_RANK].astype(BF16).astype(F32)
    nr = new[:, KV_LORA_RANK:KV_CACHE_DIM].astype(BF16).astype(F32)
    qaf = qa.astype(F32)
    qrf = qr.astype(F32)
    tok = lax.shift_right_logical(lax.broadcasted_iota(jnp.int32, (rows, 1), 0), int(math.log2(MLA_HEADS)))
    s_new = []
    for t in range(n_new):
        st = (jnp.sum(qaf * nc[t:t + 1, :], axis=-1, keepdims=True)
              + jnp.sum(qrf * nr[t:t + 1, :], axis=-1, keepdims=True))
        s_new.append(jnp.where(tok >= t, st, -jnp.inf))
    m_new = m
    for st in s_new:
        m_new = jnp.maximum(m_new, st)
    alpha = jnp.exp2((m - m_new) * c_exp)
    l = alpha * l
    acc = alpha * acc
    for t, st in enumerate(s_new):
        p = jnp.exp2((st - m_new) * c_exp)
        l = l + p
        acc = acc + p.astype(BF16).astype(F32) * nc[t:t + 1, :]
    ctx_ref[0] = (acc / l).astype(BF16)


def _attn_sample(page_table, qabs, qrope, new_rows, cache, *, pages):
    nb, rows, _ = qabs.shape
    n_pages = page_table.shape[1]
    n_chunks = n_pages // pages
    assert n_chunks * pages == n_pages and n_chunks % 2 == 0
    t_new = new_rows.shape[1]
    grid_spec = pltpu.PrefetchScalarGridSpec(
        num_scalar_prefetch=1, grid=(nb,),
        in_specs=[pl.BlockSpec((1, rows, KV_LORA_RANK), lambda b, pt: (b, 0, 0)),
                  pl.BlockSpec((1, rows, QK_ROPE_DIM), lambda b, pt: (b, 0, 0)),
                  pl.BlockSpec((1, t_new, KV_CACHE_DIM), lambda b, pt: (b, 0, 0)),
                  pl.BlockSpec(memory_space=pl.ANY)],
        out_specs=pl.BlockSpec((1, rows, KV_LORA_RANK), lambda b, pt: (b, 0, 0)),
        scratch_shapes=[pltpu.VMEM((2, pages, PAGE_SIZE, KV_CACHE_DIM), F32),
                        pltpu.SemaphoreType.DMA((2,))])
    return pl.pallas_call(
        functools.partial(_attn_sample_kernel, pages=pages, n_chunks=n_chunks),
        grid_spec=grid_spec,
        out_shape=jax.ShapeDtypeStruct((nb, rows, KV_LORA_RANK), BF16),
        compiler_params=_params(("arbitrary",)), name="attn_sample",
    )(page_table, qabs, qrope, new_rows, cache)


def _lru_gates(y, wa_ref, ba_ref, wx_ref, bx_ref, lam_ref):
    yb = y.astype(BF16)
    r = jax.nn.sigmoid(_dot(yb, wa_ref[...]) + ba_ref[...])
    i = jax.nn.sigmoid(_dot(yb, wx_ref[...]) + bx_ref[...])
    nl = -lam_ref[...]
    softplus = jnp.maximum(nl, 0.0) + jnp.log1p(jnp.exp(-jnp.abs(nl)))
    log_a = -LRU_C * r * softplus
    a = jnp.exp(log_a)
    th = jnp.tanh(log_a)
    u = (y * i) * jnp.sqrt(-2.0 * th / (1.0 - th))
    return a, u


def _lru_prompt_kernel(lx_ref, lg_ref, cw_ref, cb_ref, wa_ref, ba_ref, wx_ref, bx_ref, lam_ref,
                       rec_ref, ht_ref, xp_scr, a_scr, u_scr, h_scr, *, tt):
    ti = pl.program_id(1)
    sub = 8

    @pl.when(ti == 0)
    def _():
        xp_scr[0:sub, :] = jnp.zeros((sub, LRU_WIDTH), F32)
        h_scr[...] = jnp.zeros((sub, LRU_WIDTH), F32)

    x = lx_ref[...]
    xp_scr[sub:sub + tt, :] = x
    cw = cw_ref[...]
    y = cb_ref[...] + cw[3:4, :] * x
    for j in range(CONV_WIDTH - 1):
        y = y + cw[j:j + 1, :] * xp_scr[sub - 3 + j:sub - 3 + j + tt, :]
    a, u = _lru_gates(y, wa_ref, ba_ref, wx_ref, bx_ref, lam_ref)
    a_scr[...] = a
    u_scr[...] = u
    row = lax.broadcasted_iota(jnp.int32, (sub, LRU_WIDTH), 0)

    def chunk(c, h):
        off = pl.multiple_of(c * sub, sub)
        ca = a_scr[pl.ds(off, sub), :]
        cu = u_scr[pl.ds(off, sub), :]
        for d in (1, 2, 4):
            keep = row >= d
            cu = jnp.where(keep, ca * pltpu.roll(cu, d, 0) + cu, cu)
            ca = jnp.where(keep, ca * pltpu.roll(ca, d, 0), ca)
        hs = ca * h + cu
        u_scr[pl.ds(off, sub), :] = hs
        return hs[sub - 1:sub, :]

    h_last = lax.fori_loop(0, tt // sub, chunk, h_scr[0:1, :])
    h_scr[0:1, :] = h_last
    ht_ref[0] = h_last
    rec_ref[...] = (u_scr[...] * jax.nn.gelu(lg_ref[...])).astype(BF16)
    xp_scr[0:sub, :] = xp_scr[tt:tt + sub, :]


def _lru_prompt(lx, lg, wts, *, batch, seq, tt):
    n = lx.shape[0]
    nt = seq // tt
    blk = pl.BlockSpec((tt, LRU_WIDTH), lambda b, t: (b * nt + t, 0))
    names = ("conv_w", "conv_b", "wa", "ba", "wx", "bx", "lam")
    return pl.pallas_call(
        functools.partial(_lru_prompt_kernel, tt=tt),
        grid=(batch, nt),
        in_specs=[blk, blk] + [_const_spec(wts[k].shape) for k in names],
        out_specs=[blk, pl.BlockSpec((1, 1, LRU_WIDTH), lambda b, t: (b, 0, 0))],
        out_shape=[jax.ShapeDtypeStruct((n, LRU_WIDTH), BF16), jax.ShapeDtypeStruct((batch, 1, LRU_WIDTH), F32)],
        scratch_shapes=[pltpu.VMEM((tt + 8, LRU_WIDTH), F32), pltpu.VMEM((tt, LRU_WIDTH), F32),
                        pltpu.VMEM((tt, LRU_WIDTH), F32), pltpu.VMEM((8, LRU_WIDTH), F32)],
        compiler_params=_params(("parallel", "arbitrary")), name="lru_prompt",
    )(lx, lg, *[wts[k] for k in names])


def _lru_sample_kernel(lx_ref, lg_ref, conv_ref, h0_ref, cw_ref, cb_ref, wa_ref, ba_ref, wx_ref, bx_ref, lam_ref,
                       rec_ref, ht_ref):
    steps = lx_ref.shape[0]
    xs = [conv_ref[j] for j in range(CONV_WIDTH - 1)] + [lx_ref[t] for t in range(steps)]
    cw = cw_ref[...]
    h = h0_ref[...]
    for t in range(steps):
        y = cb_ref[...]
        for j in range(CONV_WIDTH):
            y = y + cw[j:j + 1, :] * xs[t + j]
        a, u = _lru_gates(y, wa_ref, ba_ref, wx_ref, bx_ref, lam_ref)
        h = a * h + u
        rec_ref[t] = (h * jax.nn.gelu(lg_ref[t])).astype(BF16)
    ht_ref[...] = h


def _lru_sample(lx_t, lg_t, conv_t, h0, wts):
    steps, nb, _ = lx_t.shape
    names = ("conv_w", "conv_b", "wa", "ba", "wx", "bx", "lam")
    return pl.pallas_call(
        _lru_sample_kernel,
        out_shape=[jax.ShapeDtypeStruct((steps, nb, LRU_WIDTH), BF16), jax.ShapeDtypeStruct((nb, LRU_WIDTH), F32)],
        compiler_params=pltpu.CompilerParams(vmem_limit_bytes=VMEM_LIMIT), name="lru_sample",
    )(lx_t, lg_t, conv_t, h0, *[wts[k] for k in names])


def _mem_kv_kernel(m_ref, g_ref, wk_ref, wv_ref, k_ref, v_ref):
    m = _rms(m_ref[...], g_ref[...]).astype(BF16)
    k_ref[...] = _dot(m, wk_ref[...])
    v_ref[...] = _dot(m, wv_ref[...])


def _mem_kv(mem, g, wk, wv, *, tn):
    n = mem.shape[0]
    blk = pl.BlockSpec((tn, D_MODEL), lambda i: (i, 0))
    return pl.pallas_call(
        _mem_kv_kernel, grid=(n // tn,),
        in_specs=[blk, _const_spec(g.shape), _const_spec(wk.shape), _const_spec(wv.shape)],
        out_specs=[blk, blk], out_shape=[jax.ShapeDtypeStruct((n, D_MODEL), F32)] * 2,
        compiler_params=_params(("parallel",)), name="mem_kv",
    )(mem, g, wk, wv)


def _mix_out_kernel(x_ref, attn_ref, rec_ref, *rest, absorbed):
    if absorbed:
        wuv_ref, woa_ref, wor_ref, g_ref, wq_ref, x1_ref, qm_ref = rest
        attn = _dot(attn_ref[...], wuv_ref[...]).astype(BF16)
    else:
        woa_ref, wor_ref, g_ref, wq_ref, x1_ref, qm_ref = rest
        attn = attn_ref[...]
    x1 = x_ref[...] + _dot(attn, woa_ref[...]) + _dot(rec_ref[...], wor_ref[...])
    x1_ref[...] = x1
    qm_ref[...] = _dot(_rms(x1, g_ref[...]).astype(BF16), wq_ref[...]).astype(BF16)


def _mix_out(x, attn, rec, wts, *, absorbed, tn):
    n = x.shape[0]
    row_spec = lambda w: pl.BlockSpec((tn, w), lambda i: (i, 0))
    names = (("wuv_bd",) if absorbed else ()) + (("woa_c" if absorbed else "woa_p"), "wor", "g_mem", "w_mem_q")
    return pl.pallas_call(
        functools.partial(_mix_out_kernel, absorbed=absorbed), grid=(n // tn,),
        in_specs=[row_spec(D_MODEL), row_spec(attn.shape[1]), row_spec(LRU_WIDTH)]
        + [_const_spec(wts[k].shape) for k in names],
        out_specs=[row_spec(D_MODEL), row_spec(D_MODEL)],
        out_shape=[jax.ShapeDtypeStruct((n, D_MODEL), F32), jax.ShapeDtypeStruct((n, D_MODEL), BF16)],
        compiler_params=_params(("parallel",)), name="mix_out_s" if absorbed else "mix_out_p",
    )(x, attn, rec, *[wts[k] for k in names])


def _mem_attn_kernel(q_ref, k_ref, v_ref, o_ref):
    q = q_ref[0]
    scale = MEM_HEAD_DIM ** -0.5
    for h in range(MEM_HEADS):
        lo, hi = h * MEM_HEAD_DIM, (h + 1) * MEM_HEAD_DIM
        kh = k_ref[0, :, lo:hi].astype(BF16)
        vh = v_ref[0, :, lo:hi].astype(BF16)
        s = _dot_nt(q[:, lo:hi], kh) * scale
        e = jnp.exp(s - jnp.max(s, axis=-1, keepdims=True))
        p = e / jnp.sum(e, axis=-1, keepdims=True)
        o_ref[0, :, lo:hi] = _dot(p.astype(BF16), vh).astype(BF16)


def _mem_attn(q, k, v, *, tq, name):
    nb, sq, _ = q.shape
    n_mem = k.shape[1]
    qblk = pl.BlockSpec((1, tq, D_MODEL), lambda b, i: (b, i, 0))
    kblk = pl.BlockSpec((1, n_mem, D_MODEL), lambda b, i: (b, 0, 0))
    return pl.pallas_call(
        _mem_attn_kernel, grid=(nb, sq // tq),
        in_specs=[qblk, kblk, kblk], out_specs=qblk,
        out_shape=jax.ShapeDtypeStruct((nb, sq, D_MODEL), BF16),
        compiler_params=_params(("parallel", "arbitrary")), name=name,
    )(q, k, v)


def _tail_kernel(x1_ref, o_ref, wo_ref, g_ref, wup_ref, wdn_ref, gf_ref, y_ref, *, ff_chunk):
    x2 = x1_ref[...] + _dot(o_ref[...], wo_ref[...])
    h = _rms(x2, g_ref[...]).astype(BF16)
    acc = x2
    for c in range(D_FF // ff_chunk):
        lo, hi = c * ff_chunk, (c + 1) * ff_chunk
        up = jnp.maximum(_dot(h, wup_ref[:, lo:hi]), 0.0)
        acc = acc + _dot((up * up).astype(BF16), wdn_ref[lo:hi, :])
    y_ref[...] = _rms(acc, gf_ref[...])


def _tail(x1, o, wts, *, tn, name):
    n = x1.shape[0]
    blk = pl.BlockSpec((tn, D_MODEL), lambda i: (i, 0))
    names = ("w_mem_o", "g_mlp", "w_up", "w_down", "g_final")
    return pl.pallas_call(
        functools.partial(_tail_kernel, ff_chunk=1024), grid=(n // tn,),
        in_specs=[blk, blk] + [_const_spec(wts[k].shape) for k in names],
        out_specs=blk, out_shape=jax.ShapeDtypeStruct((n, D_MODEL), F32),
        compiler_params=_params(("parallel",)), name=name,
    )(x1, o, *[wts[k] for k in names])


def _rot_half_cols(w):
    half = QK_ROPE_DIM // 2
    return jnp.concatenate([-w[..., half:], w[..., :half]], axis=-1)


def _prep_weights(w_in, q_norm_g, w_q_up, kv_norm_g, w_uk, w_uv, conv_w, conv_b, lru_w_a, lru_b_a, lru_w_x, lru_b_x,
                  lru_lambda, w_out, norm_mem_g, mem_norm_g, w_mem_q, w_mem_k, w_mem_v, w_mem_o, norm_mlp_g,
                  w_up, w_down, final_norm_g):
    w = {}
    s1 = Q_LORA_RANK
    s2 = s1 + KV_LORA_RANK
    s3 = s2 + QK_ROPE_DIM
    s4 = s3 + LRU_WIDTH
    wkr = w_in[:, s2:s3]
    lane_pad = jnp.zeros((D_MODEL, LANES - QK_ROPE_DIM), F32)
    w["win"] = jnp.concatenate([w_in[:, :s1], w_in[:, s1:s2], w_in[:, s3:s4], w_in[:, s4:],
                                wkr, lane_pad, _rot_half_cols(wkr), lane_pad], axis=1).astype(BF16)
    w["qg"] = q_norm_g.reshape(1, -1)
    w["kvg"] = kv_norm_g.reshape(1, -1)
    wq3 = w_q_up.reshape(Q_LORA_RANK, MLA_HEADS, QK_NOPE_DIM + QK_ROPE_DIM)
    nope, ropew = wq3[..., :QK_NOPE_DIM], wq3[..., QK_NOPE_DIM:]
    tail_pad = jnp.zeros((Q_LORA_RANK, MLA_HEADS, HEAD_PAD - QK_NOPE_DIM - QK_ROPE_DIM), F32)
    w["wq"] = jnp.concatenate([nope, ropew, tail_pad], -1).reshape(Q_LORA_RANK, -1).astype(BF16)
    w["wqr"] = jnp.concatenate([jnp.zeros_like(nope), _rot_half_cols(ropew), tail_pad], -1
                               ).reshape(Q_LORA_RANK, -1).astype(BF16)
    head_pad = jnp.zeros((KV_LORA_RANK, MLA_HEADS, HEAD_PAD - QK_NOPE_DIM), F32)
    w["wuk"] = jnp.concatenate([w_uk, head_pad], -1).reshape(KV_LORA_RANK, -1).astype(BF16)
    w["wuv"] = jnp.concatenate([w_uv, head_pad], -1).reshape(KV_LORA_RANK, -1).astype(BF16)
    src = jnp.arange(LANES)[:, None]
    dst = jnp.arange(MLA_HEADS * HEAD_PAD)[None, :]
    w["tile"] = jnp.logical_and(src < QK_ROPE_DIM, dst % HEAD_PAD == src + QK_NOPE_DIM).astype(BF16)
    eye = jnp.eye(MLA_HEADS, dtype=F32)
    uk_t = jnp.transpose(w_uk, (1, 2, 0))
    uk_t = jnp.concatenate([uk_t, jnp.zeros((MLA_HEADS, HEAD_PAD - QK_NOPE_DIM, KV_LORA_RANK), F32)], 1)
    w["wabs"] = jnp.einsum("hdc,hg->hdgc", uk_t, eye).reshape(MLA_HEADS * HEAD_PAD, -1).astype(BF16)
    w["wuv_bd"] = jnp.einsum("chd,hg->hcgd", w_uv, eye).reshape(MLA_HEADS * KV_LORA_RANK, -1).astype(BF16)
    n_attn = MLA_HEADS * V_HEAD_DIM
    woa = w_out[:n_attn].reshape(MLA_HEADS, V_HEAD_DIM, D_MODEL)
    w["woa_p"] = jnp.concatenate([woa, jnp.zeros((MLA_HEADS, HEAD_PAD - V_HEAD_DIM, D_MODEL), F32)], 1
                                 ).reshape(-1, D_MODEL).astype(BF16)
    w["woa_c"] = w_out[:n_attn].astype(BF16)
    w["wor"] = w_out[n_attn:].astype(BF16)
    w["conv_w"] = conv_w
    w["conv_b"] = conv_b.reshape(1, -1)
    eye_l = jnp.eye(LRU_BLOCKS, dtype=F32)
    w["wa"] = jnp.einsum("nde,nm->ndme", lru_w_a, eye_l).reshape(LRU_WIDTH, LRU_WIDTH).astype(BF16)
    w["wx"] = jnp.einsum("nde,nm->ndme", lru_w_x, eye_l).reshape(LRU_WIDTH, LRU_WIDTH).astype(BF16)
    w["ba"] = lru_b_a.reshape(1, -1)
    w["bx"] = lru_b_x.reshape(1, -1)
    w["lam"] = lru_lambda.reshape(1, -1)
    w["g_mem"] = norm_mem_g.reshape(1, -1)
    w["g_memkv"] = mem_norm_g.reshape(1, -1)
    w["w_mem_q"] = w_mem_q.astype(BF16)
    w["w_mem_k"] = w_mem_k.astype(BF16)
    w["w_mem_v"] = w_mem_v.astype(BF16)
    w["w_mem_o"] = w_mem_o.astype(BF16)
    w["g_mlp"] = norm_mlp_g.reshape(1, -1)
    w["w_up"] = w_up.astype(BF16)
    w["w_down"] = w_down.astype(BF16)
    w["g_final"] = final_norm_g.reshape(1, -1)
    return w


def _rope_tables(pos):
    inv = ROPE_THETA ** (-jnp.arange(0, QK_ROPE_DIM, 2, dtype=F32) / QK_ROPE_DIM)
    ang = pos.astype(F32)[:, None] * inv[None, :]
    cos, sin = jnp.cos(ang), jnp.sin(ang)
    n = pos.shape[0]
    ones = jnp.ones((n, QK_NOPE_DIM), F32)
    z = lambda k: jnp.zeros((n, k), F32)
    q_tail = HEAD_PAD - QK_NOPE_DIM - QK_ROPE_DIM
    cq = jnp.concatenate([ones, cos, cos, z(q_tail)], 1)
    sq = jnp.concatenate([z(QK_NOPE_DIM), sin, sin, z(q_tail)], 1)
    ck = jnp.concatenate([cos, cos, z(LANES - QK_ROPE_DIM)], 1)
    sk = jnp.concatenate([sin, sin, z(LANES - QK_ROPE_DIM)], 1)
    return cq, sq, ck, sk


def kernel(x_prompt, x_sample, cache_mla, cache_mem_k, cache_mem_v, state_lru_h, state_conv, page_table, mem_prompt, norm_mix_g, w_in, q_norm_g, w_q_up, kv_norm_g, w_uk, w_uv, conv_w, conv_b, lru_w_a, lru_b_a, lru_w_x, lru_b_x, lru_lambda, w_out, norm_mem_g, mem_norm_g, w_mem_q, w_mem_k, w_mem_v, w_mem_o, norm_mlp_g, w_up, w_down, final_norm_g):
    B, S, _ = x_prompt.shape
    Bd, T, _ = x_sample.shape
    assert w_in.shape[0] == 1, "single layer"
    wts = _prep_weights(w_in[0], q_norm_g[0], w_q_up[0], kv_norm_g[0], w_uk[0], w_uv[0], conv_w[0], conv_b[0],
                        lru_w_a[0], lru_b_a[0], lru_w_x[0], lru_b_x[0], lru_lambda[0], w_out[0], norm_mem_g[0],
                        mem_norm_g[0], w_mem_q[0], w_mem_k[0], w_mem_v[0], w_mem_o[0], norm_mlp_g[0],
                        w_up[0], w_down[0], final_norm_g)
    g_mix = norm_mix_g[0].reshape(1, -1)
    tn = 512

    xp = x_prompt.reshape(B * S, D_MODEL)
    q_p, rows_p, lx_p, lg_p, k_p, v_p = _in_proj(xp, g_mix, wts, _rope_tables(jnp.arange(S)),
                                                 prompt=True, tn=tn, pos_blocks=S // tn)
    attn_p = _attn_prompt(q_p, k_p, v_p, batch=B, seq=S, tq=512)
    rec_p, ht_p = _lru_prompt(lx_p, lg_p, wts, batch=B, seq=S, tt=256)
    x1_p, qm_p = _mix_out(xp, attn_p, rec_p, wts, absorbed=False, tn=tn)
    n_mem = mem_prompt.shape[1]
    mk_p, mv_p = _mem_kv(mem_prompt.reshape(B * n_mem, D_MODEL), wts["g_memkv"], wts["w_mem_k"], wts["w_mem_v"], tn=tn)
    o_p = _mem_attn(qm_p.reshape(B, S, D_MODEL), mk_p.reshape(B, n_mem, D_MODEL), mv_p.reshape(B, n_mem, D_MODEL),
                    tq=512, name="mem_attn_p")
    y_p = _tail(x1_p, o_p.reshape(B * S, D_MODEL), wts, tn=tn, name="tail_p")

    ns = Bd * T
    xs = x_sample.reshape(ns, D_MODEL)
    tabs_s = tuple(jnp.tile(t, (Bd, 1)) for t in _rope_tables(PAST_LEN + jnp.arange(T)))
    q_s, rows_s, lx_s, lg_s, qabs_s = _in_proj(xs, g_mix, wts, tabs_s, prompt=False, tn=ns, pos_blocks=1)
    rows_q = T * MLA_HEADS
    qr_s = q_s.reshape(ns * MLA_HEADS, HEAD_PAD)[:, QK_NOPE_DIM:QK_NOPE_DIM + QK_ROPE_DIM]
    ctx_s = _attn_sample(page_table, qabs_s.reshape(Bd, rows_q, KV_LORA_RANK), qr_s.reshape(Bd, rows_q, QK_ROPE_DIM),
                         rows_s.reshape(Bd, T, KV_CACHE_DIM), cache_mla[0], pages=8)
    to_time_major = lambda a: jnp.transpose(a.reshape(Bd, -1, LRU_WIDTH), (1, 0, 2))
    rec_t, ht_s = _lru_sample(to_time_major(lx_s), to_time_major(lg_s), to_time_major(state_conv[0]),
                              state_lru_h[0], wts)
    rec_s = jnp.transpose(rec_t, (1, 0, 2)).reshape(ns, LRU_WIDTH)
    x1_s, qm_s = _mix_out(xs, ctx_s.reshape(ns, MLA_HEADS * KV_LORA_RANK), rec_s, wts, absorbed=True, tn=ns)
    q_rows = 16
    qm_pad = jnp.pad(qm_s.reshape(Bd, T, D_MODEL), ((0, 0), (0, q_rows - T), (0, 0)))
    o_s = _mem_attn(qm_pad, cache_mem_k[0].reshape(Bd, n_mem, D_MODEL), cache_mem_v[0].reshape(Bd, n_mem, D_MODEL),
                    tq=q_rows, name="mem_attn_s")
    y_s = _tail(x1_s, o_s[:, :T].reshape(ns, D_MODEL), wts, tn=ns, name="tail_s")

    lx_p3 = lx_p.reshape(B, S, LRU_WIDTH)
    lx_s3 = lx_s.reshape(Bd, T, LRU_WIDTH)
    keep = CONV_WIDTH - 1
    return (y_p.reshape(B, S, D_MODEL), y_s.reshape(Bd, T, D_MODEL),
            rows_p.reshape(1, B, S, KV_CACHE_DIM), rows_s.reshape(1, Bd, T, KV_CACHE_DIM),
            mk_p.reshape(1, B, n_mem, MEM_HEADS, MEM_HEAD_DIM), mv_p.reshape(1, B, n_mem, MEM_HEADS, MEM_HEAD_DIM),
            ht_p.reshape(1, B, LRU_WIDTH), ht_s.reshape(1, Bd, LRU_WIDTH),
            lx_p3[:, S - keep:].reshape(1, B, keep, LRU_WIDTH), lx_s3[:, T - keep:].reshape(1, Bd, keep, LRU_WIDTH))
```

```python
import functools
import math

import jax
import jax.numpy as jnp
from jax import lax
from jax.experimental import pallas as pl
from jax.experimental.pallas import tpu as pltpu

F32 = jnp.float32
BF16 = jnp.bfloat16

D_MODEL = 1024
PAST_LEN = 16384
PAGE_SIZE = 128
MLA_HEADS = 8
QK_NOPE_DIM = 64
QK_ROPE_DIM = 32
V_HEAD_DIM = 64
Q_LORA_RANK = 384
KV_LORA_RANK = 256
KV_CACHE_DIM = KV_LORA_RANK + QK_ROPE_DIM
ROPE_THETA = 10000.0
MLA_SCALE = (QK_NOPE_DIM + QK_ROPE_DIM) ** -0.5
LRU_WIDTH = D_MODEL // 2
LRU_BLOCKS = 8
LRU_C = 8.0
CONV_WIDTH = 4
MEM_HEADS = 4
MEM_HEAD_DIM = D_MODEL // MEM_HEADS
D_FF = 4 * D_MODEL
EPS = 1e-6

LANES = 128
HEAD_PAD = LANES
LOG2E = math.log2(math.e)
VMEM_LIMIT = 48 * 1024 * 1024

_C_QLAT = 0
_C_KV = _C_QLAT + Q_LORA_RANK
_C_LX = _C_KV + KV_LORA_RANK
_C_LG = _C_LX + LRU_WIDTH
_C_KR = _C_LG + LRU_WIDTH
_C_KRR = _C_KR + LANES
IN_EXT = _C_KRR + LANES


def _rms(x, g):
    ms = jnp.mean(x * x, axis=-1, keepdims=True)
    return x * lax.rsqrt(ms + EPS) * g


def _dot(a, b):
    return jnp.dot(a, b, preferred_element_type=F32)


def _dot_nt(a, b):
    return lax.dot_general(a, b, (((1,), (1,)), ((), ())), preferred_element_type=F32)


def _const_spec(shape):
    nd = len(shape)
    return pl.BlockSpec(shape, lambda *_: (0,) * nd, pipeline_mode=pl.Buffered(1))


def _params(sem):
    return pltpu.CompilerParams(dimension_semantics=sem, vmem_limit_bytes=VMEM_LIMIT)


def _in_proj_kernel(x_ref, g_ref, win_ref, qg_ref, wq_ref, wqr_ref, kvg_ref,
                    cq_ref, sq_ref, ck_ref, sk_ref, *rest, prompt):
    if prompt:
        wuk_ref, tile_ref, wuv_ref, q_ref, rows_ref, lx_ref, lg_ref, k_ref, v_ref = rest
    else:
        wabs_ref, q_ref, rows_ref, lx_ref, lg_ref, qabs_ref = rest
    h = _rms(x_ref[...], g_ref[...]).astype(BF16)
    z = _dot(h, win_ref[...])
    qa = _rms(z[:, _C_QLAT:_C_KV], qg_ref[...]).astype(BF16)
    q1 = _dot(qa, wq_ref[...])
    q2 = _dot(qa, wqr_ref[...])
    cq = jnp.tile(cq_ref[...], (1, MLA_HEADS))
    sq = jnp.tile(sq_ref[...], (1, MLA_HEADS))
    qb = (q1 * cq + q2 * sq).astype(BF16)
    q_ref[...] = qb
    c = _rms(z[:, _C_KV:_C_LX], kvg_ref[...])
    kr = z[:, _C_KR:_C_KRR] * ck_ref[...] + z[:, _C_KRR:IN_EXT] * sk_ref[...]
    rows_ref[:, 0:KV_LORA_RANK] = c
    rows_ref[:, KV_LORA_RANK:KV_CACHE_DIM] = kr[:, 0:QK_ROPE_DIM]
    lx_ref[...] = z[:, _C_LX:_C_LG]
    lg_ref[...] = z[:, _C_LG:_C_KR]
    if prompt:
        cb = c.astype(BF16)
        k_ref[...] = (_dot(cb, wuk_ref[...]) + _dot(kr.astype(BF16), tile_ref[...])).astype(BF16)
        v_ref[...] = _dot(cb, wuv_ref[...]).astype(BF16)
    else:
        qabs_ref[...] = _dot(qb, wabs_ref[...]).astype(BF16)


def _in_proj(x, g, wts, tabs, *, prompt, tn, pos_blocks):
    n = x.shape[0]
    cq, sq, ck, sk = tabs
    tab_spec = pl.BlockSpec((tn, LANES), lambda i: (i % pos_blocks, 0))
    row_spec = lambda w: pl.BlockSpec((tn, w), lambda i: (i, 0))
    hp = MLA_HEADS * HEAD_PAD
    in_specs = [row_spec(D_MODEL), _const_spec(g.shape)]
    in_specs += [_const_spec(wts[k].shape) for k in ("win", "qg", "wq", "wqr", "kvg")]
    in_specs += [tab_spec] * 4
    args = [x, g, wts["win"], wts["qg"], wts["wq"], wts["wqr"], wts["kvg"], cq, sq, ck, sk]
    out_shape = [jax.ShapeDtypeStruct((n, hp), BF16), jax.ShapeDtypeStruct((n, KV_CACHE_DIM), F32),
                 jax.ShapeDtypeStruct((n, LRU_WIDTH), F32), jax.ShapeDtypeStruct((n, LRU_WIDTH), F32)]
    out_specs = [row_spec(hp), row_spec(KV_CACHE_DIM), row_spec(LRU_WIDTH), row_spec(LRU_WIDTH)]
    if prompt:
        extra = ("wuk", "tile", "wuv")
        out_shape += [jax.ShapeDtypeStruct((n, hp), BF16)] * 2
        out_specs += [row_spec(hp)] * 2
    else:
        extra = ("wabs",)
        out_shape += [jax.ShapeDtypeStruct((n, MLA_HEADS * KV_LORA_RANK), BF16)]
        out_specs += [row_spec(MLA_HEADS * KV_LORA_RANK)]
    in_specs += [_const_spec(wts[k].shape) for k in extra]
    args += [wts[k] for k in extra]
    return pl.pallas_call(
        functools.partial(_in_proj_kernel, prompt=prompt),
        grid=(n // tn,), in_specs=in_specs, out_specs=out_specs, out_shape=out_shape,
        compiler_params=_params(("parallel",)), name="in_proj_p" if prompt else "in_proj_s",
    )(*args)


def _attn_prompt_kernel(q_ref, k_ref, v_ref, o_ref, *, tq):
    qi = pl.program_id(2)
    q = q_ref[...]
    c_exp = MLA_SCALE * LOG2E

    def block(j, carry, masked):
        m, l, acc = carry
        off = pl.multiple_of(j * tq, tq)
        kb = k_ref[pl.ds(off, tq), :]
        vb = v_ref[pl.ds(off, tq), :]
        s = _dot_nt(q, kb)
        if masked:
            row = lax.broadcasted_iota(jnp.int32, s.shape, 0)
            col = lax.broadcasted_iota(jnp.int32, s.shape, 1)
            s = jnp.where(col <= row, s, -jnp.inf)
        m_new = jnp.maximum(m, jnp.max(s, axis=-1, keepdims=True))
        alpha = jnp.exp2((m - m_new) * c_exp)
        p = jnp.exp2((s - m_new) * c_exp)
        l = alpha * l + jnp.sum(p, axis=-1, keepdims=True)
        acc = alpha * acc + _dot(p.astype(BF16), vb)
        return m_new, l, acc

    init = (jnp.full((tq, 1), -jnp.inf, F32), jnp.zeros((tq, 1), F32), jnp.zeros((tq, HEAD_PAD), F32))
    carry = lax.fori_loop(0, qi, lambda j, c: block(j, c, False), init)
    _, l, acc = block(qi, carry, True)
    o_ref[...] = (acc / l).astype(BF16)


def _attn_prompt(q, k, v, *, batch, seq, tq):
    n = q.shape[0]
    nq = seq // tq
    return pl.pallas_call(
        functools.partial(_attn_prompt_kernel, tq=tq),
        grid=(batch, MLA_HEADS, nq),
        in_specs=[pl.BlockSpec((tq, HEAD_PAD), lambda b, h, i: (b * nq + i, h)),
                  pl.BlockSpec((seq, HEAD_PAD), lambda b, h, i: (b, h)),
                  pl.BlockSpec((seq, HEAD_PAD), lambda b, h, i: (b, h))],
        out_specs=pl.BlockSpec((tq, HEAD_PAD), lambda b, h, i: (b * nq + i, h)),
        out_shape=jax.ShapeDtypeStruct((n, MLA_HEADS * HEAD_PAD), BF16),
        compiler_params=_params(("parallel", "parallel", "arbitrary")), name="attn_prompt",
    )(q, k, v)


def _attn_sample_kernel(pt_ref, qa_ref, qr_ref, new_ref, cache_ref, ctx_ref, ring, sem, kt_scr, s_scr,
                        *, pages, n_chunks, slots, pv_chunk):
    b = pl.program_id(0)
    nb = pl.num_programs(0)
    c_exp = MLA_SCALE * LOG2E
    keys = pages * PAGE_SIZE
    group = 4

    def fetch(bb, chunk, slot):
        for i in range(pages):
            pid = pt_ref[bb, chunk * pages + i]
            pltpu.make_async_copy(cache_ref.at[pid], ring.at[slot, i], sem.at[slot]).start()

    def wait(slot):
        for i in range(pages):
            pltpu.make_async_copy(cache_ref.at[0], ring.at[slot, i], sem.at[slot]).wait()

    @pl.when(b == 0)
    def _():
        for c in range(slots):
            fetch(0, c, c)

    qa = qa_ref[0]
    qr = qr_ref[0]
    rows = qa.shape[0]

    def lane_fold(x, op):
        out = x[:, 0:LANES]
        for j in range(1, x.shape[1] // LANES):
            out = op(out, x[:, j * LANES:(j + 1) * LANES])
        return out

    def score_group(g, mrun):
        c0 = g * group
        for k in range(group):
            wait(lax.rem(c0 + k, slots))
        for k in range(group):
            slot = lax.rem(c0 + k, slots)
            kt = jnp.concatenate([ring[slot, i] for i in range(pages)], axis=1).astype(BF16)
            off = pl.multiple_of((c0 + k) * keys, keys)
            kt_scr[:, pl.ds(off, keys)] = kt
            s = _dot(qa, kt[0:KV_LORA_RANK, :]) + _dot(qr, kt[KV_LORA_RANK:KV_CACHE_DIM, :])
            s_scr[:, pl.ds(off, keys)] = s
            mrun = jnp.maximum(mrun, lane_fold(s, jnp.maximum))
        nxt = c0 + slots
        wrap = nxt >= n_chunks
        bb = jnp.where(wrap, b + 1, b)
        bb = jnp.where(bb < nb, bb, 0)
        cc = jnp.where(wrap, nxt - n_chunks, nxt)
        for k in range(group):
            fetch(bb, cc + k, lax.rem(c0 + k, slots))
        return mrun

    mrun = lax.fori_loop(0, n_chunks // group, score_group, jnp.full((rows, LANES), -jnp.inf, F32))

    @pl.when(b == nb - 1)
    def _():
        for c in range(slots):
            wait(c)

    new = new_ref[0]
    n_new = new.shape[0]
    nc = new[:, 0:KV_LORA_RANK].astype(BF16).astype(F32)
    nr = new[:, KV_LORA_RANK:KV_CACHE_DIM].astype(BF16).astype(F32)
    qaf = qa.astype(F32)
    qrf = qr.astype(F32)
    tok = lax.shift_right_logical(lax.broadcasted_iota(jnp.int32, (rows, 1), 0), int(math.log2(MLA_HEADS)))
    s_new = []
    for t in range(n_new):
        st = (jnp.sum(qaf * nc[t:t + 1, :], axis=-1, keepdims=True)
              + jnp.sum(qrf * nr[t:t + 1, :], axis=-1, keepdims=True))
        s_new.append(jnp.where(tok >= t, st, -jnp.inf))

    m = jnp.max(mrun, axis=-1, keepdims=True)
    for st in s_new:
        m = jnp.maximum(m, st)

    def pv_step(j, carry):
        acc, lrun = carry
        off = pl.multiple_of(j * pv_chunk, pv_chunk)
        p = jnp.exp2((s_scr[:, pl.ds(off, pv_chunk)] - m) * c_exp)
        acc = acc + _dot_nt(p.astype(BF16), kt_scr[0:KV_LORA_RANK, pl.ds(off, pv_chunk)])
        return acc, lrun + lane_fold(p, jnp.add)

    acc, lrun = lax.fori_loop(0, (n_chunks * keys) // pv_chunk, pv_step,
                              (jnp.zeros((rows, KV_LORA_RANK), F32), jnp.zeros((rows, LANES), F32)), unroll=True)
    l = jnp.sum(lrun, axis=-1, keepdims=True)
    for t, st in enumerate(s_new):
        pt = jnp.exp2((st - m) * c_exp)
        l = l + pt
        acc = acc + pt.astype(BF16).astype(F32) * nc[t:t + 1, :]
    ctx_ref[0] = (acc / l).astype(BF16)


def _attn_sample(page_table, qabs, qrope, new_rows, cache_t, *, pages, slots, pv_chunk):
    nb, rows, _ = qabs.shape
    n_pages = page_table.shape[1]
    n_chunks = n_pages // pages
    n_keys = n_pages * PAGE_SIZE
    assert n_chunks * pages == n_pages and n_chunks % slots == 0 and slots % 4 == 0 and n_keys % pv_chunk == 0
    t_new = new_rows.shape[1]
    grid_spec = pltpu.PrefetchScalarGridSpec(
        num_scalar_prefetch=1, grid=(nb,),
        in_specs=[pl.BlockSpec((1, rows, KV_LORA_RANK), lambda b, pt: (b, 0, 0)),
                  pl.BlockSpec((1, rows, QK_ROPE_DIM), lambda b, pt: (b, 0, 0)),
                  pl.BlockSpec((1, t_new, KV_CACHE_DIM), lambda b, pt: (b, 0, 0)),
                  pl.BlockSpec(memory_space=pl.ANY)],
        out_specs=pl.BlockSpec((1, rows, KV_LORA_RANK), lambda b, pt: (b, 0, 0)),
        scratch_shapes=[pltpu.VMEM((slots, pages, KV_CACHE_DIM, PAGE_SIZE), F32),
                        pltpu.SemaphoreType.DMA((slots,)),
                        pltpu.VMEM((KV_CACHE_DIM, n_keys), BF16),
                        pltpu.VMEM((rows, n_keys), F32)])
    return pl.pallas_call(
        functools.partial(_attn_sample_kernel, pages=pages, n_chunks=n_chunks, slots=slots, pv_chunk=pv_chunk),
        grid_spec=grid_spec,
        out_shape=jax.ShapeDtypeStruct((nb, rows, KV_LORA_RANK), BF16),
        compiler_params=_params(("arbitrary",)), name="attn_sample",
    )(page_table, qabs, qrope, new_rows, cache_t)


def _lru_gates(y, wa_ref, ba_ref, wx_ref, bx_ref, lam_ref):
    yb = y.astype(BF16)
    r = jax.nn.sigmoid(_dot(yb, wa_ref[...]) + ba_ref[...])
    i = jax.nn.sigmoid(_dot(yb, wx_ref[...]) + bx_ref[...])
    nl = -lam_ref[...]
    softplus = jnp.maximum(nl, 0.0) + jnp.log1p(jnp.exp(-jnp.abs(nl)))
    log_a = -LRU_C * r * softplus
    a = jnp.exp(log_a)
    th = jnp.tanh(log_a)
    u = (y * i) * jnp.sqrt(-2.0 * th / (1.0 - th))
    return a, u


def _lru_prompt_kernel(lx_ref, lg_ref, cw_ref, cb_ref, wa_ref, ba_ref, wx_ref, bx_ref, lam_ref,
                       rec_ref, ht_ref, xp_scr, a_scr, u_scr, h_scr, *, tt):
    ti = pl.program_id(1)
    sub = 8

    @pl.when(ti == 0)
    def _():
        xp_scr[0:sub, :] = jnp.zeros((sub, LRU_WIDTH), F32)
        h_scr[...] = jnp.zeros((sub, LRU_WIDTH), F32)

    x = lx_ref[...]
    xp_scr[sub:sub + tt, :] = x
    cw = cw_ref[...]
    y = cb_ref[...] + cw[3:4, :] * x
    for j in range(CONV_WIDTH - 1):
        y = y + cw[j:j + 1, :] * xp_scr[sub - 3 + j:sub - 3 + j + tt, :]
    a, u = _lru_gates(y, wa_ref, ba_ref, wx_ref, bx_ref, lam_ref)
    a_scr[...] = a
    u_scr[...] = u
    row = lax.broadcasted_iota(jnp.int32, (sub, LRU_WIDTH), 0)

    def chunk(c, h):
        off = pl.multiple_of(c * sub, sub)
        ca = a_scr[pl.ds(off, sub), :]
        cu = u_scr[pl.ds(off, sub), :]
        for d in (1, 2, 4):
            keep = row >= d
            cu = jnp.where(keep, ca * pltpu.roll(cu, d, 0) + cu, cu)
            ca = jnp.where(keep, ca * pltpu.roll(ca, d, 0), ca)
        hs = ca * h + cu
        u_scr[pl.ds(off, sub), :] = hs
        return hs[sub - 1:sub, :]

    h_last = lax.fori_loop(0, tt // sub, chunk, h_scr[0:1, :])
    h_scr[0:1, :] = h_last
    ht_ref[0] = h_last
    rec_ref[...] = (u_scr[...] * jax.nn.gelu(lg_ref[...])).astype(BF16)
    xp_scr[0:sub, :] = xp_scr[tt:tt + sub, :]


def _lru_prompt(lx, lg, wts, *, batch, seq, tt):
    n = lx.shape[0]
    nt = seq // tt
    blk = pl.BlockSpec((tt, LRU_WIDTH), lambda b, t: (b * nt + t, 0))
    names = ("conv_w", "conv_b", "wa", "ba", "wx", "bx", "lam")
    return pl.pallas_call(
        functools.partial(_lru_prompt_kernel, tt=tt),
        grid=(batch, nt),
        in_specs=[blk, blk] + [_const_spec(wts[k].shape) for k in names],
        out_specs=[blk, pl.BlockSpec((1, 1, LRU_WIDTH), lambda b, t: (b, 0, 0))],
        out_shape=[jax.ShapeDtypeStruct((n, LRU_WIDTH), BF16), jax.ShapeDtypeStruct((batch, 1, LRU_WIDTH), F32)],
        scratch_shapes=[pltpu.VMEM((tt + 8, LRU_WIDTH), F32), pltpu.VMEM((tt, LRU_WIDTH), F32),
                        pltpu.VMEM((tt, LRU_WIDTH), F32), pltpu.VMEM((8, LRU_WIDTH), F32)],
        compiler_params=_params(("parallel", "arbitrary")), name="lru_prompt",
    )(lx, lg, *[wts[k] for k in names])


def _lru_sample_kernel(lx_ref, lg_ref, conv_ref, h0_ref, cw_ref, cb_ref, wa_ref, ba_ref, wx_ref, bx_ref, lam_ref,
                       rec_ref, ht_ref):
    steps = lx_ref.shape[0]
    xs = [conv_ref[j] for j in range(CONV_WIDTH - 1)] + [lx_ref[t] for t in range(steps)]
    cw = cw_ref[...]
    h = h0_ref[...]
    for t in range(steps):
        y = cb_ref[...]
        for j in range(CONV_WIDTH):
            y = y + cw[j:j + 1, :] * xs[t + j]
        a, u = _lru_gates(y, wa_ref, ba_ref, wx_ref, bx_ref, lam_ref)
        h = a * h + u
        rec_ref[t] = (h * jax.nn.gelu(lg_ref[t])).astype(BF16)
    ht_ref[...] = h


def _lru_sample(lx_t, lg_t, conv_t, h0, wts):
    steps, nb, _ = lx_t.shape
    names = ("conv_w", "conv_b", "wa", "ba", "wx", "bx", "lam")
    return pl.pallas_call(
        _lru_sample_kernel,
        out_shape=[jax.ShapeDtypeStruct((steps, nb, LRU_WIDTH), BF16), jax.ShapeDtypeStruct((nb, LRU_WIDTH), F32)],
        compiler_params=pltpu.CompilerParams(vmem_limit_bytes=VMEM_LIMIT), name="lru_sample",
    )(lx_t, lg_t, conv_t, h0, *[wts[k] for k in names])


def _mem_kv_kernel(m_ref, g_ref, wk_ref, wv_ref, k_ref, v_ref):
    m = _rms(m_ref[...], g_ref[...]).astype(BF16)
    k_ref[...] = _dot(m, wk_ref[...])
    v_ref[...] = _dot(m, wv_ref[...])


def _mem_kv(mem, g, wk, wv, *, tn):
    n = mem.shape[0]
    blk = pl.BlockSpec((tn, D_MODEL), lambda i: (i, 0))
    return pl.pallas_call(
        _mem_kv_kernel, grid=(n // tn,),
        in_specs=[blk, _const_spec(g.shape), _const_spec(wk.shape), _const_spec(wv.shape)],
        out_specs=[blk, blk], out_shape=[jax.ShapeDtypeStruct((n, D_MODEL), F32)] * 2,
        compiler_params=_params(("parallel",)), name="mem_kv",
    )(mem, g, wk, wv)


def _mix_out_kernel(x_ref, attn_ref, rec_ref, *rest, absorbed):
    if absorbed:
        wuv_ref, woa_ref, wor_ref, g_ref, wq_ref, x1_ref, qm_ref = rest
        attn = _dot(attn_ref[...], wuv_ref[...]).astype(BF16)
    else:
        woa_ref, wor_ref, g_ref, wq_ref, x1_ref, qm_ref = rest
        attn = attn_ref[...]
    x1 = x_ref[...] + _dot(attn, woa_ref[...]) + _dot(rec_ref[...], wor_ref[...])
    x1_ref[...] = x1
    qm_ref[...] = _dot(_rms(x1, g_ref[...]).astype(BF16), wq_ref[...]).astype(BF16)


def _mix_out(x, attn, rec, wts, *, absorbed, tn):
    n = x.shape[0]
    row_spec = lambda w: pl.BlockSpec((tn, w), lambda i: (i, 0))
    names = (("wuv_bd",) if absorbed else ()) + (("woa_c" if absorbed else "woa_p"), "wor", "g_mem", "w_mem_q")
    return pl.pallas_call(
        functools.partial(_mix_out_kernel, absorbed=absorbed), grid=(n // tn,),
        in_specs=[row_spec(D_MODEL), row_spec(attn.shape[1]), row_spec(LRU_WIDTH)]
        + [_const_spec(wts[k].shape) for k in names],
        out_specs=[row_spec(D_MODEL), row_spec(D_MODEL)],
        out_shape=[jax.ShapeDtypeStruct((n, D_MODEL), F32), jax.ShapeDtypeStruct((n, D_MODEL), BF16)],
        compiler_params=_params(("parallel",)), name="mix_out_s" if absorbed else "mix_out_p",
    )(x, attn, rec, *[wts[k] for k in names])


def _mem_attn_kernel(q_ref, k_ref, v_ref, o_ref):
    q = q_ref[0]
    scale = MEM_HEAD_DIM ** -0.5
    for h in range(MEM_HEADS):
        lo, hi = h * MEM_HEAD_DIM, (h + 1) * MEM_HEAD_DIM
        kh = k_ref[0, :, lo:hi].astype(BF16)
        vh = v_ref[0, :, lo:hi].astype(BF16)
        s = _dot_nt(q[:, lo:hi], kh) * scale
        e = jnp.exp(s - jnp.max(s, axis=-1, keepdims=True))
        p = e / jnp.sum(e, axis=-1, keepdims=True)
        o_ref[0, :, lo:hi] = _dot(p.astype(BF16), vh).astype(BF16)


def _mem_attn(q, k, v, *, tq, name):
    nb, sq, _ = q.shape
    n_mem = k.shape[1]
    qblk = pl.BlockSpec((1, tq, D_MODEL), lambda b, i: (b, i, 0))
    kblk = pl.BlockSpec((1, n_mem, D_MODEL), lambda b, i: (b, 0, 0))
    return pl.pallas_call(
        _mem_attn_kernel, grid=(nb, sq // tq),
        in_specs=[qblk, kblk, kblk], out_specs=qblk,
        out_shape=jax.ShapeDtypeStruct((nb, sq, D_MODEL), BF16),
        compiler_params=_params(("parallel", "arbitrary")), name=name,
    )(q, k, v)


def _mem_attn_cache_kernel(q_ref, k_ref, v_ref, o_ref, *, steps):
    q = q_ref[0]
    n_mem = k_ref.shape[1]
    k2 = k_ref[0].reshape(n_mem * MEM_HEADS, MEM_HEAD_DIM).astype(BF16)
    v2 = v_ref[0].reshape(n_mem * MEM_HEADS, MEM_HEAD_DIM).astype(BF16)
    s = _dot_nt(q, k2) * (MEM_HEAD_DIM ** -0.5)
    row_head = lax.shift_right_logical(lax.broadcasted_iota(jnp.int32, s.shape, 0), int(math.log2(steps)))
    col_head = lax.bitwise_and(lax.broadcasted_iota(jnp.int32, s.shape, 1), MEM_HEADS - 1)
    s = jnp.where(row_head == col_head, s, -jnp.inf)
    e = jnp.exp(s - jnp.max(s, axis=-1, keepdims=True))
    p = e / jnp.sum(e, axis=-1, keepdims=True)
    o_ref[0] = _dot(p.astype(BF16), v2).astype(BF16)


def _mem_attn_cache(q, k, v, *, steps):
    nb, rows, _ = q.shape
    qblk = pl.BlockSpec((1, rows, MEM_HEAD_DIM), lambda b: (b, 0, 0))
    kblk = pl.BlockSpec((1,) + k.shape[1:], lambda b: (b, 0, 0, 0))
    return pl.pallas_call(
        functools.partial(_mem_attn_cache_kernel, steps=steps), grid=(nb,),
        in_specs=[qblk, kblk, kblk], out_specs=qblk,
        out_shape=jax.ShapeDtypeStruct(q.shape, BF16),
        compiler_params=_params(("parallel",)), name="mem_attn_s",
    )(q, k, v)


def _tail_kernel(x1_ref, o_ref, wo_ref, g_ref, wup_ref, wdn_ref, gf_ref, y_ref, *, ff_chunk):
    x2 = x1_ref[...] + _dot(o_ref[...], wo_ref[...])
    h = _rms(x2, g_ref[...]).astype(BF16)
    acc = x2
    for c in range(D_FF // ff_chunk):
        lo, hi = c * ff_chunk, (c + 1) * ff_chunk
        up = jnp.maximum(_dot(h, wup_ref[:, lo:hi]), 0.0)
        acc = acc + _dot((up * up).astype(BF16), wdn_ref[lo:hi, :])
    y_ref[...] = _rms(acc, gf_ref[...])


def _tail(x1, o, wts, *, tn, name):
    n = x1.shape[0]
    blk = pl.BlockSpec((tn, D_MODEL), lambda i: (i, 0))
    names = ("w_mem_o", "g_mlp", "w_up", "w_down", "g_final")
    return pl.pallas_call(
        functools.partial(_tail_kernel, ff_chunk=1024), grid=(n // tn,),
        in_specs=[blk, blk] + [_const_spec(wts[k].shape) for k in names],
        out_specs=blk, out_shape=jax.ShapeDtypeStruct((n, D_MODEL), F32),
        compiler_params=_params(("parallel",)), name=name,
    )(x1, o, *[wts[k] for k in names])


def _rot_half_cols(w):
    half = QK_ROPE_DIM // 2
    return jnp.concatenate([-w[..., half:], w[..., :half]], axis=-1)


def _prep_weights(w_in, q_norm_g, w_q_up, kv_norm_g, w_uk, w_uv, conv_w, conv_b, lru_w_a, lru_b_a, lru_w_x, lru_b_x,
                  lru_lambda, w_out, norm_mem_g, mem_norm_g, w_mem_q, w_mem_k, w_mem_v, w_mem_o, norm_mlp_g,
                  w_up, w_down, final_norm_g):
    w = {}
    s1 = Q_LORA_RANK
    s2 = s1 + KV_LORA_RANK
    s3 = s2 + QK_ROPE_DIM
    s4 = s3 + LRU_WIDTH
    wkr = w_in[:, s2:s3]
    lane_pad = jnp.zeros((D_MODEL, LANES - QK_ROPE_DIM), F32)
    w["win"] = jnp.concatenate([w_in[:, :s1], w_in[:, s1:s2], w_in[:, s3:s4], w_in[:, s4:],
                                wkr, lane_pad, _rot_half_cols(wkr), lane_pad], axis=1).astype(BF16)
    w["qg"] = q_norm_g.reshape(1, -1)
    w["kvg"] = kv_norm_g.reshape(1, -1)
    wq3 = w_q_up.reshape(Q_LORA_RANK, MLA_HEADS, QK_NOPE_DIM + QK_ROPE_DIM)
    nope, ropew = wq3[..., :QK_NOPE_DIM], wq3[..., QK_NOPE_DIM:]
    tail_pad = jnp.zeros((Q_LORA_RANK, MLA_HEADS, HEAD_PAD - QK_NOPE_DIM - QK_ROPE_DIM), F32)
    w["wq"] = jnp.concatenate([nope, ropew, tail_pad], -1).reshape(Q_LORA_RANK, -1).astype(BF16)
    w["wqr"] = jnp.concatenate([jnp.zeros_like(nope), _rot_half_cols(ropew), tail_pad], -1
                               ).reshape(Q_LORA_RANK, -1).astype(BF16)
    head_pad = jnp.zeros((KV_LORA_RANK, MLA_HEADS, HEAD_PAD - QK_NOPE_DIM), F32)
    w["wuk"] = jnp.concatenate([w_uk, head_pad], -1).reshape(KV_LORA_RANK, -1).astype(BF16)
    w["wuv"] = jnp.concatenate([w_uv, head_pad], -1).reshape(KV_LORA_RANK, -1).astype(BF16)
    src = jnp.arange(LANES)[:, None]
    dst = jnp.arange(MLA_HEADS * HEAD_PAD)[None, :]
    w["tile"] = jnp.logical_and(src < QK_ROPE_DIM, dst % HEAD_PAD == src + QK_NOPE_DIM).astype(BF16)
    eye = jnp.eye(MLA_HEADS, dtype=F32)
    uk_t = jnp.transpose(w_uk, (1, 2, 0))
    uk_t = jnp.concatenate([uk_t, jnp.zeros((MLA_HEADS, HEAD_PAD - QK_NOPE_DIM, KV_LORA_RANK), F32)], 1)
    w["wabs"] = jnp.einsum("hdc,hg->hdgc", uk_t, eye).reshape(MLA_HEADS * HEAD_PAD, -1).astype(BF16)
    w["wuv_bd"] = jnp.einsum("chd,hg->hcgd", w_uv, eye).reshape(MLA_HEADS * KV_LORA_RANK, -1).astype(BF16)
    n_attn = MLA_HEADS * V_HEAD_DIM
    woa = w_out[:n_attn].reshape(MLA_HEADS, V_HEAD_DIM, D_MODEL)
    w["woa_p"] = jnp.concatenate([woa, jnp.zeros((MLA_HEADS, HEAD_PAD - V_HEAD_DIM, D_MODEL), F32)], 1
                                 ).reshape(-1, D_MODEL).astype(BF16)
    w["woa_c"] = w_out[:n_attn].astype(BF16)
    w["wor"] = w_out[n_attn:].astype(BF16)
    w["conv_w"] = conv_w
    w["conv_b"] = conv_b.reshape(1, -1)
    eye_l = jnp.eye(LRU_BLOCKS, dtype=F32)
    w["wa"] = jnp.einsum("nde,nm->ndme", lru_w_a, eye_l).reshape(LRU_WIDTH, LRU_WIDTH).astype(BF16)
    w["wx"] = jnp.einsum("nde,nm->ndme", lru_w_x, eye_l).reshape(LRU_WIDTH, LRU_WIDTH).astype(BF16)
    w["ba"] = lru_b_a.reshape(1, -1)
    w["bx"] = lru_b_x.reshape(1, -1)
    w["lam"] = lru_lambda.reshape(1, -1)
    w["g_mem"] = norm_mem_g.reshape(1, -1)
    w["g_memkv"] = mem_norm_g.reshape(1, -1)
    w["w_mem_q"] = w_mem_q.astype(BF16)
    w["w_mem_k"] = w_mem_k.astype(BF16)
    w["w_mem_v"] = w_mem_v.astype(BF16)
    w["w_mem_o"] = w_mem_o.astype(BF16)
    w["g_mlp"] = norm_mlp_g.reshape(1, -1)
    w["w_up"] = w_up.astype(BF16)
    w["w_down"] = w_down.astype(BF16)
    w["g_final"] = final_norm_g.reshape(1, -1)
    return w


def _rope_tables(pos):
    inv = ROPE_THETA ** (-jnp.arange(0, QK_ROPE_DIM, 2, dtype=F32) / QK_ROPE_DIM)
    ang = pos.astype(F32)[:, None] * inv[None, :]
    cos, sin = jnp.cos(ang), jnp.sin(ang)
    n = pos.shape[0]
    ones = jnp.ones((n, QK_NOPE_DIM), F32)
    z = lambda k: jnp.zeros((n, k), F32)
    q_tail = HEAD_PAD - QK_NOPE_DIM - QK_ROPE_DIM
    cq = jnp.concatenate([ones, cos, cos, z(q_tail)], 1)
    sq = jnp.concatenate([z(QK_NOPE_DIM), sin, sin, z(q_tail)], 1)
    ck = jnp.concatenate([cos, cos, z(LANES - QK_ROPE_DIM)], 1)
    sk = jnp.concatenate([sin, sin, z(LANES - QK_ROPE_DIM)], 1)
    return cq, sq, ck, sk


def kernel(x_prompt, x_sample, cache_mla, cache_mem_k, cache_mem_v, state_lru_h, state_conv, page_table, mem_prompt, norm_mix_g, w_in, q_norm_g, w_q_up, kv_norm_g, w_uk, w_uv, conv_w, conv_b, lru_w_a, lru_b_a, lru_w_x, lru_b_x, lru_lambda, w_out, norm_mem_g, mem_norm_g, w_mem_q, w_mem_k, w_mem_v, w_mem_o, norm_mlp_g, w_up, w_down, final_norm_g):
    B, S, _ = x_prompt.shape
    Bd, T, _ = x_sample.shape
    assert w_in.shape[0] == 1, "single layer"
    wts = _prep_weights(w_in[0], q_norm_g[0], w_q_up[0], kv_norm_g[0], w_uk[0], w_uv[0], conv_w[0], conv_b[0],
                        lru_w_a[0], lru_b_a[0], lru_w_x[0], lru_b_x[0], lru_lambda[0], w_out[0], norm_mem_g[0],
                        mem_norm_g[0], w_mem_q[0], w_mem_k[0], w_mem_v[0], w_mem_o[0], norm_mlp_g[0],
                        w_up[0], w_down[0], final_norm_g)
    g_mix = norm_mix_g[0].reshape(1, -1)
    tn = 512

    xp = x_prompt.reshape(B * S, D_MODEL)
    q_p, rows_p, lx_p, lg_p, k_p, v_p = _in_proj(xp, g_mix, wts, _rope_tables(jnp.arange(S)),
                                                 prompt=True, tn=tn, pos_blocks=S // tn)
    attn_p = _attn_prompt(q_p, k_p, v_p, batch=B, seq=S, tq=512)
    rec_p, ht_p = _lru_prompt(lx_p, lg_p, wts, batch=B, seq=S, tt=256)
    x1_p, qm_p = _mix_out(xp, attn_p, rec_p, wts, absorbed=False, tn=tn)
    n_mem = mem_prompt.shape[1]
    mk_p, mv_p = _mem_kv(mem_prompt.reshape(B * n_mem, D_MODEL), wts["g_memkv"], wts["w_mem_k"], wts["w_mem_v"], tn=tn)
    o_p = _mem_attn(qm_p.reshape(B, S, D_MODEL), mk_p.reshape(B, n_mem, D_MODEL), mv_p.reshape(B, n_mem, D_MODEL),
                    tq=512, name="mem_attn_p")
    y_p = _tail(x1_p, o_p.reshape(B * S, D_MODEL), wts, tn=tn, name="tail_p")

    ns = Bd * T
    xs = x_sample.reshape(ns, D_MODEL)
    tabs_s = tuple(jnp.tile(t, (Bd, 1)) for t in _rope_tables(PAST_LEN + jnp.arange(T)))
    q_s, rows_s, lx_s, lg_s, qabs_s = _in_proj(xs, g_mix, wts, tabs_s, prompt=False, tn=ns, pos_blocks=1)
    rows_q = T * MLA_HEADS
    qr_s = q_s.reshape(ns * MLA_HEADS, HEAD_PAD)[:, QK_NOPE_DIM:QK_NOPE_DIM + QK_ROPE_DIM]
    ctx_s = _attn_sample(page_table, qabs_s.reshape(Bd, rows_q, KV_LORA_RANK), qr_s.reshape(Bd, rows_q, QK_ROPE_DIM),
                         rows_s.reshape(Bd, T, KV_CACHE_DIM), jnp.swapaxes(cache_mla[0], 1, 2),
                         pages=8, slots=8, pv_chunk=2048)
    to_time_major = lambda a: jnp.transpose(a.reshape(Bd, -1, LRU_WIDTH), (1, 0, 2))
    rec_t, ht_s = _lru_sample(to_time_major(lx_s), to_time_major(lg_s), to_time_major(state_conv[0]),
                              state_lru_h[0], wts)
    rec_s = jnp.transpose(rec_t, (1, 0, 2)).reshape(ns, LRU_WIDTH)
    x1_s, qm_s = _mix_out(xs, ctx_s.reshape(ns, MLA_HEADS * KV_LORA_RANK), rec_s, wts, absorbed=True, tn=ns)
    head_major = lambda a: jnp.transpose(a.reshape(Bd, T, MEM_HEADS, MEM_HEAD_DIM), (0, 2, 1, 3))
    o_s = _mem_attn_cache(head_major(qm_s).reshape(Bd, MEM_HEADS * T, MEM_HEAD_DIM), cache_mem_k[0], cache_mem_v[0],
                          steps=T)
    o_s = jnp.transpose(o_s.reshape(Bd, MEM_HEADS, T, MEM_HEAD_DIM), (0, 2, 1, 3))
    y_s = _tail(x1_s, o_s.reshape(ns, D_MODEL), wts, tn=ns, name="tail_s")

    lx_p3 = lx_p.reshape(B, S, LRU_WIDTH)
    lx_s3 = lx_s.reshape(Bd, T, LRU_WIDTH)
    keep = CONV_WIDTH - 1
    return (y_p.reshape(B, S, D_MODEL), y_s.reshape(Bd, T, D_MODEL),
            rows_p.reshape(1, B, S, KV_CACHE_DIM), rows_s.reshape(1, Bd, T, KV_CACHE_DIM),
            mk_p.reshape(1, B, n_mem, MEM_HEADS, MEM_HEAD_DIM), mv_p.reshape(1, B, n_mem, MEM_HEADS, MEM_HEAD_DIM),
            ht_p.reshape(1, B, LRU_WIDTH), ht_s.reshape(1, Bd, LRU_WIDTH),
            lx_p3[:, S - keep:].reshape(1, B, keep, LRU_WIDTH), lx_s3[:, T - keep:].reshape(1, Bd, keep, LRU_WIDTH))
```

```python
import functools
import math

import jax
import jax.numpy as jnp
from jax import lax
from jax.experimental import pallas as pl
from jax.experimental.pallas import tpu as pltpu

F32 = jnp.float32
BF16 = jnp.bfloat16

D_MODEL = 1024
PAST_LEN = 16384
PAGE_SIZE = 128
MLA_HEADS = 8
QK_NOPE_DIM = 64
QK_ROPE_DIM = 32
V_HEAD_DIM = 64
Q_LORA_RANK = 384
KV_LORA_RANK = 256
KV_CACHE_DIM = KV_LORA_RANK + QK_ROPE_DIM
ROPE_THETA = 10000.0
MLA_SCALE = (QK_NOPE_DIM + QK_ROPE_DIM) ** -0.5
LRU_WIDTH = D_MODEL // 2
LRU_BLOCKS = 8
LRU_C = 8.0
CONV_WIDTH = 4
MEM_HEADS = 4
MEM_HEAD_DIM = D_MODEL // MEM_HEADS
D_FF = 4 * D_MODEL
EPS = 1e-6

LANES = 128
HEAD_PAD = LANES
LOG2E = math.log2(math.e)
VMEM_LIMIT = 48 * 1024 * 1024

_C_QLAT = 0
_C_KV = _C_QLAT + Q_LORA_RANK
_C_LX = _C_KV + KV_LORA_RANK
_C_LG = _C_LX + LRU_WIDTH
_C_KR = _C_LG + LRU_WIDTH
_C_KRR = _C_KR + LANES
IN_EXT = _C_KRR + LANES


def _rms(x, g):
    ms = jnp.mean(x * x, axis=-1, keepdims=True)
    return x * lax.rsqrt(ms + EPS) * g


def _dot(a, b):
    return jnp.dot(a, b, preferred_element_type=F32)


def _dot_nt(a, b):
    return lax.dot_general(a, b, (((1,), (1,)), ((), ())), preferred_element_type=F32)


def _const_spec(shape):
    nd = len(shape)
    return pl.BlockSpec(shape, lambda *_: (0,) * nd, pipeline_mode=pl.Buffered(1))


def _params(sem):
    return pltpu.CompilerParams(dimension_semantics=sem, vmem_limit_bytes=VMEM_LIMIT)


def _in_proj_kernel(x_ref, g_ref, win_ref, qg_ref, wq_ref, wqr_ref, kvg_ref,
                    cq_ref, sq_ref, ck_ref, sk_ref, *rest, prompt):
    if prompt:
        wuk_ref, tile_ref, wuv_ref, q_ref, rows_ref, lx_ref, lg_ref, k_ref, v_ref = rest
    else:
        wabs_ref, q_ref, rows_ref, lx_ref, lg_ref, qabs_ref = rest
    h = _rms(x_ref[...], g_ref[...]).astype(BF16)
    z = _dot(h, win_ref[...])
    qa = _rms(z[:, _C_QLAT:_C_KV], qg_ref[...]).astype(BF16)
    q1 = _dot(qa, wq_ref[...])
    q2 = _dot(qa, wqr_ref[...])
    cq = jnp.tile(cq_ref[...], (1, MLA_HEADS))
    sq = jnp.tile(sq_ref[...], (1, MLA_HEADS))
    qb = (q1 * cq + q2 * sq).astype(BF16)
    q_ref[...] = qb
    c = _rms(z[:, _C_KV:_C_LX], kvg_ref[...])
    kr = z[:, _C_KR:_C_KRR] * ck_ref[...] + z[:, _C_KRR:IN_EXT] * sk_ref[...]
    rows_ref[:, 0:KV_LORA_RANK] = c
    rows_ref[:, KV_LORA_RANK:KV_CACHE_DIM] = kr[:, 0:QK_ROPE_DIM]
    lx_ref[...] = z[:, _C_LX:_C_LG]
    lg_ref[...] = z[:, _C_LG:_C_KR]
    if prompt:
        cb = c.astype(BF16)
        k_ref[...] = (_dot(cb, wuk_ref[...]) + _dot(kr.astype(BF16), tile_ref[...])).astype(BF16)
        v_ref[...] = _dot_nt(wuv_ref[...], cb).astype(BF16)
    else:
        qabs_ref[...] = _dot(qb, wabs_ref[...]).astype(BF16)


def _in_proj(x, g, wts, tabs, *, prompt, tn, pos_blocks):
    n = x.shape[0]
    cq, sq, ck, sk = tabs
    tab_spec = pl.BlockSpec((tn, LANES), lambda i: (i % pos_blocks, 0))
    row_spec = lambda w: pl.BlockSpec((tn, w), lambda i: (i, 0))
    hp = MLA_HEADS * HEAD_PAD
    in_specs = [row_spec(D_MODEL), _const_spec(g.shape)]
    in_specs += [_const_spec(wts[k].shape) for k in ("win", "qg", "wq", "wqr", "kvg")]
    in_specs += [tab_spec] * 4
    args = [x, g, wts["win"], wts["qg"], wts["wq"], wts["wqr"], wts["kvg"], cq, sq, ck, sk]
    out_shape = [jax.ShapeDtypeStruct((n, hp), BF16), jax.ShapeDtypeStruct((n, KV_CACHE_DIM), F32),
                 jax.ShapeDtypeStruct((n, LRU_WIDTH), F32), jax.ShapeDtypeStruct((n, LRU_WIDTH), F32)]
    out_specs = [row_spec(hp), row_spec(KV_CACHE_DIM), row_spec(LRU_WIDTH), row_spec(LRU_WIDTH)]
    if prompt:
        extra = ("wuk", "tile", "wuv_t")
        out_shape += [jax.ShapeDtypeStruct((n, hp), BF16), jax.ShapeDtypeStruct((hp, n), BF16)]
        out_specs += [row_spec(hp), pl.BlockSpec((hp, tn), lambda i: (0, i))]
    else:
        extra = ("wabs",)
        out_shape += [jax.ShapeDtypeStruct((n, MLA_HEADS * KV_LORA_RANK), BF16)]
        out_specs += [row_spec(MLA_HEADS * KV_LORA_RANK)]
    in_specs += [_const_spec(wts[k].shape) for k in extra]
    args += [wts[k] for k in extra]
    return pl.pallas_call(
        functools.partial(_in_proj_kernel, prompt=prompt),
        grid=(n // tn,), in_specs=in_specs, out_specs=out_specs, out_shape=out_shape,
        compiler_params=_params(("parallel",)), name="in_proj_p" if prompt else "in_proj_s",
    )(*args)


def _attn_prompt_kernel(q_ref, k_ref, vt_ref, o_ref, *, tq, seq):
    c_exp = MLA_SCALE * LOG2E
    for qi in range(seq // tq):
        q = q_ref[qi * tq:(qi + 1) * tq, :]
        m = jnp.full((1, tq), -jnp.inf, F32)
        l = jnp.zeros((1, tq), F32)
        acc = jnp.zeros((HEAD_PAD, tq), F32)
        for j in range(qi + 1):
            kb = k_ref[j * tq:(j + 1) * tq, :]
            vb = vt_ref[:, j * tq:(j + 1) * tq]
            st = _dot_nt(kb, q)
            if j == qi:
                key = lax.broadcasted_iota(jnp.int32, st.shape, 0)
                qry = lax.broadcasted_iota(jnp.int32, st.shape, 1)
                st = jnp.where(key <= qry, st, -jnp.inf)
            m_new = jnp.maximum(m, jnp.max(st, axis=0, keepdims=True))
            alpha = jnp.exp2((m - m_new) * c_exp)
            pt = jnp.exp2((st - m_new) * c_exp)
            l = alpha * l + jnp.sum(pt, axis=0, keepdims=True)
            acc = alpha * acc + _dot(vb, pt.astype(BF16))
            m = m_new
        o_ref[qi * tq:(qi + 1) * tq, :] = jnp.transpose(acc / l).astype(BF16)


def _attn_prompt(q, k, vt, *, batch, seq, tq):
    n = q.shape[0]
    row_blk = pl.BlockSpec((seq, HEAD_PAD), lambda b, h: (b, h))
    return pl.pallas_call(
        functools.partial(_attn_prompt_kernel, tq=tq, seq=seq),
        grid=(batch, MLA_HEADS),
        in_specs=[row_blk, row_blk, pl.BlockSpec((HEAD_PAD, seq), lambda b, h: (h, b))],
        out_specs=row_blk,
        out_shape=jax.ShapeDtypeStruct((n, MLA_HEADS * HEAD_PAD), BF16),
        compiler_params=_params(("parallel", "parallel")), name="attn_prompt",
    )(q, k, vt)


def _attn_sample_kernel(pt_ref, qa_ref, qr_ref, new_ref, cache_ref, ctx_ref, ring, sem, kt_scr, s_scr,
                        *, pages, n_chunks, slots, pv_chunk):
    b = pl.program_id(0)
    nb = pl.num_programs(0)
    c_exp = MLA_SCALE * LOG2E
    keys = pages * PAGE_SIZE
    group = 4

    def fetch(bb, chunk, slot):
        for i in range(pages):
            pid = pt_ref[bb, chunk * pages + i]
            pltpu.make_async_copy(cache_ref.at[pid], ring.at[slot, i], sem.at[slot]).start()

    def wait(slot):
        for i in range(pages):
            pltpu.make_async_copy(cache_ref.at[0], ring.at[slot, i], sem.at[slot]).wait()

    @pl.when(b == 0)
    def _():
        for c in range(slots):
            fetch(0, c, c)

    qa = qa_ref[0]
    qr = qr_ref[0]
    rows = qa.shape[0]

    def lane_fold(x, op):
        out = x[:, 0:LANES]
        for j in range(1, x.shape[1] // LANES):
            out = op(out, x[:, j * LANES:(j + 1) * LANES])
        return out

    def score_group(g, mrun):
        c0 = g * group
        for k in range(group):
            wait(lax.rem(c0 + k, slots))
        for k in range(group):
            slot = lax.rem(c0 + k, slots)
            kt = jnp.concatenate([ring[slot, i] for i in range(pages)], axis=1).astype(BF16)
            off = pl.multiple_of((c0 + k) * keys, keys)
            kt_scr[:, pl.ds(off, keys)] = kt
            s = _dot(qa, kt[0:KV_LORA_RANK, :]) + _dot(qr, kt[KV_LORA_RANK:KV_CACHE_DIM, :])
            s_scr[:, pl.ds(off, keys)] = s
            mrun = jnp.maximum(mrun, lane_fold(s, jnp.maximum))
        nxt = c0 + slots
        wrap = nxt >= n_chunks
        bb = jnp.where(wrap, b + 1, b)
        bb = jnp.where(bb < nb, bb, 0)
        cc = jnp.where(wrap, nxt - n_chunks, nxt)
        for k in range(group):
            fetch(bb, cc + k, lax.rem(c0 + k, slots))
        return mrun

    mrun = lax.fori_loop(0, n_chunks // group, score_group, jnp.full((rows, LANES), -jnp.inf, F32))

    @pl.when(b == nb - 1)
    def _():
        for c in range(slots):
            wait(c)

    new = new_ref[0]
    n_new = new.shape[0]
    nc = new[:, 0:KV_LORA_RANK].astype(BF16).astype(F32)
    nr = new[:, KV_LORA_RANK:KV_CACHE_DIM].astype(BF16).astype(F32)
    qaf = qa.astype(F32)
    qrf = qr.astype(F32)
    tok = lax.shift_right_logical(lax.broadcasted_iota(jnp.int32, (rows, 1), 0), int(math.log2(MLA_HEADS)))
    s_new = []
    for t in range(n_new):
        st = (jnp.sum(qaf * nc[t:t + 1, :], axis=-1, keepdims=True)
              + jnp.sum(qrf * nr[t:t + 1, :], axis=-1, keepdims=True))
        s_new.append(jnp.where(tok >= t, st, -jnp.inf))

    m = jnp.max(mrun, axis=-1, keepdims=True)
    for st in s_new:
        m = jnp.maximum(m, st)

    def pv_step(j, carry):
        acc, lrun = carry
        off = pl.multiple_of(j * pv_chunk, pv_chunk)
        p = jnp.exp2((s_scr[:, pl.ds(off, pv_chunk)] - m) * c_exp)
        acc = acc + _dot_nt(p.astype(BF16), kt_scr[0:KV_LORA_RANK, pl.ds(off, pv_chunk)])
        return acc, lrun + lane_fold(p, jnp.add)

    acc, lrun = lax.fori_loop(0, (n_chunks * keys) // pv_chunk, pv_step,
                              (jnp.zeros((rows, KV_LORA_RANK), F32), jnp.zeros((rows, LANES), F32)), unroll=True)
    l = jnp.sum(lrun, axis=-1, keepdims=True)
    for t, st in enumerate(s_new):
        pt = jnp.exp2((st - m) * c_exp)
        l = l + pt
        acc = acc + pt.astype(BF16).astype(F32) * nc[t:t + 1, :]
    ctx_ref[0] = (acc / l).astype(BF16)


def _attn_sample(page_table, qabs, qrope, new_rows, cache_t, *, pages, slots, pv_chunk):
    nb, rows, _ = qabs.shape
    n_pages = page_table.shape[1]
    n_chunks = n_pages // pages
    n_keys = n_pages * PAGE_SIZE
    assert n_chunks * pages == n_pages and n_chunks % slots == 0 and slots % 4 == 0 and n_keys % pv_chunk == 0
    t_new = new_rows.shape[1]
    grid_spec = pltpu.PrefetchScalarGridSpec(
        num_scalar_prefetch=1, grid=(nb,),
        in_specs=[pl.BlockSpec((1, rows, KV_LORA_RANK), lambda b, pt: (b, 0, 0)),
                  pl.BlockSpec((1, rows, QK_ROPE_DIM), lambda b, pt: (b, 0, 0)),
                  pl.BlockSpec((1, t_new, KV_CACHE_DIM), lambda b, pt: (b, 0, 0)),
                  pl.BlockSpec(memory_space=pl.ANY)],
        out_specs=pl.BlockSpec((1, rows, KV_LORA_RANK), lambda b, pt: (b, 0, 0)),
        scratch_shapes=[pltpu.VMEM((slots, pages, KV_CACHE_DIM, PAGE_SIZE), F32),
                        pltpu.SemaphoreType.DMA((slots,)),
                        pltpu.VMEM((KV_CACHE_DIM, n_keys), BF16),
                        pltpu.VMEM((rows, n_keys), F32)])
    return pl.pallas_call(
        functools.partial(_attn_sample_kernel, pages=pages, n_chunks=n_chunks, slots=slots, pv_chunk=pv_chunk),
        grid_spec=grid_spec,
        out_shape=jax.ShapeDtypeStruct((nb, rows, KV_LORA_RANK), BF16),
        compiler_params=_params(("arbitrary",)), name="attn_sample",
    )(page_table, qabs, qrope, new_rows, cache_t)


def _lru_gates(y, wa_ref, ba_ref, wx_ref, bx_ref, lam_ref):
    yb = y.astype(BF16)
    r = jax.nn.sigmoid(_dot(yb, wa_ref[...]) + ba_ref[...])
    i = jax.nn.sigmoid(_dot(yb, wx_ref[...]) + bx_ref[...])
    nl = -lam_ref[...]
    softplus = jnp.maximum(nl, 0.0) + jnp.log1p(jnp.exp(-jnp.abs(nl)))
    log_a = -LRU_C * r * softplus
    a = jnp.exp(log_a)
    th = jnp.tanh(log_a)
    u = (y * i) * jnp.sqrt(-2.0 * th / (1.0 - th))
    return a, u


def _lru_prompt_kernel(lx_ref, lg_ref, cw_ref, cb_ref, wa_ref, ba_ref, wx_ref, bx_ref, lam_ref,
                       rec_ref, ht_ref, xp_scr, a_scr, u_scr, h_scr, *, tt):
    ti = pl.program_id(1)
    sub = 8

    @pl.when(ti == 0)
    def _():
        xp_scr[0:sub, :] = jnp.zeros((sub, LRU_WIDTH), F32)
        h_scr[...] = jnp.zeros((sub, LRU_WIDTH), F32)

    x = lx_ref[...]
    xp_scr[sub:sub + tt, :] = x
    cw = cw_ref[...]
    y = cb_ref[...] + cw[3:4, :] * x
    for j in range(CONV_WIDTH - 1):
        y = y + cw[j:j + 1, :] * xp_scr[sub - 3 + j:sub - 3 + j + tt, :]
    a, u = _lru_gates(y, wa_ref, ba_ref, wx_ref, bx_ref, lam_ref)
    a_scr[...] = a
    u_scr[...] = u
    row = lax.broadcasted_iota(jnp.int32, (sub, LRU_WIDTH), 0)

    def chunk(c, h):
        off = pl.multiple_of(c * sub, sub)
        ca = a_scr[pl.ds(off, sub), :]
        cu = u_scr[pl.ds(off, sub), :]
        for d in (1, 2, 4):
            keep = row >= d
            cu = jnp.where(keep, ca * pltpu.roll(cu, d, 0) + cu, cu)
            ca = jnp.where(keep, ca * pltpu.roll(ca, d, 0), ca)
        hs = ca * h + cu
        u_scr[pl.ds(off, sub), :] = hs
        return hs[sub - 1:sub, :]

    h_last = lax.fori_loop(0, tt // sub, chunk, h_scr[0:1, :])
    h_scr[0:1, :] = h_last
    ht_ref[0] = h_last
    rec_ref[...] = (u_scr[...] * jax.nn.gelu(lg_ref[...])).astype(BF16)
    xp_scr[0:sub, :] = xp_scr[tt:tt + sub, :]


def _lru_prompt(lx, lg, wts, *, batch, seq, tt):
    n = lx.shape[0]
    nt = seq // tt
    blk = pl.BlockSpec((tt, LRU_WIDTH), lambda b, t: (b * nt + t, 0))
    names = ("conv_w", "conv_b", "wa", "ba", "wx", "bx", "lam")
    return pl.pallas_call(
        functools.partial(_lru_prompt_kernel, tt=tt),
        grid=(batch, nt),
        in_specs=[blk, blk] + [_const_spec(wts[k].shape) for k in names],
        out_specs=[blk, pl.BlockSpec((1, 1, LRU_WIDTH), lambda b, t: (b, 0, 0))],
        out_shape=[jax.ShapeDtypeStruct((n, LRU_WIDTH), BF16), jax.ShapeDtypeStruct((batch, 1, LRU_WIDTH), F32)],
        scratch_shapes=[pltpu.VMEM((tt + 8, LRU_WIDTH), F32), pltpu.VMEM((tt, LRU_WIDTH), F32),
                        pltpu.VMEM((tt, LRU_WIDTH), F32), pltpu.VMEM((8, LRU_WIDTH), F32)],
        compiler_params=_params(("parallel", "arbitrary")), name="lru_prompt",
    )(lx, lg, *[wts[k] for k in names])


def _lru_sample_kernel(lx_ref, lg_ref, conv_ref, h0_ref, cw_ref, cb_ref, wa_ref, ba_ref, wx_ref, bx_ref, lam_ref,
                       rec_ref, ht_ref):
    steps = lx_ref.shape[0]
    xs = [conv_ref[j] for j in range(CONV_WIDTH - 1)] + [lx_ref[t] for t in range(steps)]
    cw = cw_ref[...]
    h = h0_ref[...]
    for t in range(steps):
        y = cb_ref[...]
        for j in range(CONV_WIDTH):
            y = y + cw[j:j + 1, :] * xs[t + j]
        a, u = _lru_gates(y, wa_ref, ba_ref, wx_ref, bx_ref, lam_ref)
        h = a * h + u
        rec_ref[t] = (h * jax.nn.gelu(lg_ref[t])).astype(BF16)
    ht_ref[...] = h


def _lru_sample(lx_t, lg_t, conv_t, h0, wts):
    steps, nb, _ = lx_t.shape
    names = ("conv_w", "conv_b", "wa", "ba", "wx", "bx", "lam")
    return pl.pallas_call(
        _lru_sample_kernel,
        out_shape=[jax.ShapeDtypeStruct((steps, nb, LRU_WIDTH), BF16), jax.ShapeDtypeStruct((nb, LRU_WIDTH), F32)],
        compiler_params=pltpu.CompilerParams(vmem_limit_bytes=VMEM_LIMIT), name="lru_sample",
    )(lx_t, lg_t, conv_t, h0, *[wts[k] for k in names])


def _mem_kv_kernel(m_ref, g_ref, wk_ref, wv_ref, k_ref, v_ref):
    m = _rms(m_ref[...], g_ref[...]).astype(BF16)
    k_ref[...] = _dot(m, wk_ref[...])
    v_ref[...] = _dot(m, wv_ref[...])


def _mem_kv(mem, g, wk, wv, *, tn):
    n = mem.shape[0]
    blk = pl.BlockSpec((tn, D_MODEL), lambda i: (i, 0))
    return pl.pallas_call(
        _mem_kv_kernel, grid=(n // tn,),
        in_specs=[blk, _const_spec(g.shape), _const_spec(wk.shape), _const_spec(wv.shape)],
        out_specs=[blk, blk], out_shape=[jax.ShapeDtypeStruct((n, D_MODEL), F32)] * 2,
        compiler_params=_params(("parallel",)), name="mem_kv",
    )(mem, g, wk, wv)


def _mix_out_kernel(x_ref, attn_ref, rec_ref, *rest, absorbed):
    if absorbed:
        wuv_ref, woa_ref, wor_ref, g_ref, wq_ref, x1_ref, qm_ref = rest
        attn = _dot(attn_ref[...], wuv_ref[...]).astype(BF16)
    else:
        woa_ref, wor_ref, g_ref, wq_ref, x1_ref, qm_ref = rest
        attn = attn_ref[...]
    x1 = x_ref[...] + _dot(attn, woa_ref[...]) + _dot(rec_ref[...], wor_ref[...])
    x1_ref[...] = x1
    qm_ref[...] = _dot(_rms(x1, g_ref[...]).astype(BF16), wq_ref[...]).astype(BF16)


def _mix_out(x, attn, rec, wts, *, absorbed, tn):
    n = x.shape[0]
    row_spec = lambda w: pl.BlockSpec((tn, w), lambda i: (i, 0))
    names = (("wuv_bd",) if absorbed else ()) + (("woa_c" if absorbed else "woa_p"), "wor", "g_mem", "w_mem_q")
    return pl.pallas_call(
        functools.partial(_mix_out_kernel, absorbed=absorbed), grid=(n // tn,),
        in_specs=[row_spec(D_MODEL), row_spec(attn.shape[1]), row_spec(LRU_WIDTH)]
        + [_const_spec(wts[k].shape) for k in names],
        out_specs=[row_spec(D_MODEL), row_spec(D_MODEL)],
        out_shape=[jax.ShapeDtypeStruct((n, D_MODEL), F32), jax.ShapeDtypeStruct((n, D_MODEL), BF16)],
        compiler_params=_params(("parallel",)), name="mix_out_s" if absorbed else "mix_out_p",
    )(x, attn, rec, *[wts[k] for k in names])


def _mem_attn_kernel(q_ref, k_ref, v_ref, o_ref):
    q = q_ref[0]
    scale = MEM_HEAD_DIM ** -0.5
    for h in range(MEM_HEADS):
        lo, hi = h * MEM_HEAD_DIM, (h + 1) * MEM_HEAD_DIM
        kh = k_ref[0, :, lo:hi].astype(BF16)
        vh = v_ref[0, :, lo:hi].astype(BF16)
        s = _dot_nt(q[:, lo:hi], kh) * scale
        e = jnp.exp(s - jnp.max(s, axis=-1, keepdims=True))
        p = e / jnp.sum(e, axis=-1, keepdims=True)
        o_ref[0, :, lo:hi] = _dot(p.astype(BF16), vh).astype(BF16)


def _mem_attn(q, k, v, *, tq, name):
    nb, sq, _ = q.shape
    n_mem = k.shape[1]
    qblk = pl.BlockSpec((1, tq, D_MODEL), lambda b, i: (b, i, 0))
    kblk = pl.BlockSpec((1, n_mem, D_MODEL), lambda b, i: (b, 0, 0))
    return pl.pallas_call(
        _mem_attn_kernel, grid=(nb, sq // tq),
        in_specs=[qblk, kblk, kblk], out_specs=qblk,
        out_shape=jax.ShapeDtypeStruct((nb, sq, D_MODEL), BF16),
        compiler_params=_params(("parallel", "arbitrary")), name=name,
    )(q, k, v)


def _mem_attn_cache_kernel(q_ref, k_ref, v_ref, o_ref, *, steps):
    n_mem = k_ref.shape[1]
    for g in range(q_ref.shape[0]):
        q = q_ref[g]
        k2 = k_ref[g].reshape(n_mem * MEM_HEADS, MEM_HEAD_DIM).astype(BF16)
        v2 = v_ref[g].reshape(n_mem * MEM_HEADS, MEM_HEAD_DIM).astype(BF16)
        s = _dot_nt(q, k2) * (MEM_HEAD_DIM ** -0.5)
        row_head = lax.shift_right_logical(lax.broadcasted_iota(jnp.int32, s.shape, 0), int(math.log2(steps)))
        col_head = lax.bitwise_and(lax.broadcasted_iota(jnp.int32, s.shape, 1), MEM_HEADS - 1)
        s = jnp.where(row_head == col_head, s, -jnp.inf)
        e = jnp.exp(s - jnp.max(s, axis=-1, keepdims=True))
        p = e / jnp.sum(e, axis=-1, keepdims=True)
        o_ref[g] = _dot(p.astype(BF16), v2).astype(BF16)


def _mem_attn_cache(q, k, v, *, steps, per_step=4):
    nb, rows, _ = q.shape
    qblk = pl.BlockSpec((per_step, rows, MEM_HEAD_DIM), lambda b: (b, 0, 0))
    kblk = pl.BlockSpec((per_step,) + k.shape[1:], lambda b: (b, 0, 0, 0))
    return pl.pallas_call(
        functools.partial(_mem_attn_cache_kernel, steps=steps), grid=(nb // per_step,),
        in_specs=[qblk, kblk, kblk], out_specs=qblk,
        out_shape=jax.ShapeDtypeStruct(q.shape, BF16),
        compiler_params=_params(("parallel",)), name="mem_attn_s",
    )(q, k, v)


def _tail_kernel(x1_ref, o_ref, wo_ref, g_ref, wup_ref, wdn_ref, gf_ref, y_ref, *, ff_chunk):
    x2 = x1_ref[...] + _dot(o_ref[...], wo_ref[...])
    h = _rms(x2, g_ref[...]).astype(BF16)
    acc = x2
    for c in range(D_FF // ff_chunk):
        lo, hi = c * ff_chunk, (c + 1) * ff_chunk
        up = jnp.maximum(_dot(h, wup_ref[:, lo:hi]), 0.0)
        acc = acc + _dot((up * up).astype(BF16), wdn_ref[lo:hi, :])
    y_ref[...] = _rms(acc, gf_ref[...])


def _tail(x1, o, wts, *, tn, name):
    n = x1.shape[0]
    blk = pl.BlockSpec((tn, D_MODEL), lambda i: (i, 0))
    names = ("w_mem_o", "g_mlp", "w_up", "w_down", "g_final")
    return pl.pallas_call(
        functools.partial(_tail_kernel, ff_chunk=1024), grid=(n // tn,),
        in_specs=[blk, blk] + [_const_spec(wts[k].shape) for k in names],
        out_specs=blk, out_shape=jax.ShapeDtypeStruct((n, D_MODEL), F32),
        compiler_params=_params(("parallel",)), name=name,
    )(x1, o, *[wts[k] for k in names])


def _rot_half_cols(w):
    half = QK_ROPE_DIM // 2
    return jnp.concatenate([-w[..., half:], w[..., :half]], axis=-1)


def _prep_weights(w_in, q_norm_g, w_q_up, kv_norm_g, w_uk, w_uv, conv_w, conv_b, lru_w_a, lru_b_a, lru_w_x, lru_b_x,
                  lru_lambda, w_out, norm_mem_g, mem_norm_g, w_mem_q, w_mem_k, w_mem_v, w_mem_o, norm_mlp_g,
                  w_up, w_down, final_norm_g):
    w = {}
    s1 = Q_LORA_RANK
    s2 = s1 + KV_LORA_RANK
    s3 = s2 + QK_ROPE_DIM
    s4 = s3 + LRU_WIDTH
    wkr = w_in[:, s2:s3]
    lane_pad = jnp.zeros((D_MODEL, LANES - QK_ROPE_DIM), F32)
    w["win"] = jnp.concatenate([w_in[:, :s1], w_in[:, s1:s2], w_in[:, s3:s4], w_in[:, s4:],
                                wkr, lane_pad, _rot_half_cols(wkr), lane_pad], axis=1).astype(BF16)
    w["qg"] = q_norm_g.reshape(1, -1)
    w["kvg"] = kv_norm_g.reshape(1, -1)
    wq3 = w_q_up.reshape(Q_LORA_RANK, MLA_HEADS, QK_NOPE_DIM + QK_ROPE_DIM)
    nope, ropew = wq3[..., :QK_NOPE_DIM], wq3[..., QK_NOPE_DIM:]
    tail_pad = jnp.zeros((Q_LORA_RANK, MLA_HEADS, HEAD_PAD - QK_NOPE_DIM - QK_ROPE_DIM), F32)
    w["wq"] = jnp.concatenate([nope, ropew, tail_pad], -1).reshape(Q_LORA_RANK, -1).astype(BF16)
    w["wqr"] = jnp.concatenate([jnp.zeros_like(nope), _rot_half_cols(ropew), tail_pad], -1
                               ).reshape(Q_LORA_RANK, -1).astype(BF16)
    head_pad = jnp.zeros((KV_LORA_RANK, MLA_HEADS, HEAD_PAD - QK_NOPE_DIM), F32)
    w["wuk"] = jnp.concatenate([w_uk, head_pad], -1).reshape(KV_LORA_RANK, -1).astype(BF16)
    w["wuv_t"] = jnp.concatenate([w_uv, head_pad], -1).reshape(KV_LORA_RANK, -1).T.astype(BF16)
    src = jnp.arange(LANES)[:, None]
    dst = jnp.arange(MLA_HEADS * HEAD_PAD)[None, :]
    w["tile"] = jnp.logical_and(src < QK_ROPE_DIM, dst % HEAD_PAD == src + QK_NOPE_DIM).astype(BF16)
    eye = jnp.eye(MLA_HEADS, dtype=F32)
    uk_t = jnp.transpose(w_uk, (1, 2, 0))
    uk_t = jnp.concatenate([uk_t, jnp.zeros((MLA_HEADS, HEAD_PAD - QK_NOPE_DIM, KV_LORA_RANK), F32)], 1)
    w["wabs"] = jnp.einsum("hdc,hg->hdgc", uk_t, eye).reshape(MLA_HEADS * HEAD_PAD, -1).astype(BF16)
    w["wuv_bd"] = jnp.einsum("chd,hg->hcgd", w_uv, eye).reshape(MLA_HEADS * KV_LORA_RANK, -1).astype(BF16)
    n_attn = MLA_HEADS * V_HEAD_DIM
    woa = w_out[:n_attn].reshape(MLA_HEADS, V_HEAD_DIM, D_MODEL)
    w["woa_p"] = jnp.concatenate([woa, jnp.zeros((MLA_HEADS, HEAD_PAD - V_HEAD_DIM, D_MODEL), F32)], 1
                                 ).reshape(-1, D_MODEL).astype(BF16)
    w["woa_c"] = w_out[:n_attn].astype(BF16)
    w["wor"] = w_out[n_attn:].astype(BF16)
    w["conv_w"] = conv_w
    w["conv_b"] = conv_b.reshape(1, -1)
    eye_l = jnp.eye(LRU_BLOCKS, dtype=F32)
    w["wa"] = jnp.einsum("nde,nm->ndme", lru_w_a, eye_l).reshape(LRU_WIDTH, LRU_WIDTH).astype(BF16)
    w["wx"] = jnp.einsum("nde,nm->ndme", lru_w_x, eye_l).reshape(LRU_WIDTH, LRU_WIDTH).astype(BF16)
    w["ba"] = lru_b_a.reshape(1, -1)
    w["bx"] = lru_b_x.reshape(1, -1)
    w["lam"] = lru_lambda.reshape(1, -1)
    w["g_mem"] = norm_mem_g.reshape(1, -1)
    w["g_memkv"] = mem_norm_g.reshape(1, -1)
    w["w_mem_q"] = w_mem_q.astype(BF16)
    w["w_mem_k"] = w_mem_k.astype(BF16)
    w["w_mem_v"] = w_mem_v.astype(BF16)
    w["w_mem_o"] = w_mem_o.astype(BF16)
    w["g_mlp"] = norm_mlp_g.reshape(1, -1)
    w["w_up"] = w_up.astype(BF16)
    w["w_down"] = w_down.astype(BF16)
    w["g_final"] = final_norm_g.reshape(1, -1)
    return w


def _rope_tables(pos):
    inv = ROPE_THETA ** (-jnp.arange(0, QK_ROPE_DIM, 2, dtype=F32) / QK_ROPE_DIM)
    ang = pos.astype(F32)[:, None] * inv[None, :]
    cos, sin = jnp.cos(ang), jnp.sin(ang)
    n = pos.shape[0]
    ones = jnp.ones((n, QK_NOPE_DIM), F32)
    z = lambda k: jnp.zeros((n, k), F32)
    q_tail = HEAD_PAD - QK_NOPE_DIM - QK_ROPE_DIM
    cq = jnp.concatenate([ones, cos, cos, z(q_tail)], 1)
    sq = jnp.concatenate([z(QK_NOPE_DIM), sin, sin, z(q_tail)], 1)
    ck = jnp.concatenate([cos, cos, z(LANES - QK_ROPE_DIM)], 1)
    sk = jnp.concatenate([sin, sin, z(LANES - QK_ROPE_DIM)], 1)
    return cq, sq, ck, sk


def kernel(x_prompt, x_sample, cache_mla, cache_mem_k, cache_mem_v, state_lru_h, state_conv, page_table, mem_prompt, norm_mix_g, w_in, q_norm_g, w_q_up, kv_norm_g, w_uk, w_uv, conv_w, conv_b, lru_w_a, lru_b_a, lru_w_x, lru_b_x, lru_lambda, w_out, norm_mem_g, mem_norm_g, w_mem_q, w_mem_k, w_mem_v, w_mem_o, norm_mlp_g, w_up, w_down, final_norm_g):
    B, S, _ = x_prompt.shape
    Bd, T, _ = x_sample.shape
    assert w_in.shape[0] == 1, "single layer"
    wts = _prep_weights(w_in[0], q_norm_g[0], w_q_up[0], kv_norm_g[0], w_uk[0], w_uv[0], conv_w[0], conv_b[0],
                        lru_w_a[0], lru_b_a[0], lru_w_x[0], lru_b_x[0], lru_lambda[0], w_out[0], norm_mem_g[0],
                        mem_norm_g[0], w_mem_q[0], w_mem_k[0], w_mem_v[0], w_mem_o[0], norm_mlp_g[0],
                        w_up[0], w_down[0], final_norm_g)
    g_mix = norm_mix_g[0].reshape(1, -1)
    tn = 512

    xp = x_prompt.reshape(B * S, D_MODEL)
    q_p, rows_p, lx_p, lg_p, k_p, v_p = _in_proj(xp, g_mix, wts, _rope_tables(jnp.arange(S)),
                                                 prompt=True, tn=tn, pos_blocks=S // tn)
    attn_p = _attn_prompt(q_p, k_p, v_p, batch=B, seq=S, tq=512)
    rec_p, ht_p = _lru_prompt(lx_p, lg_p, wts, batch=B, seq=S, tt=256)
    x1_p, qm_p = _mix_out(xp, attn_p, rec_p, wts, absorbed=False, tn=tn)
    n_mem = mem_prompt.shape[1]
    mk_p, mv_p = _mem_kv(mem_prompt.reshape(B * n_mem, D_MODEL), wts["g_memkv"], wts["w_mem_k"], wts["w_mem_v"], tn=tn)
    o_p = _mem_attn(qm_p.reshape(B, S, D_MODEL), mk_p.reshape(B, n_mem, D_MODEL), mv_p.reshape(B, n_mem, D_MODEL),
                    tq=512, name="mem_attn_p")
    y_p = _tail(x1_p, o_p.reshape(B * S, D_MODEL), wts, tn=tn, name="tail_p")

    ns = Bd * T
    xs = x_sample.reshape(ns, D_MODEL)
    tabs_s = tuple(jnp.tile(t, (Bd, 1)) for t in _rope_tables(PAST_LEN + jnp.arange(T)))
    q_s, rows_s, lx_s, lg_s, qabs_s = _in_proj(xs, g_mix, wts, tabs_s, prompt=False, tn=ns, pos_blocks=1)
    rows_q = T * MLA_HEADS
    qr_s = q_s.reshape(ns * MLA_HEADS, HEAD_PAD)[:, QK_NOPE_DIM:QK_NOPE_DIM + QK_ROPE_DIM]
    ctx_s = _attn_sample(page_table, qabs_s.reshape(Bd, rows_q, KV_LORA_RANK), qr_s.reshape(Bd, rows_q, QK_ROPE_DIM),
                         rows_s.reshape(Bd, T, KV_CACHE_DIM), jnp.swapaxes(cache_mla[0], 1, 2),
                         pages=8, slots=16, pv_chunk=2048)
    to_time_major = lambda a: jnp.transpose(a.reshape(Bd, -1, LRU_WIDTH), (1, 0, 2))
    rec_t, ht_s = _lru_sample(to_time_major(lx_s), to_time_major(lg_s), to_time_major(state_conv[0]),
                              state_lru_h[0], wts)
    rec_s = jnp.transpose(rec_t, (1, 0, 2)).reshape(ns, LRU_WIDTH)
    x1_s, qm_s = _mix_out(xs, ctx_s.reshape(ns, MLA_HEADS * KV_LORA_RANK), rec_s, wts, absorbed=True, tn=ns)
    head_major = lambda a: jnp.transpose(a.reshape(Bd, T, MEM_HEADS, MEM_HEAD_DIM), (0, 2, 1, 3))
    o_s = _mem_attn_cache(head_major(qm_s).reshape(Bd, MEM_HEADS * T, MEM_HEAD_DIM), cache_mem_k[0], cache_mem_v[0],
                          steps=T)
    o_s = jnp.transpose(o_s.reshape(Bd, MEM_HEADS, T, MEM_HEAD_DIM), (0, 2, 1, 3))
    y_s = _tail(x1_s, o_s.reshape(ns, D_MODEL), wts, tn=ns, name="tail_s")

    lx_p3 = lx_p.reshape(B, S, LRU_WIDTH)
    lx_s3 = lx_s.reshape(Bd, T, LRU_WIDTH)
    keep = CONV_WIDTH - 1
    return (y_p.reshape(B, S, D_MODEL), y_s.reshape(Bd, T, D_MODEL),
            rows_p.reshape(1, B, S, KV_CACHE_DIM), rows_s.reshape(1, Bd, T, KV_CACHE_DIM),
            mk_p.reshape(1, B, n_mem, MEM_HEADS, MEM_HEAD_DIM), mv_p.reshape(1, B, n_mem, MEM_HEADS, MEM_HEAD_DIM),
            ht_p.reshape(1, B, LRU_WIDTH), ht_s.reshape(1, Bd, LRU_WIDTH),
            lx_p3[:, S - keep:].reshape(1, B, keep, LRU_WIDTH), lx_s3[:, T - keep:].reshape(1, Bd, keep, LRU_WIDTH))
```

```python
import functools
import math

import jax
import jax.numpy as jnp
import numpy as np
from jax import lax
from jax.experimental import pallas as pl
from jax.experimental.pallas import tpu as pltpu

F32 = jnp.float32
BF16 = jnp.bfloat16

D_MODEL = 1024
PAST_LEN = 16384
PAGE_SIZE = 128
MLA_HEADS = 8
QK_NOPE_DIM = 64
QK_ROPE_DIM = 32
V_HEAD_DIM = 64
Q_LORA_RANK = 384
KV_LORA_RANK = 256
KV_CACHE_DIM = KV_LORA_RANK + QK_ROPE_DIM
ROPE_THETA = 10000.0
MLA_SCALE = (QK_NOPE_DIM + QK_ROPE_DIM) ** -0.5
LRU_WIDTH = D_MODEL // 2
LRU_BLOCKS = 8
LRU_C = 8.0
CONV_WIDTH = 4
MEM_HEADS = 4
MEM_HEAD_DIM = D_MODEL // MEM_HEADS
D_FF = 4 * D_MODEL
EPS = 1e-6

LANES = 128
HEAD_PAD = LANES
LOG2E = math.log2(math.e)
VMEM_LIMIT = 48 * 1024 * 1024

_C_QLAT = 0
_C_KV = _C_QLAT + Q_LORA_RANK
_C_LX = _C_KV + KV_LORA_RANK
_C_LG = _C_LX + LRU_WIDTH
_C_KR = _C_LG + LRU_WIDTH
_KR_ROT_LANE = LANES // 2
IN_EXT = _C_KR + LANES


def _rms(x, g):
    ms = jnp.mean(x * x, axis=-1, keepdims=True)
    return x * lax.rsqrt(ms + EPS) * g


def _dot(a, b):
    return jnp.dot(a, b, preferred_element_type=F32)


def _dot_nt(a, b):
    return lax.dot_general(a, b, (((1,), (1,)), ((), ())), preferred_element_type=F32)


def _const_spec(shape):
    nd = len(shape)
    return pl.BlockSpec(shape, lambda *_: (0,) * nd, pipeline_mode=pl.Buffered(1))


def _params(sem, flags=None):
    return pltpu.CompilerParams(dimension_semantics=sem, vmem_limit_bytes=VMEM_LIMIT, flags=flags)


def _in_proj_kernel(x_ref, g_ref, win_ref, qg_ref, wq_ref, wqr_ref, kvg_ref,
                    cq_ref, sq_ref, cks_ref, *rest, prompt):
    if prompt:
        wuk_ref, wuv_ref, q_ref, rows_ref, lx_ref, lg_ref, k_ref, v_ref = rest
    else:
        wabs_ref, q_ref, rows_ref, lx_ref, lg_ref, qabs_ref = rest
    h = _rms(x_ref[...], g_ref[...]).astype(BF16)
    z = _dot_nt(h, win_ref[...])
    qa = _rms(z[:, _C_QLAT:_C_KV], qg_ref[...]).astype(BF16)
    q1 = _dot(qa, wq_ref[...])
    q2c = _dot(qa, wqr_ref[...])
    per_tile = LANES // QK_ROPE_DIM
    q2 = []
    for hd in range(MLA_HEADS):
        blk = q2c[:, (hd // per_tile) * LANES:(hd // per_tile + 1) * LANES]
        shift = (QK_NOPE_DIM - QK_ROPE_DIM * (hd % per_tile)) % LANES
        q2.append(pltpu.roll(blk, shift, 1) if shift else blk)
    cq = jnp.tile(cq_ref[...], (1, MLA_HEADS))
    sq = jnp.tile(sq_ref[...], (1, MLA_HEADS))
    qb = (q1 * cq + jnp.concatenate(q2, axis=1) * sq).astype(BF16)
    q_ref[...] = qb
    c = _rms(z[:, _C_KV:_C_LX], kvg_ref[...])
    t = z[:, _C_KR:IN_EXT] * cks_ref[...]
    kr = t + pltpu.roll(t, _KR_ROT_LANE, 1)
    rows_ref[:, 0:KV_LORA_RANK] = c
    rows_ref[:, KV_LORA_RANK:KV_CACHE_DIM] = kr[:, 0:QK_ROPE_DIM]
    lx_ref[...] = z[:, _C_LX:_C_LG]
    lg_ref[...] = z[:, _C_LG:_C_KR]
    if prompt:
        cb = c.astype(BF16)
        lane = lax.broadcasted_iota(jnp.int32, kr.shape, 1)
        kr_hi = jnp.where(lane >= QK_NOPE_DIM, kr, 0.0)
        k_ref[...] = (_dot(cb, wuk_ref[...]) + jnp.tile(kr_hi, (1, MLA_HEADS))).astype(BF16)
        v_ref[...] = _dot_nt(wuv_ref[...], cb).astype(BF16)
    else:
        qabs_ref[...] = _dot(qb, wabs_ref[...]).astype(BF16)


def _in_proj(x, g, wts, tabs, *, prompt, tn, pos_blocks):
    n = x.shape[0]
    cq, sq, cks = tabs
    assert QK_NOPE_DIM == _KR_ROT_LANE and QK_NOPE_DIM + QK_ROPE_DIM <= HEAD_PAD
    tab_spec = pl.BlockSpec((tn, LANES), lambda i: (i % pos_blocks, 0))
    row_spec = lambda w: pl.BlockSpec((tn, w), lambda i: (i, 0))
    hp = MLA_HEADS * HEAD_PAD
    in_specs = [row_spec(D_MODEL), _const_spec(g.shape)]
    in_specs += [_const_spec(wts[k].shape) for k in ("win", "qg", "wq", "wqr", "kvg")]
    in_specs += [tab_spec] * 3
    args = [x, g, wts["win"], wts["qg"], wts["wq"], wts["wqr"], wts["kvg"], cq, sq, cks]
    out_shape = [jax.ShapeDtypeStruct((n, hp), BF16), jax.ShapeDtypeStruct((n, KV_CACHE_DIM), F32),
                 jax.ShapeDtypeStruct((n, LRU_WIDTH), F32), jax.ShapeDtypeStruct((n, LRU_WIDTH), F32)]
    out_specs = [row_spec(hp), row_spec(KV_CACHE_DIM), row_spec(LRU_WIDTH), row_spec(LRU_WIDTH)]
    if prompt:
        extra = ("wuk", "wuv_t")
        n_v = MLA_HEADS * V_HEAD_DIM
        out_shape += [jax.ShapeDtypeStruct((n, hp), BF16), jax.ShapeDtypeStruct((n_v, n), BF16)]
        out_specs += [row_spec(hp), pl.BlockSpec((n_v, tn), lambda i: (0, i))]
    else:
        extra = ("wabs",)
        out_shape += [jax.ShapeDtypeStruct((n, MLA_HEADS * KV_LORA_RANK), BF16)]
        out_specs += [row_spec(MLA_HEADS * KV_LORA_RANK)]
    in_specs += [_const_spec(wts[k].shape) for k in extra]
    args += [wts[k] for k in extra]
    return pl.pallas_call(
        functools.partial(_in_proj_kernel, prompt=prompt),
        grid=(n // tn,), in_specs=in_specs, out_specs=out_specs, out_shape=out_shape,
        compiler_params=_params(("parallel",)), name="in_proj_p" if prompt else "in_proj_s",
    )(*args)


HEADS_PER_STEP = 2


def _attn_prompt_kernel(q_ref, k_ref, vt_ref, o_ref, s_scr, *, tq, seq):
    half = tq // 2
    col_max = lambda x: jnp.max(x, axis=0, keepdims=True)
    col_sum = lambda x: jnp.sum(x, axis=0, keepdims=True)
    tri_key = lax.broadcasted_iota(jnp.int32, (half, half), 0)
    tri_qry = lax.broadcasted_iota(jnp.int32, (half, half), 1)
    causal = lambda x: jnp.where(tri_key <= tri_qry, x, -jnp.inf)
    heads = range(HEADS_PER_STEP)
    hcols = [slice(h * HEAD_PAD, (h + 1) * HEAD_PAD) for h in heads]
    vrows = [slice(h * V_HEAD_DIM, (h + 1) * V_HEAD_DIM) for h in heads]
    for qi in range(seq // tq):
        rows = slice(qi * tq, (qi + 1) * tq)
        lo, mid, hi = qi * tq, qi * tq + half, (qi + 1) * tq
        out = []
        for h in heads:
            q = q_ref[rows, hcols[h]]
            m = jnp.full((1, tq), -jnp.inf, F32)
            for j in range(qi):
                keys = slice(j * tq, (j + 1) * tq)
                st = _dot_nt(k_ref[keys, hcols[h]], q)
                s_scr[h, keys, :] = st
                m = jnp.maximum(m, col_max(st))
            sa = _dot_nt(k_ref[lo:mid, hcols[h]], q)
            sb = causal(_dot_nt(k_ref[mid:hi, hcols[h]], q[half:, :]))
            sa = jnp.concatenate([causal(sa[:, :half]), sa[:, half:]], axis=1)
            s_scr[h, lo:mid, :] = sa
            s_scr[h, mid:hi, half:] = sb
            m = jnp.maximum(m, col_max(sa))
            m = jnp.maximum(m, jnp.concatenate([jnp.full((1, half), -jnp.inf, F32), col_max(sb)], axis=1))
            l = jnp.zeros((1, tq), F32)
            acc = jnp.zeros((V_HEAD_DIM, tq), F32)
            for j in range(qi):
                keys = slice(j * tq, (j + 1) * tq)
                pt = jnp.exp2(s_scr[h, keys, :] - m)
                l = l + col_sum(pt)
                acc = acc + _dot(vt_ref[vrows[h], keys], pt.astype(BF16))
            pa = jnp.exp2(s_scr[h, lo:mid, :] - m)
            pb = jnp.exp2(s_scr[h, mid:hi, half:] - m[:, half:])
            l = l + col_sum(pa) + jnp.concatenate([jnp.zeros((1, half), F32), col_sum(pb)], axis=1)
            late = _dot(vt_ref[vrows[h], mid:hi], pb.astype(BF16))
            acc = (acc + _dot(vt_ref[vrows[h], lo:mid], pa.astype(BF16))
                   + jnp.concatenate([jnp.zeros((V_HEAD_DIM, half), F32), late], axis=1))
            out.append(acc / l)
        o_ref[rows, :] = jnp.transpose(jnp.concatenate(out, axis=0)).astype(BF16)


def _attn_prompt(q, k, vt, *, batch, seq, tq):
    n = q.shape[0]
    assert seq % tq == 0 and HEADS_PER_STEP * V_HEAD_DIM == LANES
    row_blk = pl.BlockSpec((seq, HEADS_PER_STEP * HEAD_PAD), lambda b, h: (b, h))
    return pl.pallas_call(
        functools.partial(_attn_prompt_kernel, tq=tq, seq=seq),
        grid=(batch, MLA_HEADS // HEADS_PER_STEP),
        in_specs=[row_blk, row_blk, pl.BlockSpec((LANES, seq), lambda b, h: (h, b))],
        out_specs=pl.BlockSpec((seq, LANES), lambda b, h: (b, h)),
        out_shape=jax.ShapeDtypeStruct((n, MLA_HEADS * V_HEAD_DIM), BF16),
        scratch_shapes=[pltpu.VMEM((HEADS_PER_STEP, seq, tq), F32)],
        compiler_params=_params(("parallel", "parallel")),
        name="attn_prompt",
    )(q, k, vt)


def _attn_sample_kernel(pt_ref, qa_ref, qr_ref, new_ref, cache_ref, ctx_ref, ring, sem, kt_scr, s_scr,
                        *, pages, n_chunks, slots, pv_chunk):
    b = pl.program_id(0)
    nb = pl.num_programs(0)
    keys = pages * PAGE_SIZE
    group = 4

    def fetch(bb, chunk, slot):
        for i in range(pages):
            pid = pt_ref[bb, chunk * pages + i]
            pltpu.make_async_copy(cache_ref.at[pid], ring.at[slot, i], sem.at[slot]).start()

    def wait(slot):
        for i in range(pages):
            pltpu.make_async_copy(cache_ref.at[0], ring.at[slot, i], sem.at[slot]).wait()

    @pl.when(b == 0)
    def _():
        for c in range(slots):
            fetch(0, c, c)

    qa = qa_ref[0]
    qr = qr_ref[0]
    rows = qa.shape[0]

    def lane_fold(x, op):
        out = x[:, 0:LANES]
        for j in range(1, x.shape[1] // LANES):
            out = op(out, x[:, j * LANES:(j + 1) * LANES])
        return out

    def score_group(g, mrun):
        c0 = g * group
        for k in range(group):
            wait(lax.rem(c0 + k, slots))
        for k in range(group):
            slot = lax.rem(c0 + k, slots)
            kt = jnp.concatenate([ring[slot, i] for i in range(pages)], axis=1).astype(BF16)
            off = pl.multiple_of((c0 + k) * keys, keys)
            kt_scr[:, pl.ds(off, keys)] = kt
            s = _dot(qa, kt[0:KV_LORA_RANK, :]) + _dot(qr, kt[KV_LORA_RANK:KV_CACHE_DIM, :])
            s_scr[:, pl.ds(off, keys)] = s
            mrun = jnp.maximum(mrun, lane_fold(s, jnp.maximum))
        nxt = c0 + slots
        wrap = nxt >= n_chunks
        bb = jnp.where(wrap, b + 1, b)
        bb = jnp.where(bb < nb, bb, 0)
        cc = jnp.where(wrap, nxt - n_chunks, nxt)
        for k in range(group):
            fetch(bb, cc + k, lax.rem(c0 + k, slots))
        return mrun

    mrun = lax.fori_loop(0, n_chunks // group, score_group, jnp.full((rows, LANES), -jnp.inf, F32))

    @pl.when(b == nb - 1)
    def _():
        for c in range(slots):
            wait(c)

    new = new_ref[0]
    n_new = new.shape[0]
    nc = new[:, 0:KV_LORA_RANK].astype(BF16).astype(F32)
    nr = new[:, KV_LORA_RANK:KV_CACHE_DIM].astype(BF16).astype(F32)
    qaf = qa.astype(F32)
    qrf = qr.astype(F32)
    tok = lax.shift_right_logical(lax.broadcasted_iota(jnp.int32, (rows, 1), 0), int(math.log2(MLA_HEADS)))
    s_new = []
    for t in range(n_new):
        st = (jnp.sum(qaf * nc[t:t + 1, :], axis=-1, keepdims=True)
              + jnp.sum(qrf * nr[t:t + 1, :], axis=-1, keepdims=True))
        s_new.append(jnp.where(tok >= t, st, -jnp.inf))

    m = jnp.max(mrun, axis=-1, keepdims=True)
    for st in s_new:
        m = jnp.maximum(m, st)

    def pv_step(j, carry):
        acc, lrun = carry
        off = pl.multiple_of(j * pv_chunk, pv_chunk)
        p = jnp.exp2(s_scr[:, pl.ds(off, pv_chunk)] - m)
        acc = acc + _dot_nt(p.astype(BF16), kt_scr[0:KV_LORA_RANK, pl.ds(off, pv_chunk)])
        return acc, lrun + lane_fold(p, jnp.add)

    acc, lrun = lax.fori_loop(0, (n_chunks * keys) // pv_chunk, pv_step,
                              (jnp.zeros((rows, KV_LORA_RANK), F32), jnp.zeros((rows, LANES), F32)), unroll=True)
    l = jnp.sum(lrun, axis=-1, keepdims=True)
    for t, st in enumerate(s_new):
        pt = jnp.exp2(st - m)
        l = l + pt
        acc = acc + pt.astype(BF16).astype(F32) * nc[t:t + 1, :]
    ctx_ref[0] = (acc / l).astype(BF16)


def _attn_sample(page_table, qabs, qrope, new_rows, cache_t, *, pages, slots, pv_chunk):
    nb, rows, _ = qabs.shape
    n_pages = page_table.shape[1]
    n_chunks = n_pages // pages
    n_keys = n_pages * PAGE_SIZE
    assert n_chunks * pages == n_pages and n_chunks % slots == 0 and slots % 4 == 0 and n_keys % pv_chunk == 0
    t_new = new_rows.shape[1]
    grid_spec = pltpu.PrefetchScalarGridSpec(
        num_scalar_prefetch=1, grid=(nb,),
        in_specs=[pl.BlockSpec((1, rows, KV_LORA_RANK), lambda b, pt: (b, 0, 0)),
                  pl.BlockSpec((1, rows, QK_ROPE_DIM), lambda b, pt: (b, 0, 0)),
                  pl.BlockSpec((1, t_new, KV_CACHE_DIM), lambda b, pt: (b, 0, 0)),
                  pl.BlockSpec(memory_space=pl.ANY)],
        out_specs=pl.BlockSpec((1, rows, KV_LORA_RANK), lambda b, pt: (b, 0, 0)),
        scratch_shapes=[pltpu.VMEM((slots, pages, KV_CACHE_DIM, PAGE_SIZE), F32),
                        pltpu.SemaphoreType.DMA((slots,)),
                        pltpu.VMEM((KV_CACHE_DIM, n_keys), BF16),
                        pltpu.VMEM((rows, n_keys), F32)])
    return pl.pallas_call(
        functools.partial(_attn_sample_kernel, pages=pages, n_chunks=n_chunks, slots=slots, pv_chunk=pv_chunk),
        grid_spec=grid_spec,
        out_shape=jax.ShapeDtypeStruct((nb, rows, KV_LORA_RANK), BF16),
        compiler_params=_params(("arbitrary",)), name="attn_sample",
    )(page_table, qabs, qrope, new_rows, cache_t)


def _lru_gates(y, wa_ref, ba_ref, wx_ref, bx_ref, lam_ref):
    yb = y.astype(BF16)
    r = jax.nn.sigmoid(_dot(yb, wa_ref[...]) + ba_ref[...])
    i = jax.nn.sigmoid(_dot(yb, wx_ref[...]) + bx_ref[...])
    nl = -lam_ref[...]
    softplus = jnp.maximum(nl, 0.0) + jnp.log1p(jnp.exp(-jnp.abs(nl)))
    log_a = -LRU_C * r * softplus
    a = jnp.exp(log_a)
    th = jnp.tanh(log_a)
    u = (y * i) * jnp.sqrt(-2.0 * th / (1.0 - th))
    return a, u


def _lru_prompt_kernel(lx_ref, lg_ref, cw_ref, cb_ref, wa_ref, ba_ref, wx_ref, bx_ref, lam_ref,
                       rec_ref, ht_ref, xp_scr, a_scr, u_scr, hs_scr, h_scr, *, tt, batch, pitch):
    ti = pl.program_id(0)
    sub = 8

    @pl.when(ti == 0)
    def _():
        xp_scr[:, 0:sub, :] = jnp.zeros((batch, sub, LRU_WIDTH), F32)
        h_scr[...] = jnp.zeros((batch, LRU_WIDTH), F32)

    cw = cw_ref[...]
    ys = []
    for b in range(batch):
        x = lx_ref[b]
        xp_scr[b, sub:sub + tt, :] = x
        y = cb_ref[...] + cw[3:4, :] * x
        for j in range(CONV_WIDTH - 1):
            y = y + cw[j:j + 1, :] * xp_scr[b, sub - 3 + j:sub - 3 + j + tt, :]
        ys.append(y)
    a, u = _lru_gates(jnp.concatenate(ys, axis=0), wa_ref, ba_ref, wx_ref, bx_ref, lam_ref)
    n_col = LRU_WIDTH // LANES
    cols = [slice(c * LANES, (c + 1) * LANES) for c in range(n_col)]
    for c in range(n_col):
        for b in range(batch):
            a_scr[c, b * pitch:b * pitch + tt, :] = a[b * tt:(b + 1) * tt, cols[c]]
            u_scr[c, b * pitch:b * pitch + tt, :] = u[b * tt:(b + 1) * tt, cols[c]]

    def step(t, h):
        rows = pl.ds(t, batch, stride=pitch)
        h = tuple(a_scr[c, rows, :] * h[c] + u_scr[c, rows, :] for c in range(n_col))
        for c in range(n_col):
            hs_scr[c, rows, :] = h[c]
        return h

    h0 = h_scr[...]
    h_last = lax.fori_loop(0, tt, step, tuple(h0[:, cols[c]] for c in range(n_col)), unroll=8)
    h_last = jnp.concatenate(h_last, axis=1)
    h_scr[...] = h_last
    ht_ref[...] = h_last
    for b in range(batch):
        hs = jnp.concatenate([hs_scr[c, b * pitch:b * pitch + tt, :] for c in range(n_col)], axis=1)
        rec_ref[b] = (hs * jax.nn.gelu(lg_ref[b])).astype(BF16)
        xp_scr[b, 0:sub, :] = xp_scr[b, tt:tt + sub, :]


def _lru_prompt(lx, lg, wts, *, batch, seq, tt):
    n = lx.shape[0]
    pitch = tt + 8
    blk = pl.BlockSpec((batch, tt, LRU_WIDTH), lambda t: (0, t, 0))
    names = ("conv_w", "conv_b", "wa", "ba", "wx", "bx", "lam")
    to3 = lambda a: a.reshape(batch, seq, LRU_WIDTH)
    rec, ht = pl.pallas_call(
        functools.partial(_lru_prompt_kernel, tt=tt, batch=batch, pitch=pitch),
        grid=(seq // tt,),
        in_specs=[blk, blk] + [_const_spec(wts[k].shape) for k in names],
        out_specs=[blk, pl.BlockSpec((batch, LRU_WIDTH), lambda t: (0, 0))],
        out_shape=[jax.ShapeDtypeStruct((batch, seq, LRU_WIDTH), BF16), jax.ShapeDtypeStruct((batch, LRU_WIDTH), F32)],
        scratch_shapes=[pltpu.VMEM((batch, tt + 8, LRU_WIDTH), F32)]
        + [pltpu.VMEM((LRU_WIDTH // LANES, batch * pitch, LANES), F32)] * 3 + [pltpu.VMEM((batch, LRU_WIDTH), F32)],
        compiler_params=_params(("arbitrary",)), name="lru_prompt",
    )(to3(lx), to3(lg), *[wts[k] for k in names])
    return rec.reshape(n, LRU_WIDTH), ht


def _lru_sample_kernel(lx_ref, lg_ref, conv_ref, h0_ref, cw_ref, cb_ref, wa_ref, ba_ref, wx_ref, bx_ref, lam_ref,
                       rec_ref, ht_ref):
    steps = lx_ref.shape[0]
    xs = [conv_ref[j] for j in range(CONV_WIDTH - 1)] + [lx_ref[t] for t in range(steps)]
    cw = cw_ref[...]
    h = h0_ref[...]
    for t in range(steps):
        y = cb_ref[...]
        for j in range(CONV_WIDTH):
            y = y + cw[j:j + 1, :] * xs[t + j]
        a, u = _lru_gates(y, wa_ref, ba_ref, wx_ref, bx_ref, lam_ref)
        h = a * h + u
        rec_ref[t] = (h * jax.nn.gelu(lg_ref[t])).astype(BF16)
    ht_ref[...] = h


def _lru_sample(lx_t, lg_t, conv_t, h0, wts):
    steps, nb, _ = lx_t.shape
    names = ("conv_w", "conv_b", "wa", "ba", "wx", "bx", "lam")
    return pl.pallas_call(
        _lru_sample_kernel,
        out_shape=[jax.ShapeDtypeStruct((steps, nb, LRU_WIDTH), BF16), jax.ShapeDtypeStruct((nb, LRU_WIDTH), F32)],
        compiler_params=pltpu.CompilerParams(vmem_limit_bytes=VMEM_LIMIT), name="lru_sample",
    )(lx_t, lg_t, conv_t, h0, *[wts[k] for k in names])


def _mem_kv_kernel(m_ref, g_ref, wk_ref, wv_ref, k4_ref, v4_ref, kb_ref, vb_ref):
    m = _rms(m_ref[0], g_ref[...]).astype(BF16)
    for w_ref, o4_ref, ob_ref in ((wk_ref, k4_ref, kb_ref), (wv_ref, v4_ref, vb_ref)):
        y = _dot(m, w_ref[...])
        ob_ref[0] = y.astype(BF16)
        for h in range(MEM_HEADS):
            o4_ref[0, :, h, :] = y[:, h * MEM_HEAD_DIM:(h + 1) * MEM_HEAD_DIM]


def _mem_kv(mem, g, wk, wv):
    nb, n_mem, _ = mem.shape
    blk3 = pl.BlockSpec((1, n_mem, D_MODEL), lambda i: (i, 0, 0))
    blk4 = pl.BlockSpec((1, n_mem, MEM_HEADS, MEM_HEAD_DIM), lambda i: (i, 0, 0, 0))
    shp4 = jax.ShapeDtypeStruct((nb, n_mem, MEM_HEADS, MEM_HEAD_DIM), F32)
    shp3 = jax.ShapeDtypeStruct((nb, n_mem, D_MODEL), BF16)
    return pl.pallas_call(
        _mem_kv_kernel, grid=(nb,),
        in_specs=[blk3, _const_spec(g.shape), _const_spec(wk.shape), _const_spec(wv.shape)],
        out_specs=[blk4, blk4, blk3, blk3], out_shape=[shp4, shp4, shp3, shp3],
        compiler_params=_params(("parallel",)), name="mem_kv",
    )(mem, g, wk, wv)


def _mix_out_kernel(x_ref, attn_ref, rec_ref, *rest, absorbed):
    if absorbed:
        wuv_ref, woa_ref, wor_ref, g_ref, wq_ref, x1_ref, qm_ref = rest
        attn = _dot(attn_ref[...], wuv_ref[...]).astype(BF16)
    else:
        woa_ref, wor_ref, g_ref, wq_ref, x1_ref, qm_ref = rest
        attn = attn_ref[...]
    x1 = x_ref[...] + _dot(attn, woa_ref[...]) + _dot(rec_ref[...], wor_ref[...])
    x1_ref[...] = x1
    qm_ref[...] = _dot(_rms(x1, g_ref[...]).astype(BF16), wq_ref[...]).astype(BF16)


def _mix_out(x, attn, rec, wts, *, absorbed, tn):
    n = x.shape[0]
    row_spec = lambda w: pl.BlockSpec((tn, w), lambda i: (i, 0))
    names = (("wuv_bd",) if absorbed else ()) + ("woa", "wor", "g_mem", "w_mem_q")
    return pl.pallas_call(
        functools.partial(_mix_out_kernel, absorbed=absorbed), grid=(n // tn,),
        in_specs=[row_spec(D_MODEL), row_spec(attn.shape[1]), row_spec(LRU_WIDTH)]
        + [_const_spec(wts[k].shape) for k in names],
        out_specs=[row_spec(D_MODEL), row_spec(D_MODEL)],
        out_shape=[jax.ShapeDtypeStruct((n, D_MODEL), F32), jax.ShapeDtypeStruct((n, D_MODEL), BF16)],
        compiler_params=_params(("parallel",)), name="mix_out_s" if absorbed else "mix_out_p",
    )(x, attn, rec, *[wts[k] for k in names])


def _mem_attn_kernel(q_ref, k_ref, v_ref, o_ref):
    q = q_ref[0]
    scale = MEM_HEAD_DIM ** -0.5
    for h in range(MEM_HEADS):
        lo, hi = h * MEM_HEAD_DIM, (h + 1) * MEM_HEAD_DIM
        kh = k_ref[0, :, lo:hi].astype(BF16)
        vh = v_ref[0, :, lo:hi].astype(BF16)
        s = _dot_nt(q[:, lo:hi], kh) * scale
        e = jnp.exp(s - jnp.max(s, axis=-1, keepdims=True))
        p = e / jnp.sum(e, axis=-1, keepdims=True)
        o_ref[0, :, lo:hi] = _dot(p.astype(BF16), vh).astype(BF16)


def _mem_attn(q, k, v, *, tq, name):
    nb, sq, _ = q.shape
    n_mem = k.shape[1]
    qblk = pl.BlockSpec((1, tq, D_MODEL), lambda b, i: (b, i, 0))
    kblk = pl.BlockSpec((1, n_mem, D_MODEL), lambda b, i: (b, 0, 0))
    return pl.pallas_call(
        _mem_attn_kernel, grid=(nb, sq // tq),
        in_specs=[qblk, kblk, kblk], out_specs=qblk,
        out_shape=jax.ShapeDtypeStruct((nb, sq, D_MODEL), BF16),
        compiler_params=_params(("parallel", "arbitrary")), name=name,
    )(q, k, v)


def _mem_attn_cache_kernel(q_ref, k_ref, v_ref, o_ref, *, steps):
    n_mem = k_ref.shape[1]
    for g in range(q_ref.shape[0]):
        q = q_ref[g]
        k2 = k_ref[g].reshape(n_mem * MEM_HEADS, MEM_HEAD_DIM).astype(BF16)
        v2 = v_ref[g].reshape(n_mem * MEM_HEADS, MEM_HEAD_DIM).astype(BF16)
        s = _dot_nt(q, k2) * (MEM_HEAD_DIM ** -0.5)
        row_head = lax.shift_right_logical(lax.broadcasted_iota(jnp.int32, s.shape, 0), int(math.log2(steps)))
        col_head = lax.bitwise_and(lax.broadcasted_iota(jnp.int32, s.shape, 1), MEM_HEADS - 1)
        s = jnp.where(row_head == col_head, s, -jnp.inf)
        e = jnp.exp(s - jnp.max(s, axis=-1, keepdims=True))
        p = e / jnp.sum(e, axis=-1, keepdims=True)
        o_ref[g] = _dot(p.astype(BF16), v2).astype(BF16)


def _mem_attn_cache(q, k, v, *, steps, per_step=4):
    nb, rows, _ = q.shape
    qblk = pl.BlockSpec((per_step, rows, MEM_HEAD_DIM), lambda b: (b, 0, 0))
    kblk = pl.BlockSpec((per_step,) + k.shape[1:], lambda b: (b, 0, 0, 0))
    return pl.pallas_call(
        functools.partial(_mem_attn_cache_kernel, steps=steps), grid=(nb // per_step,),
        in_specs=[qblk, kblk, kblk], out_specs=qblk,
        out_shape=jax.ShapeDtypeStruct(q.shape, BF16),
        compiler_params=_params(("parallel",)), name="mem_attn_s",
    )(q, k, v)


def _tail_kernel(x1_ref, o_ref, wo_ref, g_ref, wup_ref, wdn_ref, gf_ref, y_ref, *, ff_chunk):
    x2 = x1_ref[...] + _dot(o_ref[...], wo_ref[...])
    h = _rms(x2, g_ref[...]).astype(BF16)
    acc = x2
    for c in range(D_FF // ff_chunk):
        lo, hi = c * ff_chunk, (c + 1) * ff_chunk
        up = jnp.maximum(_dot(h, wup_ref[:, lo:hi]), 0.0)
        acc = acc + _dot((up * up).astype(BF16), wdn_ref[lo:hi, :])
    y_ref[...] = _rms(acc, gf_ref[...])


def _tail(x1, o, wts, *, tn, name):
    n = x1.shape[0]
    blk = pl.BlockSpec((tn, D_MODEL), lambda i: (i, 0))
    names = ("w_mem_o", "g_mlp", "w_up", "w_down", "g_final")
    return pl.pallas_call(
        functools.partial(_tail_kernel, ff_chunk=1024), grid=(n // tn,),
        in_specs=[blk, blk] + [_const_spec(wts[k].shape) for k in names],
        out_specs=blk, out_shape=jax.ShapeDtypeStruct((n, D_MODEL), F32),
        compiler_params=_params(("parallel",)), name=name,
    )(x1, o, *[wts[k] for k in names])


def _rot_half_cols(w):
    half = QK_ROPE_DIM // 2
    return jnp.concatenate([-w[..., half:], w[..., :half]], axis=-1)


def _prep_weights(w_in, q_norm_g, w_q_up, kv_norm_g, w_uk, w_uv, conv_w, conv_b, lru_w_a, lru_b_a, lru_w_x, lru_b_x,
                  lru_lambda, w_out, norm_mem_g, mem_norm_g, w_mem_q, w_mem_k, w_mem_v, w_mem_o, norm_mlp_g,
                  w_up, w_down, final_norm_g):
    w = {}
    s1 = Q_LORA_RANK
    s2 = s1 + KV_LORA_RANK
    s3 = s2 + QK_ROPE_DIM
    s4 = s3 + LRU_WIDTH
    wt = w_in.T
    wkr = wt[s2:s3]
    half = QK_ROPE_DIM // 2
    wkr_rot = jnp.concatenate([-wkr[half:], wkr[:half]], axis=0)
    lane_pad = jnp.zeros((_KR_ROT_LANE - QK_ROPE_DIM, D_MODEL), F32)
    w["win"] = jnp.concatenate([wt[:s1], wt[s1:s2], wt[s3:s4], wt[s4:], wkr, lane_pad, wkr_rot, lane_pad],
                               axis=0).astype(BF16)
    w["qg"] = q_norm_g.reshape(1, -1)
    w["kvg"] = kv_norm_g.reshape(1, -1)
    wq3 = w_q_up.reshape(Q_LORA_RANK, MLA_HEADS, QK_NOPE_DIM + QK_ROPE_DIM)
    nope, ropew = wq3[..., :QK_NOPE_DIM], wq3[..., QK_NOPE_DIM:]
    tail_pad = jnp.zeros((Q_LORA_RANK, MLA_HEADS, HEAD_PAD - QK_NOPE_DIM - QK_ROPE_DIM), F32)
    w["wq"] = jnp.concatenate([nope, ropew, tail_pad], -1).reshape(Q_LORA_RANK, -1).astype(BF16)
    w["wqr"] = _rot_half_cols(ropew).reshape(Q_LORA_RANK, -1).astype(BF16)
    head_pad = jnp.zeros((KV_LORA_RANK, MLA_HEADS, HEAD_PAD - QK_NOPE_DIM), F32)
    w["wuk"] = jnp.concatenate([w_uk, head_pad], -1).reshape(KV_LORA_RANK, -1).astype(BF16)
    w["wuv_t"] = w_uv.reshape(KV_LORA_RANK, -1).T.astype(BF16)
    eye = jnp.eye(MLA_HEADS, dtype=F32)
    uk_t = jnp.transpose(w_uk, (1, 2, 0))
    uk_t = jnp.concatenate([uk_t, jnp.zeros((MLA_HEADS, HEAD_PAD - QK_NOPE_DIM, KV_LORA_RANK), F32)], 1)
    w["wabs"] = jnp.einsum("hdc,hg->hdgc", uk_t, eye).reshape(MLA_HEADS * HEAD_PAD, -1).astype(BF16)
    w["wuv_bd"] = jnp.einsum("chd,hg->hcgd", w_uv, eye).reshape(MLA_HEADS * KV_LORA_RANK, -1).astype(BF16)
    n_attn = MLA_HEADS * V_HEAD_DIM
    w["woa"] = w_out[:n_attn].astype(BF16)
    w["wor"] = w_out[n_attn:].astype(BF16)
    w["conv_w"] = conv_w
    w["conv_b"] = conv_b.reshape(1, -1)
    eye_l = jnp.eye(LRU_BLOCKS, dtype=F32)
    w["wa"] = jnp.einsum("nde,nm->ndme", lru_w_a, eye_l).reshape(LRU_WIDTH, LRU_WIDTH).astype(BF16)
    w["wx"] = jnp.einsum("nde,nm->ndme", lru_w_x, eye_l).reshape(LRU_WIDTH, LRU_WIDTH).astype(BF16)
    w["ba"] = lru_b_a.reshape(1, -1)
    w["bx"] = lru_b_x.reshape(1, -1)
    w["lam"] = lru_lambda.reshape(1, -1)
    w["g_mem"] = norm_mem_g.reshape(1, -1)
    w["g_memkv"] = mem_norm_g.reshape(1, -1)
    w["w_mem_q"] = w_mem_q.astype(BF16)
    w["w_mem_k"] = w_mem_k.astype(BF16)
    w["w_mem_v"] = w_mem_v.astype(BF16)
    w["w_mem_o"] = w_mem_o.astype(BF16)
    w["g_mlp"] = norm_mlp_g.reshape(1, -1)
    w["w_up"] = w_up.astype(BF16)
    w["w_down"] = w_down.astype(BF16)
    w["g_final"] = final_norm_g.reshape(1, -1)
    return w


def _rope_tables(pos):
    pos = np.asarray(pos, np.float64)
    inv = ROPE_THETA ** (-np.arange(0, QK_ROPE_DIM, 2, dtype=np.float64) / QK_ROPE_DIM)
    ang = pos[:, None] * inv[None, :]
    cos, sin = np.cos(ang), np.sin(ang)
    n = pos.shape[0]
    z = lambda k: np.zeros((n, k))
    q_tail = HEAD_PAD - QK_NOPE_DIM - QK_ROPE_DIM
    c_exp = MLA_SCALE * LOG2E
    cq = np.concatenate([np.ones((n, QK_NOPE_DIM)), cos, cos, z(q_tail)], 1) * c_exp
    sq = np.concatenate([z(QK_NOPE_DIM), sin, sin, z(q_tail)], 1) * c_exp
    k_gap = z(_KR_ROT_LANE - QK_ROPE_DIM)
    cks = np.concatenate([cos, cos, k_gap, sin, sin, k_gap], 1)
    return tuple(jnp.asarray(t, F32) for t in (cq, sq, cks))


def kernel(x_prompt, x_sample, cache_mla, cache_mem_k, cache_mem_v, state_lru_h, state_conv, page_table, mem_prompt, norm_mix_g, w_in, q_norm_g, w_q_up, kv_norm_g, w_uk, w_uv, conv_w, conv_b, lru_w_a, lru_b_a, lru_w_x, lru_b_x, lru_lambda, w_out, norm_mem_g, mem_norm_g, w_mem_q, w_mem_k, w_mem_v, w_mem_o, norm_mlp_g, w_up, w_down, final_norm_g):
    B, S, _ = x_prompt.shape
    Bd, T, _ = x_sample.shape
    assert w_in.shape[0] == 1, "single layer"
    wts = _prep_weights(w_in[0], q_norm_g[0], w_q_up[0], kv_norm_g[0], w_uk[0], w_uv[0], conv_w[0], conv_b[0],
                        lru_w_a[0], lru_b_a[0], lru_w_x[0], lru_b_x[0], lru_lambda[0], w_out[0], norm_mem_g[0],
                        mem_norm_g[0], w_mem_q[0], w_mem_k[0], w_mem_v[0], w_mem_o[0], norm_mlp_g[0],
                        w_up[0], w_down[0], final_norm_g)
    g_mix = norm_mix_g[0].reshape(1, -1)
    tn = 512

    xp = x_prompt.reshape(B * S, D_MODEL)
    q_p, rows_p, lx_p, lg_p, k_p, v_p = _in_proj(xp, g_mix, wts, _rope_tables(np.arange(S)),
                                                 prompt=True, tn=tn, pos_blocks=S // tn)
    attn_p = _attn_prompt(q_p, k_p, v_p, batch=B, seq=S, tq=1024)
    rec_p, ht_p = _lru_prompt(lx_p, lg_p, wts, batch=B, seq=S, tt=128)
    x1_p, qm_p = _mix_out(xp, attn_p, rec_p, wts, absorbed=False, tn=tn)
    n_mem = mem_prompt.shape[1]
    mk_p, mv_p, mkb_p, mvb_p = _mem_kv(mem_prompt, wts["g_memkv"], wts["w_mem_k"], wts["w_mem_v"])
    o_p = _mem_attn(qm_p.reshape(B, S, D_MODEL), mkb_p, mvb_p, tq=512, name="mem_attn_p")
    y_p = _tail(x1_p, o_p.reshape(B * S, D_MODEL), wts, tn=tn, name="tail_p")

    ns = Bd * T
    xs = x_sample.reshape(ns, D_MODEL)
    tabs_s = _rope_tables(np.tile(PAST_LEN + np.arange(T), Bd))
    q_s, rows_s, lx_s, lg_s, qabs_s = _in_proj(xs, g_mix, wts, tabs_s, prompt=False, tn=ns, pos_blocks=1)
    rows_q = T * MLA_HEADS
    qr_s = q_s.reshape(ns * MLA_HEADS, HEAD_PAD)[:, QK_NOPE_DIM:QK_NOPE_DIM + QK_ROPE_DIM]
    ctx_s = _attn_sample(page_table, qabs_s.reshape(Bd, rows_q, KV_LORA_RANK), qr_s.reshape(Bd, rows_q, QK_ROPE_DIM),
                         rows_s.reshape(Bd, T, KV_CACHE_DIM), jnp.swapaxes(cache_mla[0], 1, 2),
                         pages=8, slots=16, pv_chunk=2048)
    to_time_major = lambda a: jnp.transpose(a.reshape(Bd, -1, LRU_WIDTH), (1, 0, 2))
    rec_t, ht_s = _lru_sample(to_time_major(lx_s), to_time_major(lg_s), to_time_major(state_conv[0]),
                              state_lru_h[0], wts)
    rec_s = jnp.transpose(rec_t, (1, 0, 2)).reshape(ns, LRU_WIDTH)
    x1_s, qm_s = _mix_out(xs, ctx_s.reshape(ns, MLA_HEADS * KV_LORA_RANK), rec_s, wts, absorbed=True, tn=ns)
    head_major = lambda a: jnp.transpose(a.reshape(Bd, T, MEM_HEADS, MEM_HEAD_DIM), (0, 2, 1, 3))
    o_s = _mem_attn_cache(head_major(qm_s).reshape(Bd, MEM_HEADS * T, MEM_HEAD_DIM), cache_mem_k[0], cache_mem_v[0],
                          steps=T)
    o_s = jnp.transpose(o_s.reshape(Bd, MEM_HEADS, T, MEM_HEAD_DIM), (0, 2, 1, 3))
    y_s = _tail(x1_s, o_s.reshape(ns, D_MODEL), wts, tn=ns, name="tail_s")

    lx_p3 = lx_p.reshape(B, S, LRU_WIDTH)
    lx_s3 = lx_s.reshape(Bd, T, LRU_WIDTH)
    keep = CONV_WIDTH - 1
    return (y_p.reshape(B, S, D_MODEL), y_s.reshape(Bd, T, D_MODEL),
            rows_p.reshape(1, B, S, KV_CACHE_DIM), rows_s.reshape(1, Bd, T, KV_CACHE_DIM),
            mk_p.reshape(1, B, n_mem, MEM_HEADS, MEM_HEAD_DIM), mv_p.reshape(1, B, n_mem, MEM_HEADS, MEM_HEAD_DIM),
            ht_p.reshape(1, B, LRU_WIDTH), ht_s.reshape(1, Bd, LRU_WIDTH),
            lx_p3[:, S - keep:].reshape(1, B, keep, LRU_WIDTH), lx_s3[:, T - keep:].reshape(1, Bd, keep, LRU_WIDTH))
```

```python
import functools
import math

import jax
import jax.numpy as jnp
import numpy as np
from jax import lax
from jax.experimental import pallas as pl
from jax.experimental.pallas import tpu as pltpu

F32 = jnp.float32
BF16 = jnp.bfloat16

D_MODEL = 1024
PAST_LEN = 16384
PAGE_SIZE = 128
MLA_HEADS = 8
QK_NOPE_DIM = 64
QK_ROPE_DIM = 32
V_HEAD_DIM = 64
Q_LORA_RANK = 384
KV_LORA_RANK = 256
KV_CACHE_DIM = KV_LORA_RANK + QK_ROPE_DIM
ROPE_THETA = 10000.0
MLA_SCALE = (QK_NOPE_DIM + QK_ROPE_DIM) ** -0.5
LRU_WIDTH = D_MODEL // 2
LRU_BLOCKS = 8
LRU_C = 8.0
CONV_WIDTH = 4
MEM_HEADS = 4
MEM_HEAD_DIM = D_MODEL // MEM_HEADS
D_FF = 4 * D_MODEL
EPS = 1e-6

LANES = 128
HEAD_PAD = LANES
LOG2E = math.log2(math.e)
VMEM_LIMIT = 48 * 1024 * 1024

_C_QLAT = 0
_C_KV = _C_QLAT + Q_LORA_RANK
_C_LX = _C_KV + KV_LORA_RANK
_C_LG = _C_LX + LRU_WIDTH
_C_KR = _C_LG + LRU_WIDTH
_KR_ROT_LANE = LANES // 2
IN_EXT = _C_KR + LANES


def _rms(x, g):
    ms = jnp.mean(x * x, axis=-1, keepdims=True)
    return x * lax.rsqrt(ms + EPS) * g


def _dot(a, b):
    return jnp.dot(a, b, preferred_element_type=F32)


def _dot_nt(a, b):
    return lax.dot_general(a, b, (((1,), (1,)), ((), ())), preferred_element_type=F32)


def _const_spec(shape):
    nd = len(shape)
    return pl.BlockSpec(shape, lambda *_: (0,) * nd, pipeline_mode=pl.Buffered(1))


def _params(sem, flags=None):
    return pltpu.CompilerParams(dimension_semantics=sem, vmem_limit_bytes=VMEM_LIMIT, flags=flags)


def _in_proj_kernel(x_ref, g_ref, win_ref, qg_ref, wq_ref, wqr_ref, kvg_ref,
                    cq_ref, sq_ref, cks_ref, *rest, prompt):
    if prompt:
        wuk_ref, wuv_ref, q_ref, rows_ref, lx_ref, lg_ref, k_ref, v_ref = rest
    else:
        wabs_ref, q_ref, rows_ref, lx_ref, lg_ref, qabs_ref = rest
    h = _rms(x_ref[...], g_ref[...]).astype(BF16)
    z = _dot_nt(h, win_ref[...])
    qa = _rms(z[:, _C_QLAT:_C_KV], qg_ref[...]).astype(BF16)
    q1 = _dot(qa, wq_ref[...])
    q2c = _dot(qa, wqr_ref[...])
    per_tile = LANES // QK_ROPE_DIM
    q2 = []
    for hd in range(MLA_HEADS):
        blk = q2c[:, (hd // per_tile) * LANES:(hd // per_tile + 1) * LANES]
        shift = (QK_NOPE_DIM - QK_ROPE_DIM * (hd % per_tile)) % LANES
        q2.append(pltpu.roll(blk, shift, 1) if shift else blk)
    cq = jnp.tile(cq_ref[...], (1, MLA_HEADS))
    sq = jnp.tile(sq_ref[...], (1, MLA_HEADS))
    qb = (q1 * cq + jnp.concatenate(q2, axis=1) * sq).astype(BF16)
    q_ref[...] = qb
    c = _rms(z[:, _C_KV:_C_LX], kvg_ref[...])
    t = z[:, _C_KR:IN_EXT] * cks_ref[...]
    kr = t + pltpu.roll(t, _KR_ROT_LANE, 1)
    rows_ref[:, 0:KV_LORA_RANK] = c
    rows_ref[:, KV_LORA_RANK:KV_CACHE_DIM] = kr[:, 0:QK_ROPE_DIM]
    lx_ref[...] = z[:, _C_LX:_C_LG]
    lg_ref[...] = z[:, _C_LG:_C_KR]
    if prompt:
        cb = c.astype(BF16)
        lane = lax.broadcasted_iota(jnp.int32, kr.shape, 1)
        kr_hi = jnp.where(lane >= QK_NOPE_DIM, kr, 0.0)
        k_ref[...] = (_dot(cb, wuk_ref[...]) + jnp.tile(kr_hi, (1, MLA_HEADS))).astype(BF16)
        v_ref[...] = _dot_nt(wuv_ref[...], cb).astype(BF16)
    else:
        qabs_ref[...] = _dot(qb, wabs_ref[...]).astype(BF16)


def _in_proj(x, g, wts, tabs, *, prompt, tn, pos_blocks):
    n = x.shape[0]
    cq, sq, cks = tabs
    assert QK_NOPE_DIM == _KR_ROT_LANE and QK_NOPE_DIM + QK_ROPE_DIM <= HEAD_PAD
    tab_spec = pl.BlockSpec((tn, LANES), lambda i: (i % pos_blocks, 0))
    row_spec = lambda w: pl.BlockSpec((tn, w), lambda i: (i, 0))
    hp = MLA_HEADS * HEAD_PAD
    in_specs = [row_spec(D_MODEL), _const_spec(g.shape)]
    in_specs += [_const_spec(wts[k].shape) for k in ("win", "qg", "wq", "wqr", "kvg")]
    in_specs += [tab_spec] * 3
    args = [x, g, wts["win"], wts["qg"], wts["wq"], wts["wqr"], wts["kvg"], cq, sq, cks]
    out_shape = [jax.ShapeDtypeStruct((n, hp), BF16), jax.ShapeDtypeStruct((n, KV_CACHE_DIM), F32),
                 jax.ShapeDtypeStruct((n, LRU_WIDTH), F32), jax.ShapeDtypeStruct((n, LRU_WIDTH), F32)]
    out_specs = [row_spec(hp), row_spec(KV_CACHE_DIM), row_spec(LRU_WIDTH), row_spec(LRU_WIDTH)]
    if prompt:
        extra = ("wuk", "wuv_t")
        n_v = MLA_HEADS * V_HEAD_DIM
        out_shape += [jax.ShapeDtypeStruct((n, hp), BF16), jax.ShapeDtypeStruct((n_v, n), BF16)]
        out_specs += [row_spec(hp), pl.BlockSpec((n_v, tn), lambda i: (0, i))]
    else:
        extra = ("wabs",)
        out_shape += [jax.ShapeDtypeStruct((n, MLA_HEADS * KV_LORA_RANK), BF16)]
        out_specs += [row_spec(MLA_HEADS * KV_LORA_RANK)]
    in_specs += [_const_spec(wts[k].shape) for k in extra]
    args += [wts[k] for k in extra]
    return pl.pallas_call(
        functools.partial(_in_proj_kernel, prompt=prompt),
        grid=(n // tn,), in_specs=in_specs, out_specs=out_specs, out_shape=out_shape,
        compiler_params=_params(("parallel",)), name="in_proj_p" if prompt else "in_proj_s",
    )(*args)


HEADS_PER_STEP = 2
SCORE_SLOTS = 2


def _attn_prompt_kernel(q_ref, k_ref, vt_ref, o_ref, s_scr, *, leaf, seq):
    col_max = lambda x: jnp.max(x, axis=0, keepdims=True)
    col_sum = lambda x: jnp.sum(x, axis=0, keepdims=True)
    tri_key = lax.broadcasted_iota(jnp.int32, (leaf, leaf), 0)
    tri_qry = lax.broadcasted_iota(jnp.int32, (leaf, leaf), 1)
    causal = lambda x: jnp.where(tri_key <= tri_qry, x, -jnp.inf)

    def lead(x, width, fill):
        return x if width == 0 else jnp.concatenate([jnp.full((x.shape[0], width), fill, F32), x], axis=1)

    heads = range(HEADS_PER_STEP)
    hcols = [slice(h * HEAD_PAD, (h + 1) * HEAD_PAD) for h in heads]
    vrows = [slice(h * V_HEAD_DIM, (h + 1) * V_HEAD_DIM) for h in heads]
    out = []
    for h in heads:
        slot = h % SCORE_SLOTS
        m = jnp.full((1, seq), -jnp.inf, F32)
        for k0 in range(0, seq, leaf):
            keys = slice(k0, k0 + leaf)
            st = _dot_nt(k_ref[keys, hcols[h]], q_ref[k0:, hcols[h]])
            st = jnp.concatenate([causal(st[:, :leaf]), st[:, leaf:]], axis=1) if k0 + leaf < seq else causal(st)
            s_scr[slot, keys, k0:] = st
            m = jnp.maximum(m, lead(col_max(st), k0, -jnp.inf))
        l = jnp.zeros((1, seq), F32)
        acc = jnp.zeros((V_HEAD_DIM, seq), F32)
        for k0 in range(0, seq, leaf):
            keys = slice(k0, k0 + leaf)
            pt = jnp.exp2(s_scr[slot, keys, k0:] - m[:, k0:])
            l = l + lead(col_sum(pt), k0, 0.0)
            acc = acc + lead(_dot(vt_ref[vrows[h], keys], pt.astype(BF16)), k0, 0.0)
        out.append(acc / l)
    o_ref[...] = jnp.transpose(jnp.concatenate(out, axis=0)).astype(BF16)


def _attn_prompt(q, k, vt, *, batch, seq, leaf):
    n = q.shape[0]
    v_blk = HEADS_PER_STEP * V_HEAD_DIM
    assert seq % leaf == 0 and leaf % LANES == 0 and v_blk % LANES == 0 and MLA_HEADS % HEADS_PER_STEP == 0
    row_blk = pl.BlockSpec((seq, HEADS_PER_STEP * HEAD_PAD), lambda b, h: (b, h))
    return pl.pallas_call(
        functools.partial(_attn_prompt_kernel, leaf=leaf, seq=seq),
        grid=(batch, MLA_HEADS // HEADS_PER_STEP),
        in_specs=[row_blk, row_blk, pl.BlockSpec((v_blk, seq), lambda b, h: (h, b))],
        out_specs=pl.BlockSpec((seq, v_blk), lambda b, h: (b, h)),
        out_shape=jax.ShapeDtypeStruct((n, MLA_HEADS * V_HEAD_DIM), BF16),
        scratch_shapes=[pltpu.VMEM((SCORE_SLOTS, seq, seq), F32)],
        compiler_params=_params(("parallel", "parallel")),
        name="attn_prompt",
    )(q, k, vt)


def _attn_sample_kernel(pt_ref, qa_ref, qr_ref, new_ref, cache_ref, ctx_ref, ring, sem, kt_scr, s_scr,
                        *, pages, n_chunks, slots, pv_chunk):
    b = pl.program_id(0)
    nb = pl.num_programs(0)
    keys = pages * PAGE_SIZE
    group = 4

    def fetch(bb, chunk, slot):
        for i in range(pages):
            pid = pt_ref[bb, chunk * pages + i]
            pltpu.make_async_copy(cache_ref.at[pid], ring.at[slot, i], sem.at[slot]).start()

    def wait(slot):
        for i in range(pages):
            pltpu.make_async_copy(cache_ref.at[0], ring.at[slot, i], sem.at[slot]).wait()

    @pl.when(b == 0)
    def _():
        for c in range(slots):
            fetch(0, c, c)

    qa = qa_ref[0]
    qr = qr_ref[0]
    rows = qa.shape[0]

    def lane_fold(x, op):
        out = x[:, 0:LANES]
        for j in range(1, x.shape[1] // LANES):
            out = op(out, x[:, j * LANES:(j + 1) * LANES])
        return out

    def score_group(g, mrun):
        c0 = g * group
        for k in range(group):
            wait(lax.rem(c0 + k, slots))
        for k in range(group):
            slot = lax.rem(c0 + k, slots)
            kt = jnp.concatenate([ring[slot, i] for i in range(pages)], axis=1).astype(BF16)
            off = pl.multiple_of((c0 + k) * keys, keys)
            kt_scr[:, pl.ds(off, keys)] = kt
            s = _dot(qa, kt[0:KV_LORA_RANK, :]) + _dot(qr, kt[KV_LORA_RANK:KV_CACHE_DIM, :])
            s_scr[:, pl.ds(off, keys)] = s
            mrun = jnp.maximum(mrun, lane_fold(s, jnp.maximum))
        nxt = c0 + slots
        wrap = nxt >= n_chunks
        bb = jnp.where(wrap, b + 1, b)
        bb = jnp.where(bb < nb, bb, 0)
        cc = jnp.where(wrap, nxt - n_chunks, nxt)
        for k in range(group):
            fetch(bb, cc + k, lax.rem(c0 + k, slots))
        return mrun

    mrun = lax.fori_loop(0, n_chunks // group, score_group, jnp.full((rows, LANES), -jnp.inf, F32))

    @pl.when(b == nb - 1)
    def _():
        for c in range(slots):
            wait(c)

    new = new_ref[0]
    n_new = new.shape[0]
    nc = new[:, 0:KV_LORA_RANK].astype(BF16).astype(F32)
    nr = new[:, KV_LORA_RANK:KV_CACHE_DIM].astype(BF16).astype(F32)
    qaf = qa.astype(F32)
    qrf = qr.astype(F32)
    tok = lax.shift_right_logical(lax.broadcasted_iota(jnp.int32, (rows, 1), 0), int(math.log2(MLA_HEADS)))
    s_new = []
    for t in range(n_new):
        st = (jnp.sum(qaf * nc[t:t + 1, :], axis=-1, keepdims=True)
              + jnp.sum(qrf * nr[t:t + 1, :], axis=-1, keepdims=True))
        s_new.append(jnp.where(tok >= t, st, -jnp.inf))

    m = jnp.max(mrun, axis=-1, keepdims=True)
    for st in s_new:
        m = jnp.maximum(m, st)

    def pv_step(j, carry):
        acc, lrun = carry
        off = pl.multiple_of(j * pv_chunk, pv_chunk)
        p = jnp.exp2(s_scr[:, pl.ds(off, pv_chunk)] - m)
        acc = acc + _dot_nt(p.astype(BF16), kt_scr[0:KV_LORA_RANK, pl.ds(off, pv_chunk)])
        return acc, lrun + lane_fold(p, jnp.add)

    acc, lrun = lax.fori_loop(0, (n_chunks * keys) // pv_chunk, pv_step,
                              (jnp.zeros((rows, KV_LORA_RANK), F32), jnp.zeros((rows, LANES), F32)), unroll=True)
    l = jnp.sum(lrun, axis=-1, keepdims=True)
    for t, st in enumerate(s_new):
        pt = jnp.exp2(st - m)
        l = l + pt
        acc = acc + pt.astype(BF16).astype(F32) * nc[t:t + 1, :]
    ctx_ref[0] = (acc / l).astype(BF16)


def _attn_sample(page_table, qabs, qrope, new_rows, cache_t, *, pages, slots, pv_chunk):
    nb, rows, _ = qabs.shape
    n_pages = page_table.shape[1]
    n_chunks = n_pages // pages
    n_keys = n_pages * PAGE_SIZE
    assert n_chunks * pages == n_pages and n_chunks % slots == 0 and slots % 4 == 0 and n_keys % pv_chunk == 0
    t_new = new_rows.shape[1]
    grid_spec = pltpu.PrefetchScalarGridSpec(
        num_scalar_prefetch=1, grid=(nb,),
        in_specs=[pl.BlockSpec((1, rows, KV_LORA_RANK), lambda b, pt: (b, 0, 0)),
                  pl.BlockSpec((1, rows, QK_ROPE_DIM), lambda b, pt: (b, 0, 0)),
                  pl.BlockSpec((1, t_new, KV_CACHE_DIM), lambda b, pt: (b, 0, 0)),
                  pl.BlockSpec(memory_space=pl.ANY)],
        out_specs=pl.BlockSpec((1, rows, KV_LORA_RANK), lambda b, pt: (b, 0, 0)),
        scratch_shapes=[pltpu.VMEM((slots, pages, KV_CACHE_DIM, PAGE_SIZE), F32),
                        pltpu.SemaphoreType.DMA((slots,)),
                        pltpu.VMEM((KV_CACHE_DIM, n_keys), BF16),
                        pltpu.VMEM((rows, n_keys), F32)])
    return pl.pallas_call(
        functools.partial(_attn_sample_kernel, pages=pages, n_chunks=n_chunks, slots=slots, pv_chunk=pv_chunk),
        grid_spec=grid_spec,
        out_shape=jax.ShapeDtypeStruct((nb, rows, KV_LORA_RANK), BF16),
        compiler_params=_params(("arbitrary",)), name="attn_sample",
    )(page_table, qabs, qrope, new_rows, cache_t)


def _lru_gates(y, wa_ref, ba_ref, wx_ref, bx_ref, lam_ref):
    yb = y.astype(BF16)
    r = jax.nn.sigmoid(_dot(yb, wa_ref[...]) + ba_ref[...])
    i = jax.nn.sigmoid(_dot(yb, wx_ref[...]) + bx_ref[...])
    nl = -lam_ref[...]
    softplus = jnp.maximum(nl, 0.0) + jnp.log1p(jnp.exp(-jnp.abs(nl)))
    log_a = -LRU_C * r * softplus
    a = jnp.exp(log_a)
    th = jnp.tanh(log_a)
    u = (y * i) * jnp.sqrt(-2.0 * th / (1.0 - th))
    return a, u


def _lru_prompt_kernel(lx_ref, lg_ref, cw_ref, cb_ref, wa_ref, ba_ref, wx_ref, bx_ref, lam_ref,
                       rec_ref, ht_ref, xp_scr, a_scr, u_scr, hs_scr, h_scr, *, tt, batch, pitch):
    ti = pl.program_id(0)
    sub = 8

    @pl.when(ti == 0)
    def _():
        xp_scr[:, 0:sub, :] = jnp.zeros((batch, sub, LRU_WIDTH), F32)
        h_scr[...] = jnp.zeros((batch, LRU_WIDTH), F32)

    cw = cw_ref[...]
    ys = []
    for b in range(batch):
        x = lx_ref[b]
        xp_scr[b, sub:sub + tt, :] = x
        y = cb_ref[...] + cw[3:4, :] * x
        for j in range(CONV_WIDTH - 1):
            y = y + cw[j:j + 1, :] * xp_scr[b, sub - 3 + j:sub - 3 + j + tt, :]
        ys.append(y)
    a, u = _lru_gates(jnp.concatenate(ys, axis=0), wa_ref, ba_ref, wx_ref, bx_ref, lam_ref)
    n_col = LRU_WIDTH // LANES
    cols = [slice(c * LANES, (c + 1) * LANES) for c in range(n_col)]
    for c in range(n_col):
        for b in range(batch):
            a_scr[c, b * pitch:b * pitch + tt, :] = a[b * tt:(b + 1) * tt, cols[c]]
            u_scr[c, b * pitch:b * pitch + tt, :] = u[b * tt:(b + 1) * tt, cols[c]]

    def step(t, h):
        rows = pl.ds(t, batch, stride=pitch)
        h = tuple(a_scr[c, rows, :] * h[c] + u_scr[c, rows, :] for c in range(n_col))
        for c in range(n_col):
            hs_scr[c, rows, :] = h[c]
        return h

    h0 = h_scr[...]
    h_last = lax.fori_loop(0, tt, step, tuple(h0[:, cols[c]] for c in range(n_col)), unroll=8)
    h_last = jnp.concatenate(h_last, axis=1)
    h_scr[...] = h_last
    ht_ref[...] = h_last
    for b in range(batch):
        hs = jnp.concatenate([hs_scr[c, b * pitch:b * pitch + tt, :] for c in range(n_col)], axis=1)
        rec_ref[b] = (hs * jax.nn.gelu(lg_ref[b])).astype(BF16)
        xp_scr[b, 0:sub, :] = xp_scr[b, tt:tt + sub, :]


def _lru_prompt(lx, lg, wts, *, batch, seq, tt):
    n = lx.shape[0]
    pitch = tt + 8
    blk = pl.BlockSpec((batch, tt, LRU_WIDTH), lambda t: (0, t, 0))
    names = ("conv_w", "conv_b", "wa", "ba", "wx", "bx", "lam")
    to3 = lambda a: a.reshape(batch, seq, LRU_WIDTH)
    rec, ht = pl.pallas_call(
        functools.partial(_lru_prompt_kernel, tt=tt, batch=batch, pitch=pitch),
        grid=(seq // tt,),
        in_specs=[blk, blk] + [_const_spec(wts[k].shape) for k in names],
        out_specs=[blk, pl.BlockSpec((batch, LRU_WIDTH), lambda t: (0, 0))],
        out_shape=[jax.ShapeDtypeStruct((batch, seq, LRU_WIDTH), BF16), jax.ShapeDtypeStruct((batch, LRU_WIDTH), F32)],
        scratch_shapes=[pltpu.VMEM((batch, tt + 8, LRU_WIDTH), F32)]
        + [pltpu.VMEM((LRU_WIDTH // LANES, batch * pitch, LANES), F32)] * 3 + [pltpu.VMEM((batch, LRU_WIDTH), F32)],
        compiler_params=_params(("arbitrary",)), name="lru_prompt",
    )(to3(lx), to3(lg), *[wts[k] for k in names])
    return rec.reshape(n, LRU_WIDTH), ht


def _lru_sample_kernel(lx_ref, lg_ref, conv_ref, h0_ref, cw_ref, cb_ref, wa_ref, ba_ref, wx_ref, bx_ref, lam_ref,
                       rec_ref, ht_ref):
    steps = lx_ref.shape[0]
    xs = [conv_ref[j] for j in range(CONV_WIDTH - 1)] + [lx_ref[t] for t in range(steps)]
    cw = cw_ref[...]
    h = h0_ref[...]
    for t in range(steps):
        y = cb_ref[...]
        for j in range(CONV_WIDTH):
            y = y + cw[j:j + 1, :] * xs[t + j]
        a, u = _lru_gates(y, wa_ref, ba_ref, wx_ref, bx_ref, lam_ref)
        h = a * h + u
        rec_ref[t] = (h * jax.nn.gelu(lg_ref[t])).astype(BF16)
    ht_ref[...] = h


def _lru_sample(lx_t, lg_t, conv_t, h0, wts):
    steps, nb, _ = lx_t.shape
    names = ("conv_w", "conv_b", "wa", "ba", "wx", "bx", "lam")
    return pl.pallas_call(
        _lru_sample_kernel,
        out_shape=[jax.ShapeDtypeStruct((steps, nb, LRU_WIDTH), BF16), jax.ShapeDtypeStruct((nb, LRU_WIDTH), F32)],
        compiler_params=pltpu.CompilerParams(vmem_limit_bytes=VMEM_LIMIT), name="lru_sample",
    )(lx_t, lg_t, conv_t, h0, *[wts[k] for k in names])


def _mem_kv_kernel(m_ref, g_ref, wk_ref, wv_ref, k4_ref, v4_ref, kb_ref, vb_ref):
    m = _rms(m_ref[0], g_ref[...]).astype(BF16)
    for w_ref, o4_ref, ob_ref in ((wk_ref, k4_ref, kb_ref), (wv_ref, v4_ref, vb_ref)):
        y = _dot(m, w_ref[...])
        ob_ref[0] = y.astype(BF16)
        for h in range(MEM_HEADS):
            o4_ref[0, :, h, :] = y[:, h * MEM_HEAD_DIM:(h + 1) * MEM_HEAD_DIM]


def _mem_kv(mem, g, wk, wv):
    nb, n_mem, _ = mem.shape
    blk3 = pl.BlockSpec((1, n_mem, D_MODEL), lambda i: (i, 0, 0))
    blk4 = pl.BlockSpec((1, n_mem, MEM_HEADS, MEM_HEAD_DIM), lambda i: (i, 0, 0, 0))
    shp4 = jax.ShapeDtypeStruct((nb, n_mem, MEM_HEADS, MEM_HEAD_DIM), F32)
    shp3 = jax.ShapeDtypeStruct((nb, n_mem, D_MODEL), BF16)
    return pl.pallas_call(
        _mem_kv_kernel, grid=(nb,),
        in_specs=[blk3, _const_spec(g.shape), _const_spec(wk.shape), _const_spec(wv.shape)],
        out_specs=[blk4, blk4, blk3, blk3], out_shape=[shp4, shp4, shp3, shp3],
        compiler_params=_params(("parallel",)), name="mem_kv",
    )(mem, g, wk, wv)


def _mix_out_kernel(x_ref, attn_ref, rec_ref, *rest, absorbed):
    if absorbed:
        wuv_ref, woa_ref, wor_ref, g_ref, wq_ref, x1_ref, qm_ref = rest
        attn = _dot(attn_ref[...], wuv_ref[...]).astype(BF16)
    else:
        woa_ref, wor_ref, g_ref, wq_ref, x1_ref, qm_ref = rest
        attn = attn_ref[...]
    x1 = x_ref[...] + _dot(attn, woa_ref[...]) + _dot(rec_ref[...], wor_ref[...])
    x1_ref[...] = x1
    qm_ref[...] = _dot(_rms(x1, g_ref[...]).astype(BF16), wq_ref[...]).astype(BF16)


def _mix_out(x, attn, rec, wts, *, absorbed, tn):
    n = x.shape[0]
    row_spec = lambda w: pl.BlockSpec((tn, w), lambda i: (i, 0))
    names = (("wuv_bd",) if absorbed else ()) + ("woa", "wor", "g_mem", "w_mem_q")
    return pl.pallas_call(
        functools.partial(_mix_out_kernel, absorbed=absorbed), grid=(n // tn,),
        in_specs=[row_spec(D_MODEL), row_spec(attn.shape[1]), row_spec(LRU_WIDTH)]
        + [_const_spec(wts[k].shape) for k in names],
        out_specs=[row_spec(D_MODEL), row_spec(D_MODEL)],
        out_shape=[jax.ShapeDtypeStruct((n, D_MODEL), F32), jax.ShapeDtypeStruct((n, D_MODEL), BF16)],
        compiler_params=_params(("parallel",)), name="mix_out_s" if absorbed else "mix_out_p",
    )(x, attn, rec, *[wts[k] for k in names])


def _mem_attn_kernel(q_ref, k_ref, v_ref, o_ref):
    q = q_ref[0]
    scale = MEM_HEAD_DIM ** -0.5
    for h in range(MEM_HEADS):
        lo, hi = h * MEM_HEAD_DIM, (h + 1) * MEM_HEAD_DIM
        kh = k_ref[0, :, lo:hi].astype(BF16)
        vh = v_ref[0, :, lo:hi].astype(BF16)
        s = _dot_nt(q[:, lo:hi], kh) * scale
        e = jnp.exp(s - jnp.max(s, axis=-1, keepdims=True))
        p = e / jnp.sum(e, axis=-1, keepdims=True)
        o_ref[0, :, lo:hi] = _dot(p.astype(BF16), vh).astype(BF16)


def _mem_attn(q, k, v, *, tq, name):
    nb, sq, _ = q.shape
    n_mem = k.shape[1]
    qblk = pl.BlockSpec((1, tq, D_MODEL), lambda b, i: (b, i, 0))
    kblk = pl.BlockSpec((1, n_mem, D_MODEL), lambda b, i: (b, 0, 0))
    return pl.pallas_call(
        _mem_attn_kernel, grid=(nb, sq // tq),
        in_specs=[qblk, kblk, kblk], out_specs=qblk,
        out_shape=jax.ShapeDtypeStruct((nb, sq, D_MODEL), BF16),
        compiler_params=_params(("parallel", "arbitrary")), name=name,
    )(q, k, v)


def _mem_attn_cache_kernel(q_ref, k_ref, v_ref, o_ref, *, steps):
    n_mem = k_ref.shape[1]
    for g in range(q_ref.shape[0]):
        q = q_ref[g]
        k2 = k_ref[g].reshape(n_mem * MEM_HEADS, MEM_HEAD_DIM).astype(BF16)
        v2 = v_ref[g].reshape(n_mem * MEM_HEADS, MEM_HEAD_DIM).astype(BF16)
        s = _dot_nt(q, k2) * (MEM_HEAD_DIM ** -0.5)
        row_head = lax.shift_right_logical(lax.broadcasted_iota(jnp.int32, s.shape, 0), int(math.log2(steps)))
        col_head = lax.bitwise_and(lax.broadcasted_iota(jnp.int32, s.shape, 1), MEM_HEADS - 1)
        s = jnp.where(row_head == col_head, s, -jnp.inf)
        e = jnp.exp(s - jnp.max(s, axis=-1, keepdims=True))
        p = e / jnp.sum(e, axis=-1, keepdims=True)
        o_ref[g] = _dot(p.astype(BF16), v2).astype(BF16)


def _mem_attn_cache(q, k, v, *, steps, per_step=4):
    nb, rows, _ = q.shape
    qblk = pl.BlockSpec((per_step, rows, MEM_HEAD_DIM), lambda b: (b, 0, 0))
    kblk = pl.BlockSpec((per_step,) + k.shape[1:], lambda b: (b, 0, 0, 0))
    return pl.pallas_call(
        functools.partial(_mem_attn_cache_kernel, steps=steps), grid=(nb // per_step,),
        in_specs=[qblk, kblk, kblk], out_specs=qblk,
        out_shape=jax.ShapeDtypeStruct(q.shape, BF16),
        compiler_params=_params(("parallel",)), name="mem_attn_s",
    )(q, k, v)


def _tail_kernel(x1_ref, o_ref, wo_ref, g_ref, wup_ref, wdn_ref, gf_ref, y_ref, *, ff_chunk):
    x2 = x1_ref[...] + _dot(o_ref[...], wo_ref[...])
    h = _rms(x2, g_ref[...]).astype(BF16)
    acc = x2
    for c in range(D_FF // ff_chunk):
        lo, hi = c * ff_chunk, (c + 1) * ff_chunk
        up = jnp.maximum(_dot(h, wup_ref[:, lo:hi]), 0.0)
        acc = acc + _dot((up * up).astype(BF16), wdn_ref[lo:hi, :])
    y_ref[...] = _rms(acc, gf_ref[...])


def _tail(x1, o, wts, *, tn, name):
    n = x1.shape[0]
    blk = pl.BlockSpec((tn, D_MODEL), lambda i: (i, 0))
    names = ("w_mem_o", "g_mlp", "w_up", "w_down", "g_final")
    return pl.pallas_call(
        functools.partial(_tail_kernel, ff_chunk=1024), grid=(n // tn,),
        in_specs=[blk, blk] + [_const_spec(wts[k].shape) for k in names],
        out_specs=blk, out_shape=jax.ShapeDtypeStruct((n, D_MODEL), F32),
        compiler_params=_params(("parallel",)), name=name,
    )(x1, o, *[wts[k] for k in names])


def _rot_half_cols(w):
    half = QK_ROPE_DIM // 2
    return jnp.concatenate([-w[..., half:], w[..., :half]], axis=-1)


def _prep_weights(w_in, q_norm_g, w_q_up, kv_norm_g, w_uk, w_uv, conv_w, conv_b, lru_w_a, lru_b_a, lru_w_x, lru_b_x,
                  lru_lambda, w_out, norm_mem_g, mem_norm_g, w_mem_q, w_mem_k, w_mem_v, w_mem_o, norm_mlp_g,
                  w_up, w_down, final_norm_g):
    w = {}
    s1 = Q_LORA_RANK
    s2 = s1 + KV_LORA_RANK
    s3 = s2 + QK_ROPE_DIM
    s4 = s3 + LRU_WIDTH
    wt = w_in.T
    wkr = wt[s2:s3]
    half = QK_ROPE_DIM // 2
    wkr_rot = jnp.concatenate([-wkr[half:], wkr[:half]], axis=0)
    lane_pad = jnp.zeros((_KR_ROT_LANE - QK_ROPE_DIM, D_MODEL), F32)
    w["win"] = jnp.concatenate([wt[:s1], wt[s1:s2], wt[s3:s4], wt[s4:], wkr, lane_pad, wkr_rot, lane_pad],
                               axis=0).astype(BF16)
    w["qg"] = q_norm_g.reshape(1, -1)
    w["kvg"] = kv_norm_g.reshape(1, -1)
    wq3 = w_q_up.reshape(Q_LORA_RANK, MLA_HEADS, QK_NOPE_DIM + QK_ROPE_DIM)
    nope, ropew = wq3[..., :QK_NOPE_DIM], wq3[..., QK_NOPE_DIM:]
    tail_pad = jnp.zeros((Q_LORA_RANK, MLA_HEADS, HEAD_PAD - QK_NOPE_DIM - QK_ROPE_DIM), F32)
    w["wq"] = jnp.concatenate([nope, ropew, tail_pad], -1).reshape(Q_LORA_RANK, -1).astype(BF16)
    w["wqr"] = _rot_half_cols(ropew).reshape(Q_LORA_RANK, -1).astype(BF16)
    head_pad = jnp.zeros((KV_LORA_RANK, MLA_HEADS, HEAD_PAD - QK_NOPE_DIM), F32)
    w["wuk"] = jnp.concatenate([w_uk, head_pad], -1).reshape(KV_LORA_RANK, -1).astype(BF16)
    w["wuv_t"] = w_uv.reshape(KV_LORA_RANK, -1).T.astype(BF16)
    eye = jnp.eye(MLA_HEADS, dtype=F32)
    uk_t = jnp.transpose(w_uk, (1, 2, 0))
    uk_t = jnp.concatenate([uk_t, jnp.zeros((MLA_HEADS, HEAD_PAD - QK_NOPE_DIM, KV_LORA_RANK), F32)], 1)
    w["wabs"] = jnp.einsum("hdc,hg->hdgc", uk_t, eye).reshape(MLA_HEADS * HEAD_PAD, -1).astype(BF16)
    w["wuv_bd"] = jnp.einsum("chd,hg->hcgd", w_uv, eye).reshape(MLA_HEADS * KV_LORA_RANK, -1).astype(BF16)
    n_attn = MLA_HEADS * V_HEAD_DIM
    w["woa"] = w_out[:n_attn].astype(BF16)
    w["wor"] = w_out[n_attn:].astype(BF16)
    w["conv_w"] = conv_w
    w["conv_b"] = conv_b.reshape(1, -1)
    eye_l = jnp.eye(LRU_BLOCKS, dtype=F32)
    w["wa"] = jnp.einsum("nde,nm->ndme", lru_w_a, eye_l).reshape(LRU_WIDTH, LRU_WIDTH).astype(BF16)
    w["wx"] = jnp.einsum("nde,nm->ndme", lru_w_x, eye_l).reshape(LRU_WIDTH, LRU_WIDTH).astype(BF16)
    w["ba"] = lru_b_a.reshape(1, -1)
    w["bx"] = lru_b_x.reshape(1, -1)
    w["lam"] = lru_lambda.reshape(1, -1)
    w["g_mem"] = norm_mem_g.reshape(1, -1)
    w["g_memkv"] = mem_norm_g.reshape(1, -1)
    w["w_mem_q"] = w_mem_q.astype(BF16)
    w["w_mem_k"] = w_mem_k.astype(BF16)
    w["w_mem_v"] = w_mem_v.astype(BF16)
    w["w_mem_o"] = w_mem_o.astype(BF16)
    w["g_mlp"] = norm_mlp_g.reshape(1, -1)
    w["w_up"] = w_up.astype(BF16)
    w["w_down"] = w_down.astype(BF16)
    w["g_final"] = final_norm_g.reshape(1, -1)
    return w


def _rope_tables(pos):
    pos = np.asarray(pos, np.float64)
    inv = ROPE_THETA ** (-np.arange(0, QK_ROPE_DIM, 2, dtype=np.float64) / QK_ROPE_DIM)
    ang = pos[:, None] * inv[None, :]
    cos, sin = np.cos(ang), np.sin(ang)
    n = pos.shape[0]
    z = lambda k: np.zeros((n, k))
    q_tail = HEAD_PAD - QK_NOPE_DIM - QK_ROPE_DIM
    c_exp = MLA_SCALE * LOG2E
    cq = np.concatenate([np.ones((n, QK_NOPE_DIM)), cos, cos, z(q_tail)], 1) * c_exp
    sq = np.concatenate([z(QK_NOPE_DIM), sin, sin, z(q_tail)], 1) * c_exp
    k_gap = z(_KR_ROT_LANE - QK_ROPE_DIM)
    cks = np.concatenate([cos, cos, k_gap, sin, sin, k_gap], 1)
    return tuple(jnp.asarray(t, F32) for t in (cq, sq, cks))


def kernel(x_prompt, x_sample, cache_mla, cache_mem_k, cache_mem_v, state_lru_h, state_conv, page_table, mem_prompt, norm_mix_g, w_in, q_norm_g, w_q_up, kv_norm_g, w_uk, w_uv, conv_w, conv_b, lru_w_a, lru_b_a, lru_w_x, lru_b_x, lru_lambda, w_out, norm_mem_g, mem_norm_g, w_mem_q, w_mem_k, w_mem_v, w_mem_o, norm_mlp_g, w_up, w_down, final_norm_g):
    B, S, _ = x_prompt.shape
    Bd, T, _ = x_sample.shape
    assert w_in.shape[0] == 1, "single layer"
    wts = _prep_weights(w_in[0], q_norm_g[0], w_q_up[0], kv_norm_g[0], w_uk[0], w_uv[0], conv_w[0], conv_b[0],
                        lru_w_a[0], lru_b_a[0], lru_w_x[0], lru_b_x[0], lru_lambda[0], w_out[0], norm_mem_g[0],
                        mem_norm_g[0], w_mem_q[0], w_mem_k[0], w_mem_v[0], w_mem_o[0], norm_mlp_g[0],
                        w_up[0], w_down[0], final_norm_g)
    g_mix = norm_mix_g[0].reshape(1, -1)
    tn = 512

    xp = x_prompt.reshape(B * S, D_MODEL)
    q_p, rows_p, lx_p, lg_p, k_p, v_p = _in_proj(xp, g_mix, wts, _rope_tables(np.arange(S)),
                                                 prompt=True, tn=tn, pos_blocks=S // tn)
    attn_p = _attn_prompt(q_p, k_p, v_p, batch=B, seq=S, leaf=256)
    rec_p, ht_p = _lru_prompt(lx_p, lg_p, wts, batch=B, seq=S, tt=128)
    x1_p, qm_p = _mix_out(xp, attn_p, rec_p, wts, absorbed=False, tn=tn)
    n_mem = mem_prompt.shape[1]
    mk_p, mv_p, mkb_p, mvb_p = _mem_kv(mem_prompt, wts["g_memkv"], wts["w_mem_k"], wts["w_mem_v"])
    o_p = _mem_attn(qm_p.reshape(B, S, D_MODEL), mkb_p, mvb_p, tq=512, name="mem_attn_p")
    y_p = _tail(x1_p, o_p.reshape(B * S, D_MODEL), wts, tn=tn, name="tail_p")

    ns = Bd * T
    xs = x_sample.reshape(ns, D_MODEL)
    tabs_s = _rope_tables(np.tile(PAST_LEN + np.arange(T), Bd))
    q_s, rows_s, lx_s, lg_s, qabs_s = _in_proj(xs, g_mix, wts, tabs_s, prompt=False, tn=ns, pos_blocks=1)
    rows_q = T * MLA_HEADS
    qr_s = q_s.reshape(ns * MLA_HEADS, HEAD_PAD)[:, QK_NOPE_DIM:QK_NOPE_DIM + QK_ROPE_DIM]
    ctx_s = _attn_sample(page_table, qabs_s.reshape(Bd, rows_q, KV_LORA_RANK), qr_s.reshape(Bd, rows_q, QK_ROPE_DIM),
                         rows_s.reshape(Bd, T, KV_CACHE_DIM), jnp.swapaxes(cache_mla[0], 1, 2),
                         pages=8, slots=16, pv_chunk=2048)
    to_time_major = lambda a: jnp.transpose(a.reshape(Bd, -1, LRU_WIDTH), (1, 0, 2))
    rec_t, ht_s = _lru_sample(to_time_major(lx_s), to_time_major(lg_s), to_time_major(state_conv[0]),
                              state_lru_h[0], wts)
    rec_s = jnp.transpose(rec_t, (1, 0, 2)).reshape(ns, LRU_WIDTH)
    x1_s, qm_s = _mix_out(xs, ctx_s.reshape(ns, MLA_HEADS * KV_LORA_RANK), rec_s, wts, absorbed=True, tn=ns)
    head_major = lambda a: jnp.transpose(a.reshape(Bd, T, MEM_HEADS, MEM_HEAD_DIM), (0, 2, 1, 3))
    o_s = _mem_attn_cache(head_major(qm_s).reshape(Bd, MEM_HEADS * T, MEM_HEAD_DIM), cache_mem_k[0], cache_mem_v[0],
                          steps=T)
    o_s = jnp.transpose(o_s.reshape(Bd, MEM_HEADS, T, MEM_HEAD_DIM), (0, 2, 1, 3))
    y_s = _tail(x1_s, o_s.reshape(ns, D_MODEL), wts, tn=ns, name="tail_s")

    lx_p3 = lx_p.reshape(B, S, LRU_WIDTH)
    lx_s3 = lx_s.reshape(Bd, T, LRU_WIDTH)
    keep = CONV_WIDTH - 1
    return (y_p.reshape(B, S, D_MODEL), y_s.reshape(Bd, T, D_MODEL),
            rows_p.reshape(1, B, S, KV_CACHE_DIM), rows_s.reshape(1, Bd, T, KV_CACHE_DIM),
            mk_p.reshape(1, B, n_mem, MEM_HEADS, MEM_HEAD_DIM), mv_p.reshape(1, B, n_mem, MEM_HEADS, MEM_HEAD_DIM),
            ht_p.reshape(1, B, LRU_WIDTH), ht_s.reshape(1, Bd, LRU_WIDTH),
            lx_p3[:, S - keep:].reshape(1, B, keep, LRU_WIDTH), lx_s3[:, T - keep:].reshape(1, Bd, keep, LRU_WIDTH))
```

```python
import functools
import math

import jax
import jax.numpy as jnp
import numpy as np
from jax import lax
from jax.experimental import pallas as pl
from jax.experimental.pallas import tpu as pltpu

F32 = jnp.float32
BF16 = jnp.bfloat16

D_MODEL = 1024
PAST_LEN = 16384
PAGE_SIZE = 128
MLA_HEADS = 8
QK_NOPE_DIM = 64
QK_ROPE_DIM = 32
V_HEAD_DIM = 64
Q_LORA_RANK = 384
KV_LORA_RANK = 256
KV_CACHE_DIM = KV_LORA_RANK + QK_ROPE_DIM
ROPE_THETA = 10000.0
MLA_SCALE = (QK_NOPE_DIM + QK_ROPE_DIM) ** -0.5
LRU_WIDTH = D_MODEL // 2
LRU_BLOCKS = 8
LRU_C = 8.0
CONV_WIDTH = 4
MEM_HEADS = 4
MEM_HEAD_DIM = D_MODEL // MEM_HEADS
D_FF = 4 * D_MODEL
EPS = 1e-6

LANES = 128
HEAD_PAD = LANES
LOG2E = math.log2(math.e)
VMEM_LIMIT = 48 * 1024 * 1024

_C_QLAT = 0
_C_KV = _C_QLAT + Q_LORA_RANK
_C_LX = _C_KV + KV_LORA_RANK
_C_LG = _C_LX + LRU_WIDTH
_C_KR = _C_LG + LRU_WIDTH
_KR_ROT_LANE = LANES // 2
IN_EXT = _C_KR + LANES


def _rms(x, g):
    ms = jnp.mean(x * x, axis=-1, keepdims=True)
    return x * lax.rsqrt(ms + EPS) * g


def _dot(a, b):
    return jnp.dot(a, b, preferred_element_type=F32)


def _dot_nt(a, b):
    return lax.dot_general(a, b, (((1,), (1,)), ((), ())), preferred_element_type=F32)


def _const_spec(shape):
    nd = len(shape)
    return pl.BlockSpec(shape, lambda *_: (0,) * nd, pipeline_mode=pl.Buffered(1))


def _params(sem, flags=None):
    return pltpu.CompilerParams(dimension_semantics=sem, vmem_limit_bytes=VMEM_LIMIT, flags=flags)


def _in_proj_kernel(x_ref, g_ref, win_ref, qg_ref, wq_ref, wqr_ref, kvg_ref,
                    cq_ref, sq_ref, cks_ref, *rest, prompt):
    if prompt:
        wuk_ref, wuv_ref, q_ref, rows_ref, lx_ref, lg_ref, k_ref, v_ref = rest
    else:
        wabs_ref, q_ref, rows_ref, lx_ref, lg_ref, qabs_ref = rest
    h = _rms(x_ref[...], g_ref[...]).astype(BF16)
    z = _dot_nt(h, win_ref[...])
    qa = _rms(z[:, _C_QLAT:_C_KV], qg_ref[...]).astype(BF16)
    q1 = _dot(qa, wq_ref[...])
    q2c = _dot(qa, wqr_ref[...])
    per_tile = LANES // QK_ROPE_DIM
    q2 = []
    for hd in range(MLA_HEADS):
        blk = q2c[:, (hd // per_tile) * LANES:(hd // per_tile + 1) * LANES]
        shift = (QK_NOPE_DIM - QK_ROPE_DIM * (hd % per_tile)) % LANES
        q2.append(pltpu.roll(blk, shift, 1) if shift else blk)
    cq = jnp.tile(cq_ref[...], (1, MLA_HEADS))
    sq = jnp.tile(sq_ref[...], (1, MLA_HEADS))
    qb = (q1 * cq + jnp.concatenate(q2, axis=1) * sq).astype(BF16)
    q_ref[...] = qb
    c = _rms(z[:, _C_KV:_C_LX], kvg_ref[...])
    t = z[:, _C_KR:IN_EXT] * cks_ref[...]
    kr = t + pltpu.roll(t, _KR_ROT_LANE, 1)
    rows_ref[:, 0:KV_LORA_RANK] = c
    rows_ref[:, KV_LORA_RANK:KV_CACHE_DIM] = kr[:, 0:QK_ROPE_DIM]
    lx_ref[...] = z[:, _C_LX:_C_LG]
    lg_ref[...] = z[:, _C_LG:_C_KR]
    if prompt:
        cb = c.astype(BF16)
        lane = lax.broadcasted_iota(jnp.int32, kr.shape, 1)
        kr_hi = jnp.where(lane >= QK_NOPE_DIM, kr, 0.0)
        k_ref[...] = (_dot(cb, wuk_ref[...]) + jnp.tile(kr_hi, (1, MLA_HEADS))).astype(BF16)
        v_ref[...] = _dot_nt(wuv_ref[...], cb).astype(BF16)
    else:
        for hd in range(MLA_HEADS):
            qabs_ref[:, hd * KV_LORA_RANK:(hd + 1) * KV_LORA_RANK] = _dot(
                qb[:, hd * HEAD_PAD:(hd + 1) * HEAD_PAD], wabs_ref[hd]).astype(BF16)


def _in_proj(x, g, wts, tabs, *, prompt, tn, pos_blocks):
    n = x.shape[0]
    cq, sq, cks = tabs
    assert QK_NOPE_DIM == _KR_ROT_LANE and QK_NOPE_DIM + QK_ROPE_DIM <= HEAD_PAD
    tab_spec = pl.BlockSpec((tn, LANES), lambda i: (i % pos_blocks, 0))
    row_spec = lambda w: pl.BlockSpec((tn, w), lambda i: (i, 0))
    hp = MLA_HEADS * HEAD_PAD
    in_specs = [row_spec(D_MODEL), _const_spec(g.shape)]
    in_specs += [_const_spec(wts[k].shape) for k in ("win", "qg", "wq", "wqr", "kvg")]
    in_specs += [tab_spec] * 3
    args = [x, g, wts["win"], wts["qg"], wts["wq"], wts["wqr"], wts["kvg"], cq, sq, cks]
    out_shape = [jax.ShapeDtypeStruct((n, hp), BF16), jax.ShapeDtypeStruct((n, KV_CACHE_DIM), F32),
                 jax.ShapeDtypeStruct((n, LRU_WIDTH), F32), jax.ShapeDtypeStruct((n, LRU_WIDTH), F32)]
    out_specs = [row_spec(hp), row_spec(KV_CACHE_DIM), row_spec(LRU_WIDTH), row_spec(LRU_WIDTH)]
    if prompt:
        extra = ("wuk", "wuv_t")
        n_v = MLA_HEADS * V_HEAD_DIM
        out_shape += [jax.ShapeDtypeStruct((n, hp), BF16), jax.ShapeDtypeStruct((n_v, n), BF16)]
        out_specs += [row_spec(hp), pl.BlockSpec((n_v, tn), lambda i: (0, i))]
    else:
        extra = ("wabs",)
        out_shape += [jax.ShapeDtypeStruct((n, MLA_HEADS * KV_LORA_RANK), BF16)]
        out_specs += [row_spec(MLA_HEADS * KV_LORA_RANK)]
    in_specs += [_const_spec(wts[k].shape) for k in extra]
    args += [wts[k] for k in extra]
    return pl.pallas_call(
        functools.partial(_in_proj_kernel, prompt=prompt),
        grid=(n // tn,), in_specs=in_specs, out_specs=out_specs, out_shape=out_shape,
        compiler_params=_params(("parallel",)), name="in_proj_p" if prompt else "in_proj_s",
    )(*args)


HEADS_PER_STEP = 2
SCORE_SLOTS = 2


def _attn_prompt_kernel(q_ref, k_ref, vt_ref, o_ref, s_scr, *, leaf, seq):
    col_max = lambda x: jnp.max(x, axis=0, keepdims=True)
    col_sum = lambda x: jnp.sum(x, axis=0, keepdims=True)
    tri_key = lax.broadcasted_iota(jnp.int32, (leaf, leaf), 0)
    tri_qry = lax.broadcasted_iota(jnp.int32, (leaf, leaf), 1)
    causal = lambda x: jnp.where(tri_key <= tri_qry, x, -jnp.inf)

    def lead(x, width, fill):
        return x if width == 0 else jnp.concatenate([jnp.full((x.shape[0], width), fill, F32), x], axis=1)

    heads = range(HEADS_PER_STEP)
    hcols = [slice(h * HEAD_PAD, (h + 1) * HEAD_PAD) for h in heads]
    vrows = [slice(h * V_HEAD_DIM, (h + 1) * V_HEAD_DIM) for h in heads]
    out = []
    for h in heads:
        slot = h % SCORE_SLOTS
        m = jnp.full((1, seq), -jnp.inf, F32)
        for k0 in range(0, seq, leaf):
            keys = slice(k0, k0 + leaf)
            st = _dot_nt(k_ref[keys, hcols[h]], q_ref[k0:, hcols[h]])
            st = jnp.concatenate([causal(st[:, :leaf]), st[:, leaf:]], axis=1) if k0 + leaf < seq else causal(st)
            s_scr[slot, keys, k0:] = st
            m = jnp.maximum(m, lead(col_max(st), k0, -jnp.inf))
        l = jnp.zeros((1, seq), F32)
        acc = jnp.zeros((V_HEAD_DIM, seq), F32)
        for k0 in range(0, seq, leaf):
            keys = slice(k0, k0 + leaf)
            pt = jnp.exp2(s_scr[slot, keys, k0:] - m[:, k0:])
            l = l + lead(col_sum(pt), k0, 0.0)
            acc = acc + lead(_dot(vt_ref[vrows[h], keys], pt.astype(BF16)), k0, 0.0)
        out.append(acc / l)
    o_ref[...] = jnp.transpose(jnp.concatenate(out, axis=0)).astype(BF16)


def _attn_prompt(q, k, vt, *, batch, seq, leaf):
    n = q.shape[0]
    v_blk = HEADS_PER_STEP * V_HEAD_DIM
    assert seq % leaf == 0 and leaf % LANES == 0 and v_blk % LANES == 0 and MLA_HEADS % HEADS_PER_STEP == 0
    row_blk = pl.BlockSpec((seq, HEADS_PER_STEP * HEAD_PAD), lambda b, h: (b, h))
    return pl.pallas_call(
        functools.partial(_attn_prompt_kernel, leaf=leaf, seq=seq),
        grid=(batch, MLA_HEADS // HEADS_PER_STEP),
        in_specs=[row_blk, row_blk, pl.BlockSpec((v_blk, seq), lambda b, h: (h, b))],
        out_specs=pl.BlockSpec((seq, v_blk), lambda b, h: (b, h)),
        out_shape=jax.ShapeDtypeStruct((n, MLA_HEADS * V_HEAD_DIM), BF16),
        scratch_shapes=[pltpu.VMEM((SCORE_SLOTS, seq, seq), F32)],
        compiler_params=_params(("parallel", "parallel")),
        name="attn_prompt",
    )(q, k, vt)


def _attn_sample_kernel(pt_ref, qa_ref, qr_ref, new_ref, cache_ref, ctx_ref, ring, sem, kt_scr, s_scr,
                        *, pages, n_chunks, slots, pv_chunk):
    b = pl.program_id(0)
    nb = pl.num_programs(0)
    keys = pages * PAGE_SIZE
    group = 4

    def fetch(bb, chunk, slot):
        for i in range(pages):
            pid = pt_ref[bb, chunk * pages + i]
            pltpu.make_async_copy(cache_ref.at[pid], ring.at[slot, i], sem.at[slot]).start()

    def wait(slot):
        for i in range(pages):
            pltpu.make_async_copy(cache_ref.at[0], ring.at[slot, i], sem.at[slot]).wait()

    @pl.when(b == 0)
    def _():
        for c in range(slots):
            fetch(0, c, c)

    qa = qa_ref[0]
    qr = qr_ref[0]
    rows = qa.shape[0]

    def lane_fold(x, op):
        out = x[:, 0:LANES]
        for j in range(1, x.shape[1] // LANES):
            out = op(out, x[:, j * LANES:(j + 1) * LANES])
        return out

    def score_group(g, mrun):
        c0 = g * group
        for k in range(group):
            wait(lax.rem(c0 + k, slots))
        for k in range(group):
            slot = lax.rem(c0 + k, slots)
            kt = jnp.concatenate([ring[slot, i] for i in range(pages)], axis=1).astype(BF16)
            off = pl.multiple_of((c0 + k) * keys, keys)
            kt_scr[:, pl.ds(off, keys)] = kt
            s = _dot(qa, kt[0:KV_LORA_RANK, :]) + _dot(qr, kt[KV_LORA_RANK:KV_CACHE_DIM, :])
            s_scr[:, pl.ds(off, keys)] = s
            mrun = jnp.maximum(mrun, lane_fold(s, jnp.maximum))
        nxt = c0 + slots
        wrap = nxt >= n_chunks
        bb = jnp.where(wrap, b + 1, b)
        bb = jnp.where(bb < nb, bb, 0)
        cc = jnp.where(wrap, nxt - n_chunks, nxt)
        for k in range(group):
            fetch(bb, cc + k, lax.rem(c0 + k, slots))
        return mrun

    mrun = lax.fori_loop(0, n_chunks // group, score_group, jnp.full((rows, LANES), -jnp.inf, F32))

    @pl.when(b == nb - 1)
    def _():
        for c in range(slots):
            wait(c)

    new = new_ref[0]
    n_new = new.shape[0]
    nc = new[:, 0:KV_LORA_RANK].astype(BF16).astype(F32)
    nr = new[:, KV_LORA_RANK:KV_CACHE_DIM].astype(BF16).astype(F32)
    qaf = qa.astype(F32)
    qrf = qr.astype(F32)
    tok = lax.shift_right_logical(lax.broadcasted_iota(jnp.int32, (rows, 1), 0), int(math.log2(MLA_HEADS)))
    s_new = []
    for t in range(n_new):
        st = (jnp.sum(qaf * nc[t:t + 1, :], axis=-1, keepdims=True)
              + jnp.sum(qrf * nr[t:t + 1, :], axis=-1, keepdims=True))
        s_new.append(jnp.where(tok >= t, st, -jnp.inf))

    m = jnp.max(mrun, axis=-1, keepdims=True)
    for st in s_new:
        m = jnp.maximum(m, st)

    def pv_step(j, carry):
        acc, lrun = carry
        off = pl.multiple_of(j * pv_chunk, pv_chunk)
        p = jnp.exp2(s_scr[:, pl.ds(off, pv_chunk)] - m)
        acc = acc + _dot_nt(p.astype(BF16), kt_scr[0:KV_LORA_RANK, pl.ds(off, pv_chunk)])
        return acc, lrun + lane_fold(p, jnp.add)

    acc, lrun = lax.fori_loop(0, (n_chunks * keys) // pv_chunk, pv_step,
                              (jnp.zeros((rows, KV_LORA_RANK), F32), jnp.zeros((rows, LANES), F32)), unroll=True)
    l = jnp.sum(lrun, axis=-1, keepdims=True)
    for t, st in enumerate(s_new):
        pt = jnp.exp2(st - m)
        l = l + pt
        acc = acc + pt.astype(BF16).astype(F32) * nc[t:t + 1, :]
    ctx_ref[0] = (acc / l).astype(BF16)


def _attn_sample(page_table, qabs, qrope, new_rows, cache_t, *, pages, slots, pv_chunk):
    nb, rows, _ = qabs.shape
    n_pages = page_table.shape[1]
    n_chunks = n_pages // pages
    n_keys = n_pages * PAGE_SIZE
    assert n_chunks * pages == n_pages and n_chunks % slots == 0 and slots % 4 == 0 and n_keys % pv_chunk == 0
    t_new = new_rows.shape[1]
    grid_spec = pltpu.PrefetchScalarGridSpec(
        num_scalar_prefetch=1, grid=(nb,),
        in_specs=[pl.BlockSpec((1, rows, KV_LORA_RANK), lambda b, pt: (b, 0, 0)),
                  pl.BlockSpec((1, rows, QK_ROPE_DIM), lambda b, pt: (b, 0, 0)),
                  pl.BlockSpec((1, t_new, KV_CACHE_DIM), lambda b, pt: (b, 0, 0)),
                  pl.BlockSpec(memory_space=pl.ANY)],
        out_specs=pl.BlockSpec((1, rows, KV_LORA_RANK), lambda b, pt: (b, 0, 0)),
        scratch_shapes=[pltpu.VMEM((slots, pages, KV_CACHE_DIM, PAGE_SIZE), F32),
                        pltpu.SemaphoreType.DMA((slots,)),
                        pltpu.VMEM((KV_CACHE_DIM, n_keys), BF16),
                        pltpu.VMEM((rows, n_keys), F32)])
    return pl.pallas_call(
        functools.partial(_attn_sample_kernel, pages=pages, n_chunks=n_chunks, slots=slots, pv_chunk=pv_chunk),
        grid_spec=grid_spec,
        out_shape=jax.ShapeDtypeStruct((nb, rows, KV_LORA_RANK), BF16),
        compiler_params=_params(("arbitrary",)), name="attn_sample",
    )(page_table, qabs, qrope, new_rows, cache_t)


def _lru_gates(y, wa_ref, ba_ref, wx_ref, bx_ref, lam_ref):
    yb = y.astype(BF16)
    r = jax.nn.sigmoid(_dot(yb, wa_ref[...]) + ba_ref[...])
    i = jax.nn.sigmoid(_dot(yb, wx_ref[...]) + bx_ref[...])
    nl = -lam_ref[...]
    softplus = jnp.maximum(nl, 0.0) + jnp.log1p(jnp.exp(-jnp.abs(nl)))
    log_a = -LRU_C * r * softplus
    a = jnp.exp(log_a)
    th = jnp.tanh(log_a)
    u = (y * i) * jnp.sqrt(-2.0 * th / (1.0 - th))
    return a, u


def _lru_prompt_kernel(lx_ref, lg_ref, cw_ref, cb_ref, wa_ref, ba_ref, wx_ref, bx_ref, lam_ref,
                       rec_ref, ht_ref, xp_scr, a_scr, u_scr, hs_scr, h_scr, *, tt, batch, pitch):
    ti = pl.program_id(0)
    sub = 8

    @pl.when(ti == 0)
    def _():
        xp_scr[:, 0:sub, :] = jnp.zeros((batch, sub, LRU_WIDTH), F32)
        h_scr[...] = jnp.zeros((batch, LRU_WIDTH), F32)

    cw = cw_ref[...]
    ys = []
    for b in range(batch):
        x = lx_ref[b]
        xp_scr[b, sub:sub + tt, :] = x
        y = cb_ref[...] + cw[3:4, :] * x
        for j in range(CONV_WIDTH - 1):
            y = y + cw[j:j + 1, :] * xp_scr[b, sub - 3 + j:sub - 3 + j + tt, :]
        ys.append(y)
    a, u = _lru_gates(jnp.concatenate(ys, axis=0), wa_ref, ba_ref, wx_ref, bx_ref, lam_ref)
    n_col = LRU_WIDTH // LANES
    cols = [slice(c * LANES, (c + 1) * LANES) for c in range(n_col)]
    for c in range(n_col):
        for b in range(batch):
            a_scr[c, b * pitch:b * pitch + tt, :] = a[b * tt:(b + 1) * tt, cols[c]]
            u_scr[c, b * pitch:b * pitch + tt, :] = u[b * tt:(b + 1) * tt, cols[c]]

    def step(t, h):
        rows = pl.ds(t, batch, stride=pitch)
        h = tuple(a_scr[c, rows, :] * h[c] + u_scr[c, rows, :] for c in range(n_col))
        for c in range(n_col):
            hs_scr[c, rows, :] = h[c]
        return h

    h0 = h_scr[...]
    h_last = lax.fori_loop(0, tt, step, tuple(h0[:, cols[c]] for c in range(n_col)), unroll=8)
    h_last = jnp.concatenate(h_last, axis=1)
    h_scr[...] = h_last
    ht_ref[...] = h_last
    for b in range(batch):
        hs = jnp.concatenate([hs_scr[c, b * pitch:b * pitch + tt, :] for c in range(n_col)], axis=1)
        rec_ref[b] = (hs * jax.nn.gelu(lg_ref[b])).astype(BF16)
        xp_scr[b, 0:sub, :] = xp_scr[b, tt:tt + sub, :]


def _lru_prompt(lx, lg, wts, *, batch, seq, tt):
    n = lx.shape[0]
    pitch = tt + 8
    blk = pl.BlockSpec((batch, tt, LRU_WIDTH), lambda t: (0, t, 0))
    names = ("conv_w", "conv_b", "wa", "ba", "wx", "bx", "lam")
    to3 = lambda a: a.reshape(batch, seq, LRU_WIDTH)
    rec, ht = pl.pallas_call(
        functools.partial(_lru_prompt_kernel, tt=tt, batch=batch, pitch=pitch),
        grid=(seq // tt,),
        in_specs=[blk, blk] + [_const_spec(wts[k].shape) for k in names],
        out_specs=[blk, pl.BlockSpec((batch, LRU_WIDTH), lambda t: (0, 0))],
        out_shape=[jax.ShapeDtypeStruct((batch, seq, LRU_WIDTH), BF16), jax.ShapeDtypeStruct((batch, LRU_WIDTH), F32)],
        scratch_shapes=[pltpu.VMEM((batch, tt + 8, LRU_WIDTH), F32)]
        + [pltpu.VMEM((LRU_WIDTH // LANES, batch * pitch, LANES), F32)] * 3 + [pltpu.VMEM((batch, LRU_WIDTH), F32)],
        compiler_params=_params(("arbitrary",)), name="lru_prompt",
    )(to3(lx), to3(lg), *[wts[k] for k in names])
    return rec.reshape(n, LRU_WIDTH), ht


def _lru_sample_kernel(lx_ref, lg_ref, conv_ref, h0_ref, cw_ref, cb_ref, wa_ref, ba_ref, wx_ref, bx_ref, lam_ref,
                       rec_ref, ht_ref):
    steps = lx_ref.shape[0]
    xs = [conv_ref[j] for j in range(CONV_WIDTH - 1)] + [lx_ref[t] for t in range(steps)]
    cw = cw_ref[...]
    h = h0_ref[...]
    for t in range(steps):
        y = cb_ref[...]
        for j in range(CONV_WIDTH):
            y = y + cw[j:j + 1, :] * xs[t + j]
        a, u = _lru_gates(y, wa_ref, ba_ref, wx_ref, bx_ref, lam_ref)
        h = a * h + u
        rec_ref[t] = (h * jax.nn.gelu(lg_ref[t])).astype(BF16)
    ht_ref[...] = h


def _lru_sample(lx_t, lg_t, conv_t, h0, wts):
    steps, nb, _ = lx_t.shape
    names = ("conv_w", "conv_b", "wa", "ba", "wx", "bx", "lam")
    return pl.pallas_call(
        _lru_sample_kernel,
        out_shape=[jax.ShapeDtypeStruct((steps, nb, LRU_WIDTH), BF16), jax.ShapeDtypeStruct((nb, LRU_WIDTH), F32)],
        compiler_params=pltpu.CompilerParams(vmem_limit_bytes=VMEM_LIMIT), name="lru_sample",
    )(lx_t, lg_t, conv_t, h0, *[wts[k] for k in names])


def _mem_kv_kernel(m_ref, g_ref, wk_ref, wv_ref, k4_ref, v4_ref, kb_ref, vb_ref):
    m = _rms(m_ref[0], g_ref[...]).astype(BF16)
    for w_ref, o4_ref, ob_ref in ((wk_ref, k4_ref, kb_ref), (wv_ref, v4_ref, vb_ref)):
        y = _dot(m, w_ref[...])
        ob_ref[0] = y.astype(BF16)
        for h in range(MEM_HEADS):
            o4_ref[0, :, h, :] = y[:, h * MEM_HEAD_DIM:(h + 1) * MEM_HEAD_DIM]


def _mem_kv(mem, g, wk, wv):
    nb, n_mem, _ = mem.shape
    blk3 = pl.BlockSpec((1, n_mem, D_MODEL), lambda i: (i, 0, 0))
    blk4 = pl.BlockSpec((1, n_mem, MEM_HEADS, MEM_HEAD_DIM), lambda i: (i, 0, 0, 0))
    shp4 = jax.ShapeDtypeStruct((nb, n_mem, MEM_HEADS, MEM_HEAD_DIM), F32)
    shp3 = jax.ShapeDtypeStruct((nb, n_mem, D_MODEL), BF16)
    return pl.pallas_call(
        _mem_kv_kernel, grid=(nb,),
        in_specs=[blk3, _const_spec(g.shape), _const_spec(wk.shape), _const_spec(wv.shape)],
        out_specs=[blk4, blk4, blk3, blk3], out_shape=[shp4, shp4, shp3, shp3],
        compiler_params=_params(("parallel",)), name="mem_kv",
    )(mem, g, wk, wv)


def _mix_out_kernel(x_ref, attn_ref, rec_ref, *rest, absorbed):
    if absorbed:
        wuv_ref, woa_ref, wor_ref, g_ref, wq_ref, x1_ref, qm_ref = rest
        attn = _dot(attn_ref[...], wuv_ref[...]).astype(BF16)
    else:
        woa_ref, wor_ref, g_ref, wq_ref, x1_ref, qm_ref = rest
        attn = attn_ref[...]
    x1 = x_ref[...] + _dot(attn, woa_ref[...]) + _dot(rec_ref[...], wor_ref[...])
    x1_ref[...] = x1
    qm_ref[...] = _dot(_rms(x1, g_ref[...]).astype(BF16), wq_ref[...]).astype(BF16)


def _mix_out(x, attn, rec, wts, *, absorbed, tn):
    n = x.shape[0]
    row_spec = lambda w: pl.BlockSpec((tn, w), lambda i: (i, 0))
    names = (("wuv_bd",) if absorbed else ()) + ("woa", "wor", "g_mem", "w_mem_q")
    return pl.pallas_call(
        functools.partial(_mix_out_kernel, absorbed=absorbed), grid=(n // tn,),
        in_specs=[row_spec(D_MODEL), row_spec(attn.shape[1]), row_spec(LRU_WIDTH)]
        + [_const_spec(wts[k].shape) for k in names],
        out_specs=[row_spec(D_MODEL), row_spec(D_MODEL)],
        out_shape=[jax.ShapeDtypeStruct((n, D_MODEL), F32), jax.ShapeDtypeStruct((n, D_MODEL), BF16)],
        compiler_params=_params(("parallel",)), name="mix_out_s" if absorbed else "mix_out_p",
    )(x, attn, rec, *[wts[k] for k in names])


def _mem_attend(q, k_ref, v_ref):
    scale = MEM_HEAD_DIM ** -0.5
    out = []
    for h in range(MEM_HEADS):
        lo, hi = h * MEM_HEAD_DIM, (h + 1) * MEM_HEAD_DIM
        s = _dot_nt(q[:, lo:hi], k_ref[0, :, lo:hi]) * scale
        e = jnp.exp(s - jnp.max(s, axis=-1, keepdims=True))
        p = e / jnp.sum(e, axis=-1, keepdims=True)
        out.append(_dot(p.astype(BF16), v_ref[0, :, lo:hi]).astype(BF16))
    return jnp.concatenate(out, axis=1)


def _mem_attn_cache_kernel(q_ref, k_ref, v_ref, o_ref, *, steps):
    n_mem = k_ref.shape[1]
    for g in range(q_ref.shape[0]):
        q = q_ref[g]
        k2 = k_ref[g].reshape(n_mem * MEM_HEADS, MEM_HEAD_DIM).astype(BF16)
        v2 = v_ref[g].reshape(n_mem * MEM_HEADS, MEM_HEAD_DIM).astype(BF16)
        s = _dot_nt(q, k2) * (MEM_HEAD_DIM ** -0.5)
        row_head = lax.shift_right_logical(lax.broadcasted_iota(jnp.int32, s.shape, 0), int(math.log2(steps)))
        col_head = lax.bitwise_and(lax.broadcasted_iota(jnp.int32, s.shape, 1), MEM_HEADS - 1)
        s = jnp.where(row_head == col_head, s, -jnp.inf)
        e = jnp.exp(s - jnp.max(s, axis=-1, keepdims=True))
        p = e / jnp.sum(e, axis=-1, keepdims=True)
        o_ref[g] = _dot(p.astype(BF16), v2).astype(BF16)


def _mem_attn_cache(q, k, v, *, steps, per_step=8):
    nb, rows, _ = q.shape
    qblk = pl.BlockSpec((per_step, rows, MEM_HEAD_DIM), lambda b: (b, 0, 0))
    kblk = pl.BlockSpec((per_step,) + k.shape[1:], lambda b: (b, 0, 0, 0))
    return pl.pallas_call(
        functools.partial(_mem_attn_cache_kernel, steps=steps), grid=(nb // per_step,),
        in_specs=[qblk, kblk, kblk], out_specs=qblk,
        out_shape=jax.ShapeDtypeStruct(q.shape, BF16),
        compiler_params=_params(("parallel",)), name="mem_attn_s",
    )(q, k, v)


def _tail_kernel(x1_ref, *rest, ff_chunk, attend):
    if attend:
        (qm_ref, k_ref, v_ref, qm_nxt_ref, k_nxt_ref, v_nxt_ref,
         wo_ref, g_ref, wup_ref, wdn_ref, gf_ref, y_ref, o_scr) = rest

        @pl.when(pl.program_id(0) == 0)
        def _():
            o_scr[...] = _mem_attend(qm_ref[...], k_ref, v_ref)

        o = o_scr[...]
    else:
        o_ref, wo_ref, g_ref, wup_ref, wdn_ref, gf_ref, y_ref = rest
        o = o_ref[...]
    x2 = x1_ref[...] + _dot(o, wo_ref[...])
    h = _rms(x2, g_ref[...]).astype(BF16)
    acc = x2
    for c in range(D_FF // ff_chunk):
        lo, hi = c * ff_chunk, (c + 1) * ff_chunk
        up = jnp.maximum(_dot(h, wup_ref[:, lo:hi]), 0.0)
        acc = acc + _dot((up * up).astype(BF16), wdn_ref[lo:hi, :])
    y_ref[...] = _rms(acc, gf_ref[...])
    if attend:
        o_scr[...] = _mem_attend(qm_nxt_ref[...], k_nxt_ref, v_nxt_ref)


def _tail(x1, o, wts, *, tn, name, mem_kv=None, seq=None):
    n = x1.shape[0]
    steps = n // tn
    blk = pl.BlockSpec((tn, D_MODEL), lambda i: (i, 0))
    names = ("w_mem_o", "g_mlp", "w_up", "w_down", "g_final")
    args, specs, scratch = [x1, o], [blk, blk], []
    if mem_kv is not None:
        per_seq = seq // tn
        nxt = lambda i: jnp.minimum(i + 1, steps - 1)
        kshape = (1,) + mem_kv[0].shape[1:]
        kblk = pl.BlockSpec(kshape, lambda i: (i // per_seq, 0, 0))
        kblk_nxt = pl.BlockSpec(kshape, lambda i: (nxt(i) // per_seq, 0, 0))
        args += [*mem_kv, o, *mem_kv]
        specs += [kblk, kblk, pl.BlockSpec((tn, D_MODEL), lambda i: (nxt(i), 0)), kblk_nxt, kblk_nxt]
        scratch = [pltpu.VMEM((tn, D_MODEL), BF16)]
    return pl.pallas_call(
        functools.partial(_tail_kernel, ff_chunk=1024, attend=mem_kv is not None), grid=(steps,),
        in_specs=specs + [_const_spec(wts[k].shape) for k in names],
        out_specs=blk, out_shape=jax.ShapeDtypeStruct((n, D_MODEL), F32), scratch_shapes=scratch,
        compiler_params=_params(("arbitrary",) if mem_kv is not None else ("parallel",)), name=name,
    )(*args, *[wts[k] for k in names])


def _rot_half_cols(w):
    half = QK_ROPE_DIM // 2
    return jnp.concatenate([-w[..., half:], w[..., :half]], axis=-1)


def _prep_weights(w_in, q_norm_g, w_q_up, kv_norm_g, w_uk, w_uv, conv_w, conv_b, lru_w_a, lru_b_a, lru_w_x, lru_b_x,
                  lru_lambda, w_out, norm_mem_g, mem_norm_g, w_mem_q, w_mem_k, w_mem_v, w_mem_o, norm_mlp_g,
                  w_up, w_down, final_norm_g):
    w = {}
    s1 = Q_LORA_RANK
    s2 = s1 + KV_LORA_RANK
    s3 = s2 + QK_ROPE_DIM
    s4 = s3 + LRU_WIDTH
    wt = w_in.T
    wkr = wt[s2:s3]
    half = QK_ROPE_DIM // 2
    wkr_rot = jnp.concatenate([-wkr[half:], wkr[:half]], axis=0)
    lane_pad = jnp.zeros((_KR_ROT_LANE - QK_ROPE_DIM, D_MODEL), F32)
    w["win"] = jnp.concatenate([wt[:s1], wt[s1:s2], wt[s3:s4], wt[s4:], wkr, lane_pad, wkr_rot, lane_pad],
                               axis=0).astype(BF16)
    w["qg"] = q_norm_g.reshape(1, -1)
    w["kvg"] = kv_norm_g.reshape(1, -1)
    wq3 = w_q_up.reshape(Q_LORA_RANK, MLA_HEADS, QK_NOPE_DIM + QK_ROPE_DIM)
    nope, ropew = wq3[..., :QK_NOPE_DIM], wq3[..., QK_NOPE_DIM:]
    tail_pad = jnp.zeros((Q_LORA_RANK, MLA_HEADS, HEAD_PAD - QK_NOPE_DIM - QK_ROPE_DIM), F32)
    w["wq"] = jnp.concatenate([nope, ropew, tail_pad], -1).reshape(Q_LORA_RANK, -1).astype(BF16)
    w["wqr"] = _rot_half_cols(ropew).reshape(Q_LORA_RANK, -1).astype(BF16)
    head_pad = jnp.zeros((KV_LORA_RANK, MLA_HEADS, HEAD_PAD - QK_NOPE_DIM), F32)
    w["wuk"] = jnp.concatenate([w_uk, head_pad], -1).reshape(KV_LORA_RANK, -1).astype(BF16)
    w["wuv_t"] = w_uv.reshape(KV_LORA_RANK, -1).T.astype(BF16)
    eye = jnp.eye(MLA_HEADS, dtype=F32)
    uk_t = jnp.transpose(w_uk, (1, 2, 0))
    uk_t = jnp.concatenate([uk_t, jnp.zeros((MLA_HEADS, HEAD_PAD - QK_NOPE_DIM, KV_LORA_RANK), F32)], 1)
    w["wabs"] = uk_t.astype(BF16)
    w["wuv_bd"] = jnp.einsum("chd,hg->hcgd", w_uv, eye).reshape(MLA_HEADS * KV_LORA_RANK, -1).astype(BF16)
    n_attn = MLA_HEADS * V_HEAD_DIM
    w["woa"] = w_out[:n_attn].astype(BF16)
    w["wor"] = w_out[n_attn:].astype(BF16)
    w["conv_w"] = conv_w
    w["conv_b"] = conv_b.reshape(1, -1)
    eye_l = jnp.eye(LRU_BLOCKS, dtype=F32)
    w["wa"] = jnp.einsum("nde,nm->ndme", lru_w_a, eye_l).reshape(LRU_WIDTH, LRU_WIDTH).astype(BF16)
    w["wx"] = jnp.einsum("nde,nm->ndme", lru_w_x, eye_l).reshape(LRU_WIDTH, LRU_WIDTH).astype(BF16)
    w["ba"] = lru_b_a.reshape(1, -1)
    w["bx"] = lru_b_x.reshape(1, -1)
    w["lam"] = lru_lambda.reshape(1, -1)
    w["g_mem"] = norm_mem_g.reshape(1, -1)
    w["g_memkv"] = mem_norm_g.reshape(1, -1)
    w["w_mem_q"] = w_mem_q.astype(BF16)
    w["w_mem_k"] = w_mem_k.astype(BF16)
    w["w_mem_v"] = w_mem_v.astype(BF16)
    w["w_mem_o"] = w_mem_o.astype(BF16)
    w["g_mlp"] = norm_mlp_g.reshape(1, -1)
    w["w_up"] = w_up.astype(BF16)
    w["w_down"] = w_down.astype(BF16)
    w["g_final"] = final_norm_g.reshape(1, -1)
    return w


def _rope_tables(pos):
    pos = np.asarray(pos, np.float64)
    inv = ROPE_THETA ** (-np.arange(0, QK_ROPE_DIM, 2, dtype=np.float64) / QK_ROPE_DIM)
    ang = pos[:, None] * inv[None, :]
    cos, sin = np.cos(ang), np.sin(ang)
    n = pos.shape[0]
    z = lambda k: np.zeros((n, k))
    q_tail = HEAD_PAD - QK_NOPE_DIM - QK_ROPE_DIM
    c_exp = MLA_SCALE * LOG2E
    cq = np.concatenate([np.ones((n, QK_NOPE_DIM)), cos, cos, z(q_tail)], 1) * c_exp
    sq = np.concatenate([z(QK_NOPE_DIM), sin, sin, z(q_tail)], 1) * c_exp
    k_gap = z(_KR_ROT_LANE - QK_ROPE_DIM)
    cks = np.concatenate([cos, cos, k_gap, sin, sin, k_gap], 1)
    return tuple(jnp.asarray(t, F32) for t in (cq, sq, cks))


def kernel(x_prompt, x_sample, cache_mla, cache_mem_k, cache_mem_v, state_lru_h, state_conv, page_table, mem_prompt, norm_mix_g, w_in, q_norm_g, w_q_up, kv_norm_g, w_uk, w_uv, conv_w, conv_b, lru_w_a, lru_b_a, lru_w_x, lru_b_x, lru_lambda, w_out, norm_mem_g, mem_norm_g, w_mem_q, w_mem_k, w_mem_v, w_mem_o, norm_mlp_g, w_up, w_down, final_norm_g):
    B, S, _ = x_prompt.shape
    Bd, T, _ = x_sample.shape
    assert w_in.shape[0] == 1, "single layer"
    wts = _prep_weights(w_in[0], q_norm_g[0], w_q_up[0], kv_norm_g[0], w_uk[0], w_uv[0], conv_w[0], conv_b[0],
                        lru_w_a[0], lru_b_a[0], lru_w_x[0], lru_b_x[0], lru_lambda[0], w_out[0], norm_mem_g[0],
                        mem_norm_g[0], w_mem_q[0], w_mem_k[0], w_mem_v[0], w_mem_o[0], norm_mlp_g[0],
                        w_up[0], w_down[0], final_norm_g)
    g_mix = norm_mix_g[0].reshape(1, -1)
    tn = 512

    xp = x_prompt.reshape(B * S, D_MODEL)
    q_p, rows_p, lx_p, lg_p, k_p, v_p = _in_proj(xp, g_mix, wts, _rope_tables(np.arange(S)),
                                                 prompt=True, tn=tn, pos_blocks=S // tn)
    attn_p = _attn_prompt(q_p, k_p, v_p, batch=B, seq=S, leaf=256)
    rec_p, ht_p = _lru_prompt(lx_p, lg_p, wts, batch=B, seq=S, tt=128)
    x1_p, qm_p = _mix_out(xp, attn_p, rec_p, wts, absorbed=False, tn=tn)
    n_mem = mem_prompt.shape[1]
    mk_p, mv_p, mkb_p, mvb_p = _mem_kv(mem_prompt, wts["g_memkv"], wts["w_mem_k"], wts["w_mem_v"])
    y_p = _tail(x1_p, qm_p, wts, tn=tn, name="tail_p", mem_kv=(mkb_p, mvb_p), seq=S)

    ns = Bd * T
    xs = x_sample.reshape(ns, D_MODEL)
    tabs_s = _rope_tables(np.tile(PAST_LEN + np.arange(T), Bd))
    q_s, rows_s, lx_s, lg_s, qabs_s = _in_proj(xs, g_mix, wts, tabs_s, prompt=False, tn=ns, pos_blocks=1)
    rows_q = T * MLA_HEADS
    qr_s = q_s.reshape(ns * MLA_HEADS, HEAD_PAD)[:, QK_NOPE_DIM:QK_NOPE_DIM + QK_ROPE_DIM]
    ctx_s = _attn_sample(page_table, qabs_s.reshape(Bd, rows_q, KV_LORA_RANK), qr_s.reshape(Bd, rows_q, QK_ROPE_DIM),
                         rows_s.reshape(Bd, T, KV_CACHE_DIM), jnp.swapaxes(cache_mla[0], 1, 2),
                         pages=8, slots=16, pv_chunk=2048)
    to_time_major = lambda a: jnp.transpose(a.reshape(Bd, -1, LRU_WIDTH), (1, 0, 2))
    rec_t, ht_s = _lru_sample(to_time_major(lx_s), to_time_major(lg_s), to_time_major(state_conv[0]),
                              state_lru_h[0], wts)
    rec_s = jnp.transpose(rec_t, (1, 0, 2)).reshape(ns, LRU_WIDTH)
    x1_s, qm_s = _mix_out(xs, ctx_s.reshape(ns, MLA_HEADS * KV_LORA_RANK), rec_s, wts, absorbed=True, tn=ns)
    head_major = lambda a: jnp.transpose(a.reshape(Bd, T, MEM_HEADS, MEM_HEAD_DIM), (0, 2, 1, 3))
    o_s = _mem_attn_cache(head_major(qm_s).reshape(Bd, MEM_HEADS * T, MEM_HEAD_DIM), cache_mem_k[0], cache_mem_v[0],
                          steps=T)
    o_s = jnp.transpose(o_s.reshape(Bd, MEM_HEADS, T, MEM_HEAD_DIM), (0, 2, 1, 3))
    y_s = _tail(x1_s, o_s.reshape(ns, D_MODEL), wts, tn=ns, name="tail_s")

    lx_p3 = lx_p.reshape(B, S, LRU_WIDTH)
    lx_s3 = lx_s.reshape(Bd, T, LRU_WIDTH)
    keep = CONV_WIDTH - 1
    return (y_p.reshape(B, S, D_MODEL), y_s.reshape(Bd, T, D_MODEL),
            rows_p.reshape(1, B, S, KV_CACHE_DIM), rows_s.reshape(1, Bd, T, KV_CACHE_DIM),
            mk_p.reshape(1, B, n_mem, MEM_HEADS, MEM_HEAD_DIM), mv_p.reshape(1, B, n_mem, MEM_HEADS, MEM_HEAD_DIM),
            ht_p.reshape(1, B, LRU_WIDTH), ht_s.reshape(1, Bd, LRU_WIDTH),
            lx_p3[:, S - keep:].reshape(1, B, keep, LRU_WIDTH), lx_s3[:, T - keep:].reshape(1, Bd, keep, LRU_WIDTH))
```

```python
import functools
import math

import jax
import jax.numpy as jnp
import numpy as np
from jax import lax
from jax.experimental import pallas as pl
from jax.experimental.pallas import tpu as pltpu

F32 = jnp.float32
BF16 = jnp.bfloat16

D_MODEL = 1024
PAST_LEN = 16384
PAGE_SIZE = 128
MLA_HEADS = 8
QK_NOPE_DIM = 64
QK_ROPE_DIM = 32
V_HEAD_DIM = 64
Q_LORA_RANK = 384
KV_LORA_RANK = 256
KV_CACHE_DIM = KV_LORA_RANK + QK_ROPE_DIM
ROPE_THETA = 10000.0
MLA_SCALE = (QK_NOPE_DIM + QK_ROPE_DIM) ** -0.5
LRU_WIDTH = D_MODEL // 2
LRU_BLOCKS = 8
LRU_C = 8.0
CONV_WIDTH = 4
MEM_HEADS = 4
MEM_HEAD_DIM = D_MODEL // MEM_HEADS
D_FF = 4 * D_MODEL
EPS = 1e-6

LANES = 128
HEAD_PAD = LANES
LOG2E = math.log2(math.e)
VMEM_LIMIT = 48 * 1024 * 1024

_C_QLAT = 0
_C_KV = _C_QLAT + Q_LORA_RANK
_C_LX = _C_KV + KV_LORA_RANK
_C_LG = _C_LX + LRU_WIDTH
_C_KR = _C_LG + LRU_WIDTH
_KR_ROT_LANE = LANES // 2
IN_EXT = _C_KR + LANES


def _rms(x, g):
    ms = jnp.mean(x * x, axis=-1, keepdims=True)
    return x * lax.rsqrt(ms + EPS) * g


def _dot(a, b):
    return jnp.dot(a, b, preferred_element_type=F32)


def _dot_nt(a, b):
    return lax.dot_general(a, b, (((1,), (1,)), ((), ())), preferred_element_type=F32)


def _const_spec(shape):
    nd = len(shape)
    return pl.BlockSpec(shape, lambda *_: (0,) * nd, pipeline_mode=pl.Buffered(1))


def _params(sem, flags=None):
    return pltpu.CompilerParams(dimension_semantics=sem, vmem_limit_bytes=VMEM_LIMIT, flags=flags)


def _in_proj_kernel(x_ref, g_ref, win_ref, qg_ref, wq_ref, wqr_ref, kvg_ref,
                    cq_ref, sq_ref, cks_ref, *rest, prompt):
    if prompt:
        wuk_ref, wuv_ref, q_ref, rows_ref, lx_ref, lg_ref, k_ref, v_ref = rest
    else:
        wabs_ref, q_ref, rows_ref, lx_ref, lg_ref, qabs_ref = rest
    h = _rms(x_ref[...], g_ref[...]).astype(BF16)
    z = _dot_nt(h, win_ref[...])
    qa = _rms(z[:, _C_QLAT:_C_KV], qg_ref[...]).astype(BF16)
    q1 = _dot(qa, wq_ref[...])
    q2c = _dot(qa, wqr_ref[...])
    per_tile = LANES // QK_ROPE_DIM
    q2 = []
    for hd in range(MLA_HEADS):
        blk = q2c[:, (hd // per_tile) * LANES:(hd // per_tile + 1) * LANES]
        shift = (QK_NOPE_DIM - QK_ROPE_DIM * (hd % per_tile)) % LANES
        q2.append(pltpu.roll(blk, shift, 1) if shift else blk)
    cq = jnp.tile(cq_ref[...], (1, MLA_HEADS))
    sq = jnp.tile(sq_ref[...], (1, MLA_HEADS))
    qb = (q1 * cq + jnp.concatenate(q2, axis=1) * sq).astype(BF16)
    q_ref[...] = qb
    c = _rms(z[:, _C_KV:_C_LX], kvg_ref[...])
    t = z[:, _C_KR:IN_EXT] * cks_ref[...]
    kr = t + pltpu.roll(t, _KR_ROT_LANE, 1)
    rows_ref[:, 0:KV_LORA_RANK] = c
    rows_ref[:, KV_LORA_RANK:KV_CACHE_DIM] = kr[:, 0:QK_ROPE_DIM]
    lx_ref[...] = z[:, _C_LX:_C_LG]
    lg_ref[...] = z[:, _C_LG:_C_KR]
    if prompt:
        cb = c.astype(BF16)
        lane = lax.broadcasted_iota(jnp.int32, kr.shape, 1)
        kr_hi = jnp.where(lane >= QK_NOPE_DIM, kr, 0.0)
        k_ref[...] = (_dot(cb, wuk_ref[...]) + jnp.tile(kr_hi, (1, MLA_HEADS))).astype(BF16)
        v_ref[...] = _dot_nt(wuv_ref[...], cb).astype(BF16)
    else:
        for hd in range(MLA_HEADS):
            qabs_ref[:, hd * KV_LORA_RANK:(hd + 1) * KV_LORA_RANK] = _dot(
                qb[:, hd * HEAD_PAD:(hd + 1) * HEAD_PAD], wabs_ref[hd]).astype(BF16)


def _in_proj(x, g, wts, tabs, *, prompt, tn, pos_blocks):
    n = x.shape[0]
    cq, sq, cks = tabs
    assert QK_NOPE_DIM == _KR_ROT_LANE and QK_NOPE_DIM + QK_ROPE_DIM <= HEAD_PAD
    tab_spec = pl.BlockSpec((tn, LANES), lambda i: (i % pos_blocks, 0))
    row_spec = lambda w: pl.BlockSpec((tn, w), lambda i: (i, 0))
    hp = MLA_HEADS * HEAD_PAD
    in_specs = [row_spec(D_MODEL), _const_spec(g.shape)]
    in_specs += [_const_spec(wts[k].shape) for k in ("win", "qg", "wq", "wqr", "kvg")]
    in_specs += [tab_spec] * 3
    args = [x, g, wts["win"], wts["qg"], wts["wq"], wts["wqr"], wts["kvg"], cq, sq, cks]
    out_shape = [jax.ShapeDtypeStruct((n, hp), BF16), jax.ShapeDtypeStruct((n, KV_CACHE_DIM), F32),
                 jax.ShapeDtypeStruct((n, LRU_WIDTH), F32), jax.ShapeDtypeStruct((n, LRU_WIDTH), F32)]
    out_specs = [row_spec(hp), row_spec(KV_CACHE_DIM), row_spec(LRU_WIDTH), row_spec(LRU_WIDTH)]
    if prompt:
        extra = ("wuk", "wuv_t")
        n_v = MLA_HEADS * V_HEAD_DIM
        out_shape += [jax.ShapeDtypeStruct((n, hp), BF16), jax.ShapeDtypeStruct((n_v, n), BF16)]
        out_specs += [row_spec(hp), pl.BlockSpec((n_v, tn), lambda i: (0, i))]
    else:
        extra = ("wabs",)
        out_shape += [jax.ShapeDtypeStruct((n, MLA_HEADS * KV_LORA_RANK), BF16)]
        out_specs += [row_spec(MLA_HEADS * KV_LORA_RANK)]
    in_specs += [_const_spec(wts[k].shape) for k in extra]
    args += [wts[k] for k in extra]
    return pl.pallas_call(
        functools.partial(_in_proj_kernel, prompt=prompt),
        grid=(n // tn,), in_specs=in_specs, out_specs=out_specs, out_shape=out_shape,
        compiler_params=_params(("parallel",)), name="in_proj_p" if prompt else "in_proj_s",
    )(*args)


HEADS_PER_STEP = 2
SCORE_SLOTS = 2


def _attn_prompt_kernel(q_ref, k_ref, vt_ref, o_ref, s_scr, *, leaf, seq):
    col_max = lambda x: jnp.max(x, axis=0, keepdims=True)
    col_sum = lambda x: jnp.sum(x, axis=0, keepdims=True)
    tri_key = lax.broadcasted_iota(jnp.int32, (leaf, leaf), 0)
    tri_qry = lax.broadcasted_iota(jnp.int32, (leaf, leaf), 1)
    causal = lambda x: jnp.where(tri_key <= tri_qry, x, -jnp.inf)

    def lead(x, width, fill):
        return x if width == 0 else jnp.concatenate([jnp.full((x.shape[0], width), fill, F32), x], axis=1)

    heads = range(HEADS_PER_STEP)
    hcols = [slice(h * HEAD_PAD, (h + 1) * HEAD_PAD) for h in heads]
    vrows = [slice(h * V_HEAD_DIM, (h + 1) * V_HEAD_DIM) for h in heads]
    out = []
    for h in heads:
        slot = h % SCORE_SLOTS
        m = jnp.full((1, seq), -jnp.inf, F32)
        for k0 in range(0, seq, leaf):
            keys = slice(k0, k0 + leaf)
            st = _dot_nt(k_ref[keys, hcols[h]], q_ref[k0:, hcols[h]])
            st = jnp.concatenate([causal(st[:, :leaf]), st[:, leaf:]], axis=1) if k0 + leaf < seq else causal(st)
            s_scr[slot, keys, k0:] = st
            m = jnp.maximum(m, lead(col_max(st), k0, -jnp.inf))
        l = jnp.zeros((1, seq), F32)
        acc = jnp.zeros((V_HEAD_DIM, seq), F32)
        for k0 in range(0, seq, leaf):
            keys = slice(k0, k0 + leaf)
            pt = jnp.exp2(s_scr[slot, keys, k0:] - m[:, k0:])
            l = l + lead(col_sum(pt), k0, 0.0)
            acc = acc + lead(_dot(vt_ref[vrows[h], keys], pt.astype(BF16)), k0, 0.0)
        out.append(acc / l)
    o_ref[...] = jnp.transpose(jnp.concatenate(out, axis=0)).astype(BF16)


def _attn_prompt(q, k, vt, *, batch, seq, leaf):
    n = q.shape[0]
    v_blk = HEADS_PER_STEP * V_HEAD_DIM
    assert seq % leaf == 0 and leaf % LANES == 0 and v_blk % LANES == 0 and MLA_HEADS % HEADS_PER_STEP == 0
    row_blk = pl.BlockSpec((seq, HEADS_PER_STEP * HEAD_PAD), lambda b, h: (b, h))
    return pl.pallas_call(
        functools.partial(_attn_prompt_kernel, leaf=leaf, seq=seq),
        grid=(batch, MLA_HEADS // HEADS_PER_STEP),
        in_specs=[row_blk, row_blk, pl.BlockSpec((v_blk, seq), lambda b, h: (h, b))],
        out_specs=pl.BlockSpec((seq, v_blk), lambda b, h: (b, h)),
        out_shape=jax.ShapeDtypeStruct((n, MLA_HEADS * V_HEAD_DIM), BF16),
        scratch_shapes=[pltpu.VMEM((SCORE_SLOTS, seq, seq), F32)],
        compiler_params=_params(("parallel", "parallel")),
        name="attn_prompt",
    )(q, k, vt)


def _attn_sample_kernel(pt_ref, qa_ref, qr_ref, new_ref, cache_ref, ctx_ref, ring, sem, kt_scr, s_scr,
                        *, pages, n_chunks, slots, pv_chunk):
    b = pl.program_id(0)
    nb = pl.num_programs(0)
    keys = pages * PAGE_SIZE
    group = 4

    def fetch(bb, chunk, slot):
        for i in range(pages):
            pid = pt_ref[bb, chunk * pages + i]
            pltpu.make_async_copy(cache_ref.at[pid], ring.at[slot, i], sem.at[slot]).start()

    def wait(slot):
        for i in range(pages):
            pltpu.make_async_copy(cache_ref.at[0], ring.at[slot, i], sem.at[slot]).wait()

    @pl.when(b == 0)
    def _():
        for c in range(slots):
            fetch(0, c, c)

    qa = qa_ref[0]
    qr = qr_ref[0]
    rows = qa.shape[0]

    def lane_fold(x, op):
        out = x[:, 0:LANES]
        for j in range(1, x.shape[1] // LANES):
            out = op(out, x[:, j * LANES:(j + 1) * LANES])
        return out

    def score_group(g, mrun):
        c0 = g * group
        for k in range(group):
            wait(lax.rem(c0 + k, slots))
        for k in range(group):
            slot = lax.rem(c0 + k, slots)
            kt = jnp.concatenate([ring[slot, i] for i in range(pages)], axis=1).astype(BF16)
            off = pl.multiple_of((c0 + k) * keys, keys)
            kt_scr[:, pl.ds(off, keys)] = kt
            s = _dot(qa, kt[0:KV_LORA_RANK, :]) + _dot(qr, kt[KV_LORA_RANK:KV_CACHE_DIM, :])
            s_scr[:, pl.ds(off, keys)] = s
            mrun = jnp.maximum(mrun, lane_fold(s, jnp.maximum))
        nxt = c0 + slots
        wrap = nxt >= n_chunks
        bb = jnp.where(wrap, b + 1, b)
        bb = jnp.where(bb < nb, bb, 0)
        cc = jnp.where(wrap, nxt - n_chunks, nxt)
        for k in range(group):
            fetch(bb, cc + k, lax.rem(c0 + k, slots))
        return mrun

    mrun = lax.fori_loop(0, n_chunks // group, score_group, jnp.full((rows, LANES), -jnp.inf, F32))

    @pl.when(b == nb - 1)
    def _():
        for c in range(slots):
            wait(c)

    new = new_ref[0]
    n_new = new.shape[0]
    nc = new[:, 0:KV_LORA_RANK].astype(BF16).astype(F32)
    nr = new[:, KV_LORA_RANK:KV_CACHE_DIM].astype(BF16).astype(F32)
    qaf = qa.astype(F32)
    qrf = qr.astype(F32)
    tok = lax.shift_right_logical(lax.broadcasted_iota(jnp.int32, (rows, 1), 0), int(math.log2(MLA_HEADS)))
    s_new = []
    for t in range(n_new):
        st = (jnp.sum(qaf * nc[t:t + 1, :], axis=-1, keepdims=True)
              + jnp.sum(qrf * nr[t:t + 1, :], axis=-1, keepdims=True))
        s_new.append(jnp.where(tok >= t, st, -jnp.inf))

    m = jnp.max(mrun, axis=-1, keepdims=True)
    for st in s_new:
        m = jnp.maximum(m, st)

    def pv_step(j, carry):
        acc, lrun = carry
        off = pl.multiple_of(j * pv_chunk, pv_chunk)
        p = jnp.exp2(s_scr[:, pl.ds(off, pv_chunk)] - m)
        acc = acc + _dot_nt(p.astype(BF16), kt_scr[0:KV_LORA_RANK, pl.ds(off, pv_chunk)])
        return acc, lrun + lane_fold(p, jnp.add)

    acc, lrun = lax.fori_loop(0, (n_chunks * keys) // pv_chunk, pv_step,
                              (jnp.zeros((rows, KV_LORA_RANK), F32), jnp.zeros((rows, LANES), F32)), unroll=True)
    l = jnp.sum(lrun, axis=-1, keepdims=True)
    for t, st in enumerate(s_new):
        pt = jnp.exp2(st - m)
        l = l + pt
        acc = acc + pt.astype(BF16).astype(F32) * nc[t:t + 1, :]
    ctx_ref[0] = (acc / l).astype(BF16)


def _attn_sample(page_table, qabs, qrope, new_rows, cache_t, *, pages, slots, pv_chunk):
    nb, rows, _ = qabs.shape
    n_pages = page_table.shape[1]
    n_chunks = n_pages // pages
    n_keys = n_pages * PAGE_SIZE
    assert n_chunks * pages == n_pages and n_chunks % slots == 0 and slots % 4 == 0 and n_keys % pv_chunk == 0
    t_new = new_rows.shape[1]
    grid_spec = pltpu.PrefetchScalarGridSpec(
        num_scalar_prefetch=1, grid=(nb,),
        in_specs=[pl.BlockSpec((1, rows, KV_LORA_RANK), lambda b, pt: (b, 0, 0)),
                  pl.BlockSpec((1, rows, QK_ROPE_DIM), lambda b, pt: (b, 0, 0)),
                  pl.BlockSpec((1, t_new, KV_CACHE_DIM), lambda b, pt: (b, 0, 0)),
                  pl.BlockSpec(memory_space=pl.ANY)],
        out_specs=pl.BlockSpec((1, rows, KV_LORA_RANK), lambda b, pt: (b, 0, 0)),
        scratch_shapes=[pltpu.VMEM((slots, pages, KV_CACHE_DIM, PAGE_SIZE), F32),
                        pltpu.SemaphoreType.DMA((slots,)),
                        pltpu.VMEM((KV_CACHE_DIM, n_keys), BF16),
                        pltpu.VMEM((rows, n_keys), F32)])
    return pl.pallas_call(
        functools.partial(_attn_sample_kernel, pages=pages, n_chunks=n_chunks, slots=slots, pv_chunk=pv_chunk),
        grid_spec=grid_spec,
        out_shape=jax.ShapeDtypeStruct((nb, rows, KV_LORA_RANK), BF16),
        compiler_params=_params(("arbitrary",)), name="attn_sample",
    )(page_table, qabs, qrope, new_rows, cache_t)


def _lru_gates(y, wa_ref, ba_ref, wx_ref, bx_ref, lam_ref):
    yb = y.astype(BF16)
    r = jax.nn.sigmoid(_dot(yb, wa_ref[...]) + ba_ref[...])
    i = jax.nn.sigmoid(_dot(yb, wx_ref[...]) + bx_ref[...])
    nl = -lam_ref[...]
    softplus = jnp.maximum(nl, 0.0) + jnp.log1p(jnp.exp(-jnp.abs(nl)))
    log_a = -LRU_C * r * softplus
    a = jnp.exp(log_a)
    th = jnp.tanh(log_a)
    u = (y * i) * jnp.sqrt(-2.0 * th / (1.0 - th))
    return a, u


def _lru_prompt_kernel(lx_ref, cw_ref, cb_ref, wa_ref, ba_ref, wx_ref, bx_ref, lam_ref,
                       hs_ref, ht_ref, xp_scr, a_scr, u_scr, hs_scr, h_scr, *, tt, batch, pitch):
    ti = pl.program_id(0)
    sub = 8

    @pl.when(ti == 0)
    def _():
        xp_scr[:, 0:sub, :] = jnp.zeros((batch, sub, LRU_WIDTH), F32)
        h_scr[...] = jnp.zeros((batch, LRU_WIDTH), F32)

    cw = cw_ref[...]
    ys = []
    for b in range(batch):
        x = lx_ref[b]
        xp_scr[b, sub:sub + tt, :] = x
        y = cb_ref[...] + cw[3:4, :] * x
        for j in range(CONV_WIDTH - 1):
            y = y + cw[j:j + 1, :] * xp_scr[b, sub - 3 + j:sub - 3 + j + tt, :]
        ys.append(y)
    a, u = _lru_gates(jnp.concatenate(ys, axis=0), wa_ref, ba_ref, wx_ref, bx_ref, lam_ref)
    n_col = LRU_WIDTH // LANES
    cols = [slice(c * LANES, (c + 1) * LANES) for c in range(n_col)]
    for c in range(n_col):
        for b in range(batch):
            a_scr[c, b * pitch:b * pitch + tt, :] = a[b * tt:(b + 1) * tt, cols[c]]
            u_scr[c, b * pitch:b * pitch + tt, :] = u[b * tt:(b + 1) * tt, cols[c]]

    def step(t, h):
        rows = pl.ds(t, batch, stride=pitch)
        h = tuple(a_scr[c, rows, :] * h[c] + u_scr[c, rows, :] for c in range(n_col))
        for c in range(n_col):
            hs_scr[c, rows, :] = h[c]
        return h

    h0 = h_scr[...]
    h_last = lax.fori_loop(0, tt, step, tuple(h0[:, cols[c]] for c in range(n_col)), unroll=8)
    h_last = jnp.concatenate(h_last, axis=1)
    h_scr[...] = h_last
    ht_ref[...] = h_last
    for b in range(batch):
        hs_ref[b] = jnp.concatenate([hs_scr[c, b * pitch:b * pitch + tt, :] for c in range(n_col)], axis=1)
        xp_scr[b, 0:sub, :] = xp_scr[b, tt:tt + sub, :]


def _lru_prompt(lx, wts, *, batch, seq, tt):
    n = lx.shape[0]
    pitch = tt + 8
    blk = pl.BlockSpec((batch, tt, LRU_WIDTH), lambda t: (0, t, 0))
    names = ("conv_w", "conv_b", "wa", "ba", "wx", "bx", "lam")
    hs, ht = pl.pallas_call(
        functools.partial(_lru_prompt_kernel, tt=tt, batch=batch, pitch=pitch),
        grid=(seq // tt,),
        in_specs=[blk] + [_const_spec(wts[k].shape) for k in names],
        out_specs=[blk, pl.BlockSpec((batch, LRU_WIDTH), lambda t: (0, 0))],
        out_shape=[jax.ShapeDtypeStruct((batch, seq, LRU_WIDTH), F32), jax.ShapeDtypeStruct((batch, LRU_WIDTH), F32)],
        scratch_shapes=[pltpu.VMEM((batch, tt + 8, LRU_WIDTH), F32)]
        + [pltpu.VMEM((LRU_WIDTH // LANES, batch * pitch, LANES), F32)] * 3 + [pltpu.VMEM((batch, LRU_WIDTH), F32)],
        compiler_params=_params(("arbitrary",)), name="lru_prompt",
    )(lx.reshape(batch, seq, LRU_WIDTH), *[wts[k] for k in names])
    return hs.reshape(n, LRU_WIDTH), ht


def _lru_sample_kernel(lx_ref, lg_ref, conv_ref, h0_ref, cw_ref, cb_ref, wa_ref, ba_ref, wx_ref, bx_ref, lam_ref,
                       rec_ref, ht_ref):
    steps = lx_ref.shape[0]
    xs = [conv_ref[j] for j in range(CONV_WIDTH - 1)] + [lx_ref[t] for t in range(steps)]
    cw = cw_ref[...]
    h = h0_ref[...]
    for t in range(steps):
        y = cb_ref[...]
        for j in range(CONV_WIDTH):
            y = y + cw[j:j + 1, :] * xs[t + j]
        a, u = _lru_gates(y, wa_ref, ba_ref, wx_ref, bx_ref, lam_ref)
        h = a * h + u
        rec_ref[t] = (h * jax.nn.gelu(lg_ref[t])).astype(BF16)
    ht_ref[...] = h


def _lru_sample(lx_t, lg_t, conv_t, h0, wts):
    steps, nb, _ = lx_t.shape
    names = ("conv_w", "conv_b", "wa", "ba", "wx", "bx", "lam")
    return pl.pallas_call(
        _lru_sample_kernel,
        out_shape=[jax.ShapeDtypeStruct((steps, nb, LRU_WIDTH), BF16), jax.ShapeDtypeStruct((nb, LRU_WIDTH), F32)],
        compiler_params=pltpu.CompilerParams(vmem_limit_bytes=VMEM_LIMIT), name="lru_sample",
    )(lx_t, lg_t, conv_t, h0, *[wts[k] for k in names])


def _mem_kv_kernel(m_ref, g_ref, wk_ref, wv_ref, k4_ref, v4_ref, kb_ref, vb_ref):
    m = _rms(m_ref[0], g_ref[...]).astype(BF16)
    for w_ref, o4_ref, ob_ref in ((wk_ref, k4_ref, kb_ref), (wv_ref, v4_ref, vb_ref)):
        y = _dot(m, w_ref[...])
        ob_ref[0] = y.astype(BF16)
        for h in range(MEM_HEADS):
            o4_ref[0, :, h, :] = y[:, h * MEM_HEAD_DIM:(h + 1) * MEM_HEAD_DIM]


def _mem_kv(mem, g, wk, wv):
    nb, n_mem, _ = mem.shape
    blk3 = pl.BlockSpec((1, n_mem, D_MODEL), lambda i: (i, 0, 0))
    blk4 = pl.BlockSpec((1, n_mem, MEM_HEADS, MEM_HEAD_DIM), lambda i: (i, 0, 0, 0))
    shp4 = jax.ShapeDtypeStruct((nb, n_mem, MEM_HEADS, MEM_HEAD_DIM), F32)
    shp3 = jax.ShapeDtypeStruct((nb, n_mem, D_MODEL), BF16)
    return pl.pallas_call(
        _mem_kv_kernel, grid=(nb,),
        in_specs=[blk3, _const_spec(g.shape), _const_spec(wk.shape), _const_spec(wv.shape)],
        out_specs=[blk4, blk4, blk3, blk3], out_shape=[shp4, shp4, shp3, shp3],
        compiler_params=_params(("parallel",)), name="mem_kv",
    )(mem, g, wk, wv)


def _mix_out_kernel(x_ref, attn_ref, rec_ref, *rest, absorbed):
    if absorbed:
        wuv_ref, woa_ref, wor_ref, g_ref, wq_ref, x1_ref, qm_ref = rest
        attn = _dot(attn_ref[...], wuv_ref[...]).astype(BF16)
        rec = rec_ref[...]
    else:
        gate_ref, woa_ref, wor_ref, g_ref, wq_ref, x1_ref, qm_ref = rest
        attn = attn_ref[...]
        rec = (rec_ref[...] * jax.nn.gelu(gate_ref[...])).astype(BF16)
    x1 = x_ref[...] + _dot(attn, woa_ref[...]) + _dot(rec, wor_ref[...])
    x1_ref[...] = x1
    qm_ref[...] = _dot(_rms(x1, g_ref[...]).astype(BF16), wq_ref[...]).astype(BF16)


def _mix_out(x, attn, rec, wts, *, absorbed, tn, gate=None):
    n = x.shape[0]
    assert absorbed == (gate is None)
    row_spec = lambda w: pl.BlockSpec((tn, w), lambda i: (i, 0))
    names = (("wuv_bd",) if absorbed else ()) + ("woa", "wor", "g_mem", "w_mem_q")
    acts = [x, attn, rec] + ([] if absorbed else [gate])
    return pl.pallas_call(
        functools.partial(_mix_out_kernel, absorbed=absorbed), grid=(n // tn,),
        in_specs=[row_spec(a.shape[1]) for a in acts] + [_const_spec(wts[k].shape) for k in names],
        out_specs=[row_spec(D_MODEL), row_spec(D_MODEL)],
        out_shape=[jax.ShapeDtypeStruct((n, D_MODEL), F32), jax.ShapeDtypeStruct((n, D_MODEL), BF16)],
        compiler_params=_params(("parallel",)), name="mix_out_s" if absorbed else "mix_out_p",
    )(*acts, *[wts[k] for k in names])


def _mem_attend(q, k_ref, v_ref):
    scale = MEM_HEAD_DIM ** -0.5
    out = []
    for h in range(MEM_HEADS):
        lo, hi = h * MEM_HEAD_DIM, (h + 1) * MEM_HEAD_DIM
        s = _dot_nt(q[:, lo:hi], k_ref[0, :, lo:hi]) * scale
        e = jnp.exp(s - jnp.max(s, axis=-1, keepdims=True))
        p = e / jnp.sum(e, axis=-1, keepdims=True)
        out.append(_dot(p.astype(BF16), v_ref[0, :, lo:hi]).astype(BF16))
    return jnp.concatenate(out, axis=1)


def _mem_attn_cache_kernel(q_ref, k_ref, v_ref, o_ref, *, steps):
    n_mem = k_ref.shape[1]
    for g in range(q_ref.shape[0]):
        q = q_ref[g]
        k2 = k_ref[g].reshape(n_mem * MEM_HEADS, MEM_HEAD_DIM).astype(BF16)
        v2 = v_ref[g].reshape(n_mem * MEM_HEADS, MEM_HEAD_DIM).astype(BF16)
        s = _dot_nt(q, k2) * (MEM_HEAD_DIM ** -0.5)
        row_head = lax.shift_right_logical(lax.broadcasted_iota(jnp.int32, s.shape, 0), int(math.log2(steps)))
        col_head = lax.bitwise_and(lax.broadcasted_iota(jnp.int32, s.shape, 1), MEM_HEADS - 1)
        s = jnp.where(row_head == col_head, s, -jnp.inf)
        e = jnp.exp(s - jnp.max(s, axis=-1, keepdims=True))
        p = e / jnp.sum(e, axis=-1, keepdims=True)
        o_ref[g] = _dot(p.astype(BF16), v2).astype(BF16)


def _mem_attn_cache(q, k, v, *, steps, per_step=8):
    nb, rows, _ = q.shape
    qblk = pl.BlockSpec((per_step, rows, MEM_HEAD_DIM), lambda b: (b, 0, 0))
    kblk = pl.BlockSpec((per_step,) + k.shape[1:], lambda b: (b, 0, 0, 0))
    return pl.pallas_call(
        functools.partial(_mem_attn_cache_kernel, steps=steps), grid=(nb // per_step,),
        in_specs=[qblk, kblk, kblk], out_specs=qblk,
        out_shape=jax.ShapeDtypeStruct(q.shape, BF16),
        compiler_params=_params(("parallel",)), name="mem_attn_s",
    )(q, k, v)


def _tail_kernel(x1_ref, *rest, ff_chunk, attend):
    if attend:
        (qm_ref, k_ref, v_ref, qm_nxt_ref, k_nxt_ref, v_nxt_ref,
         wo_ref, g_ref, wup_ref, wdn_ref, gf_ref, y_ref, o_scr) = rest

        @pl.when(pl.program_id(0) == 0)
        def _():
            o_scr[...] = _mem_attend(qm_ref[...], k_ref, v_ref)

        o = o_scr[...]
    else:
        o_ref, wo_ref, g_ref, wup_ref, wdn_ref, gf_ref, y_ref = rest
        o = o_ref[...]
    x2 = x1_ref[...] + _dot(o, wo_ref[...])
    h = _rms(x2, g_ref[...]).astype(BF16)
    acc = x2
    for c in range(D_FF // ff_chunk):
        lo, hi = c * ff_chunk, (c + 1) * ff_chunk
        up = jnp.maximum(_dot(h, wup_ref[:, lo:hi]), 0.0)
        acc = acc + _dot((up * up).astype(BF16), wdn_ref[lo:hi, :])
    y_ref[...] = _rms(acc, gf_ref[...])
    if attend:
        o_scr[...] = _mem_attend(qm_nxt_ref[...], k_nxt_ref, v_nxt_ref)


def _tail(x1, o, wts, *, tn, name, mem_kv=None, seq=None):
    n = x1.shape[0]
    steps = n // tn
    blk = pl.BlockSpec((tn, D_MODEL), lambda i: (i, 0))
    names = ("w_mem_o", "g_mlp", "w_up", "w_down", "g_final")
    args, specs, scratch = [x1, o], [blk, blk], []
    if mem_kv is not None:
        per_seq = seq // tn
        nxt = lambda i: jnp.minimum(i + 1, steps - 1)
        kshape = (1,) + mem_kv[0].shape[1:]
        kblk = pl.BlockSpec(kshape, lambda i: (i // per_seq, 0, 0))
        kblk_nxt = pl.BlockSpec(kshape, lambda i: (nxt(i) // per_seq, 0, 0))
        args += [*mem_kv, o, *mem_kv]
        specs += [kblk, kblk, pl.BlockSpec((tn, D_MODEL), lambda i: (nxt(i), 0)), kblk_nxt, kblk_nxt]
        scratch = [pltpu.VMEM((tn, D_MODEL), BF16)]
    return pl.pallas_call(
        functools.partial(_tail_kernel, ff_chunk=1024, attend=mem_kv is not None), grid=(steps,),
        in_specs=specs + [_const_spec(wts[k].shape) for k in names],
        out_specs=blk, out_shape=jax.ShapeDtypeStruct((n, D_MODEL), F32), scratch_shapes=scratch,
        compiler_params=_params(("arbitrary",) if mem_kv is not None else ("parallel",)), name=name,
    )(*args, *[wts[k] for k in names])


def _rot_half_cols(w):
    half = QK_ROPE_DIM // 2
    return jnp.concatenate([-w[..., half:], w[..., :half]], axis=-1)


def _prep_weights(w_in, q_norm_g, w_q_up, kv_norm_g, w_uk, w_uv, conv_w, conv_b, lru_w_a, lru_b_a, lru_w_x, lru_b_x,
                  lru_lambda, w_out, norm_mem_g, mem_norm_g, w_mem_q, w_mem_k, w_mem_v, w_mem_o, norm_mlp_g,
                  w_up, w_down, final_norm_g):
    w = {}
    s1 = Q_LORA_RANK
    s2 = s1 + KV_LORA_RANK
    s3 = s2 + QK_ROPE_DIM
    s4 = s3 + LRU_WIDTH
    wt = w_in.T
    wkr = wt[s2:s3]
    half = QK_ROPE_DIM // 2
    wkr_rot = jnp.concatenate([-wkr[half:], wkr[:half]], axis=0)
    lane_pad = jnp.zeros((_KR_ROT_LANE - QK_ROPE_DIM, D_MODEL), F32)
    w["win"] = jnp.concatenate([wt[:s1], wt[s1:s2], wt[s3:s4], wt[s4:], wkr, lane_pad, wkr_rot, lane_pad],
                               axis=0).astype(BF16)
    w["qg"] = q_norm_g.reshape(1, -1)
    w["kvg"] = kv_norm_g.reshape(1, -1)
    wq3 = w_q_up.reshape(Q_LORA_RANK, MLA_HEADS, QK_NOPE_DIM + QK_ROPE_DIM)
    nope, ropew = wq3[..., :QK_NOPE_DIM], wq3[..., QK_NOPE_DIM:]
    tail_pad = jnp.zeros((Q_LORA_RANK, MLA_HEADS, HEAD_PAD - QK_NOPE_DIM - QK_ROPE_DIM), F32)
    w["wq"] = jnp.concatenate([nope, ropew, tail_pad], -1).reshape(Q_LORA_RANK, -1).astype(BF16)
    w["wqr"] = _rot_half_cols(ropew).reshape(Q_LORA_RANK, -1).astype(BF16)
    head_pad = jnp.zeros((KV_LORA_RANK, MLA_HEADS, HEAD_PAD - QK_NOPE_DIM), F32)
    w["wuk"] = jnp.concatenate([w_uk, head_pad], -1).reshape(KV_LORA_RANK, -1).astype(BF16)
    w["wuv_t"] = w_uv.reshape(KV_LORA_RANK, -1).T.astype(BF16)
    eye = jnp.eye(MLA_HEADS, dtype=F32)
    uk_t = jnp.transpose(w_uk, (1, 2, 0))
    uk_t = jnp.concatenate([uk_t, jnp.zeros((MLA_HEADS, HEAD_PAD - QK_NOPE_DIM, KV_LORA_RANK), F32)], 1)
    w["wabs"] = uk_t.astype(BF16)
    w["wuv_bd"] = jnp.einsum("chd,hg->hcgd", w_uv, eye).reshape(MLA_HEADS * KV_LORA_RANK, -1).astype(BF16)
    n_attn = MLA_HEADS * V_HEAD_DIM
    w["woa"] = w_out[:n_attn].astype(BF16)
    w["wor"] = w_out[n_attn:].astype(BF16)
    w["conv_w"] = conv_w
    w["conv_b"] = conv_b.reshape(1, -1)
    eye_l = jnp.eye(LRU_BLOCKS, dtype=F32)
    w["wa"] = jnp.einsum("nde,nm->ndme", lru_w_a, eye_l).reshape(LRU_WIDTH, LRU_WIDTH).astype(BF16)
    w["wx"] = jnp.einsum("nde,nm->ndme", lru_w_x, eye_l).reshape(LRU_WIDTH, LRU_WIDTH).astype(BF16)
    w["ba"] = lru_b_a.reshape(1, -1)
    w["bx"] = lru_b_x.reshape(1, -1)
    w["lam"] = lru_lambda.reshape(1, -1)
    w["g_mem"] = norm_mem_g.reshape(1, -1)
    w["g_memkv"] = mem_norm_g.reshape(1, -1)
    w["w_mem_q"] = w_mem_q.astype(BF16)
    w["w_mem_k"] = w_mem_k.astype(BF16)
    w["w_mem_v"] = w_mem_v.astype(BF16)
    w["w_mem_o"] = w_mem_o.astype(BF16)
    w["g_mlp"] = norm_mlp_g.reshape(1, -1)
    w["w_up"] = w_up.astype(BF16)
    w["w_down"] = w_down.astype(BF16)
    w["g_final"] = final_norm_g.reshape(1, -1)
    return w


def _rope_tables(pos):
    pos = np.asarray(pos, np.float64)
    inv = ROPE_THETA ** (-np.arange(0, QK_ROPE_DIM, 2, dtype=np.float64) / QK_ROPE_DIM)
    ang = pos[:, None] * inv[None, :]
    cos, sin = np.cos(ang), np.sin(ang)
    n = pos.shape[0]
    z = lambda k: np.zeros((n, k))
    q_tail = HEAD_PAD - QK_NOPE_DIM - QK_ROPE_DIM
    c_exp = MLA_SCALE * LOG2E
    cq = np.concatenate([np.ones((n, QK_NOPE_DIM)), cos, cos, z(q_tail)], 1) * c_exp
    sq = np.concatenate([z(QK_NOPE_DIM), sin, sin, z(q_tail)], 1) * c_exp
    k_gap = z(_KR_ROT_LANE - QK_ROPE_DIM)
    cks = np.concatenate([cos, cos, k_gap, sin, sin, k_gap], 1)
    return tuple(jnp.asarray(t, F32) for t in (cq, sq, cks))


def kernel(x_prompt, x_sample, cache_mla, cache_mem_k, cache_mem_v, state_lru_h, state_conv, page_table, mem_prompt, norm_mix_g, w_in, q_norm_g, w_q_up, kv_norm_g, w_uk, w_uv, conv_w, conv_b, lru_w_a, lru_b_a, lru_w_x, lru_b_x, lru_lambda, w_out, norm_mem_g, mem_norm_g, w_mem_q, w_mem_k, w_mem_v, w_mem_o, norm_mlp_g, w_up, w_down, final_norm_g):
    B, S, _ = x_prompt.shape
    Bd, T, _ = x_sample.shape
    assert w_in.shape[0] == 1, "single layer"
    wts = _prep_weights(w_in[0], q_norm_g[0], w_q_up[0], kv_norm_g[0], w_uk[0], w_uv[0], conv_w[0], conv_b[0],
                        lru_w_a[0], lru_b_a[0], lru_w_x[0], lru_b_x[0], lru_lambda[0], w_out[0], norm_mem_g[0],
                        mem_norm_g[0], w_mem_q[0], w_mem_k[0], w_mem_v[0], w_mem_o[0], norm_mlp_g[0],
                        w_up[0], w_down[0], final_norm_g)
    g_mix = norm_mix_g[0].reshape(1, -1)
    tn = 512

    xp = x_prompt.reshape(B * S, D_MODEL)
    q_p, rows_p, lx_p, lg_p, k_p, v_p = _in_proj(xp, g_mix, wts, _rope_tables(np.arange(S)),
                                                 prompt=True, tn=tn, pos_blocks=S // tn)
    attn_p = _attn_prompt(q_p, k_p, v_p, batch=B, seq=S, leaf=256)
    hs_p, ht_p = _lru_prompt(lx_p, wts, batch=B, seq=S, tt=128)
    x1_p, qm_p = _mix_out(xp, attn_p, hs_p, wts, absorbed=False, tn=tn, gate=lg_p)
    n_mem = mem_prompt.shape[1]
    mk_p, mv_p, mkb_p, mvb_p = _mem_kv(mem_prompt, wts["g_memkv"], wts["w_mem_k"], wts["w_mem_v"])
    y_p = _tail(x1_p, qm_p, wts, tn=tn, name="tail_p", mem_kv=(mkb_p, mvb_p), seq=S)

    ns = Bd * T
    xs = x_sample.reshape(ns, D_MODEL)
    tabs_s = _rope_tables(np.tile(PAST_LEN + np.arange(T), Bd))
    q_s, rows_s, lx_s, lg_s, qabs_s = _in_proj(xs, g_mix, wts, tabs_s, prompt=False, tn=ns, pos_blocks=1)
    rows_q = T * MLA_HEADS
    qr_s = q_s.reshape(ns * MLA_HEADS, HEAD_PAD)[:, QK_NOPE_DIM:QK_NOPE_DIM + QK_ROPE_DIM]
    ctx_s = _attn_sample(page_table, qabs_s.reshape(Bd, rows_q, KV_LORA_RANK), qr_s.reshape(Bd, rows_q, QK_ROPE_DIM),
                         rows_s.reshape(Bd, T, KV_CACHE_DIM), jnp.swapaxes(cache_mla[0], 1, 2),
                         pages=8, slots=16, pv_chunk=2048)
    to_time_major = lambda a: jnp.transpose(a.reshape(Bd, -1, LRU_WIDTH), (1, 0, 2))
    rec_t, ht_s = _lru_sample(to_time_major(lx_s), to_time_major(lg_s), to_time_major(state_conv[0]),
                              state_lru_h[0], wts)
    rec_s = jnp.transpose(rec_t, (1, 0, 2)).reshape(ns, LRU_WIDTH)
    x1_s, qm_s = _mix_out(xs, ctx_s.reshape(ns, MLA_HEADS * KV_LORA_RANK), rec_s, wts, absorbed=True, tn=ns)
    head_major = lambda a: jnp.transpose(a.reshape(Bd, T, MEM_HEADS, MEM_HEAD_DIM), (0, 2, 1, 3))
    o_s = _mem_attn_cache(head_major(qm_s).reshape(Bd, MEM_HEADS * T, MEM_HEAD_DIM), cache_mem_k[0], cache_mem_v[0],
                          steps=T)
    o_s = jnp.transpose(o_s.reshape(Bd, MEM_HEADS, T, MEM_HEAD_DIM), (0, 2, 1, 3))
    y_s = _tail(x1_s, o_s.reshape(ns, D_MODEL), wts, tn=ns, name="tail_s")

    lx_p3 = lx_p.reshape(B, S, LRU_WIDTH)
    lx_s3 = lx_s.reshape(Bd, T, LRU_WIDTH)
    keep = CONV_WIDTH - 1
    return (y_p.reshape(B, S, D_MODEL), y_s.reshape(Bd, T, D_MODEL),
            rows_p.reshape(1, B, S, KV_CACHE_DIM), rows_s.reshape(1, Bd, T, KV_CACHE_DIM),
            mk_p.reshape(1, B, n_mem, MEM_HEADS, MEM_HEAD_DIM), mv_p.reshape(1, B, n_mem, MEM_HEADS, MEM_HEAD_DIM),
            ht_p.reshape(1, B, LRU_WIDTH), ht_s.reshape(1, Bd, LRU_WIDTH),
            lx_p3[:, S - keep:].reshape(1, B, keep, LRU_WIDTH), lx_s3[:, T - keep:].reshape(1, Bd, keep, LRU_WIDTH))
```

```python
import functools
import math

import jax
import jax.numpy as jnp
import numpy as np
from jax import lax
from jax.experimental import pallas as pl
from jax.experimental.pallas import tpu as pltpu

F32 = jnp.float32
BF16 = jnp.bfloat16

D_MODEL = 1024
PAST_LEN = 16384
PAGE_SIZE = 128
MLA_HEADS = 8
QK_NOPE_DIM = 64
QK_ROPE_DIM = 32
V_HEAD_DIM = 64
Q_LORA_RANK = 384
KV_LORA_RANK = 256
KV_CACHE_DIM = KV_LORA_RANK + QK_ROPE_DIM
ROPE_THETA = 10000.0
MLA_SCALE = (QK_NOPE_DIM + QK_ROPE_DIM) ** -0.5
LRU_WIDTH = D_MODEL // 2
LRU_BLOCKS = 8
LRU_C = 8.0
CONV_WIDTH = 4
MEM_HEADS = 4
MEM_HEAD_DIM = D_MODEL // MEM_HEADS
D_FF = 4 * D_MODEL
EPS = 1e-6

LANES = 128
SUBLANES = 8
HEAD_PAD = LANES
LOG2E = math.log2(math.e)
VMEM_LIMIT = 48 * 1024 * 1024

_C_QLAT = 0
_C_KV = _C_QLAT + Q_LORA_RANK
_C_LX = _C_KV + KV_LORA_RANK
_C_LG = _C_LX + LRU_WIDTH
_C_KR = _C_LG + LRU_WIDTH
_KR_ROT_LANE = LANES // 2
IN_EXT = _C_KR + LANES


def _rms(x, g):
    ms = jnp.mean(x * x, axis=-1, keepdims=True)
    return x * lax.rsqrt(ms + EPS) * g


def _dot(a, b):
    return jnp.dot(a, b, preferred_element_type=F32)


def _dot_nt(a, b):
    return lax.dot_general(a, b, (((1,), (1,)), ((), ())), preferred_element_type=F32)


def _const_spec(shape):
    nd = len(shape)
    return pl.BlockSpec(shape, lambda *_: (0,) * nd, pipeline_mode=pl.Buffered(1))


def _params(sem, flags=None):
    return pltpu.CompilerParams(dimension_semantics=sem, vmem_limit_bytes=VMEM_LIMIT, flags=flags)


def _in_proj_kernel(x_ref, g_ref, win_ref, qg_ref, wq_ref, wqr_ref, kvg_ref,
                    cq_ref, sq_ref, cks_ref, *rest, prompt):
    if prompt:
        wuk_ref, wuv_ref, q_ref, rows_ref, lx_ref, lg_ref, k_ref, v_ref = rest
    else:
        wabs_ref, q_ref, rows_ref, lx_ref, lg_ref, qabs_ref = rest
    h = _rms(x_ref[...], g_ref[...]).astype(BF16)
    z = _dot_nt(h, win_ref[...])
    qa = _rms(z[:, _C_QLAT:_C_KV], qg_ref[...]).astype(BF16)
    q1 = _dot(qa, wq_ref[...])
    q2c = _dot(qa, wqr_ref[...])
    per_tile = LANES // QK_ROPE_DIM
    q2 = []
    for hd in range(MLA_HEADS):
        blk = q2c[:, (hd // per_tile) * LANES:(hd // per_tile + 1) * LANES]
        shift = (QK_NOPE_DIM - QK_ROPE_DIM * (hd % per_tile)) % LANES
        q2.append(pltpu.roll(blk, shift, 1) if shift else blk)
    cq = jnp.tile(cq_ref[...], (1, MLA_HEADS))
    sq = jnp.tile(sq_ref[...], (1, MLA_HEADS))
    qb = (q1 * cq + jnp.concatenate(q2, axis=1) * sq).astype(BF16)
    q_ref[...] = qb
    c = _rms(z[:, _C_KV:_C_LX], kvg_ref[...])
    t = z[:, _C_KR:IN_EXT] * cks_ref[...]
    kr = t + pltpu.roll(t, _KR_ROT_LANE, 1)
    rows_ref[:, 0:KV_LORA_RANK] = c
    rows_ref[:, KV_LORA_RANK:KV_CACHE_DIM] = kr[:, 0:QK_ROPE_DIM]
    lx_ref[...] = z[:, _C_LX:_C_LG]
    lg_ref[...] = z[:, _C_LG:_C_KR]
    if prompt:
        cb = c.astype(BF16)
        lane = lax.broadcasted_iota(jnp.int32, kr.shape, 1)
        kr_hi = jnp.where(lane >= QK_NOPE_DIM, kr, 0.0)
        k_ref[...] = (_dot(cb, wuk_ref[...]) + jnp.tile(kr_hi, (1, MLA_HEADS))).astype(BF16)
        v_ref[...] = _dot_nt(wuv_ref[...], cb).astype(BF16)
    else:
        for hd in range(MLA_HEADS):
            qabs_ref[:, hd * KV_LORA_RANK:(hd + 1) * KV_LORA_RANK] = _dot(
                qb[:, hd * HEAD_PAD:(hd + 1) * HEAD_PAD], wabs_ref[hd]).astype(BF16)


def _in_proj(x, g, wts, tabs, *, prompt, tn, pos_blocks):
    n = x.shape[0]
    cq, sq, cks = tabs
    assert QK_NOPE_DIM == _KR_ROT_LANE and QK_NOPE_DIM + QK_ROPE_DIM <= HEAD_PAD
    tab_spec = pl.BlockSpec((tn, LANES), lambda i: (i % pos_blocks, 0))
    row_spec = lambda w: pl.BlockSpec((tn, w), lambda i: (i, 0))
    hp = MLA_HEADS * HEAD_PAD
    in_specs = [row_spec(D_MODEL), _const_spec(g.shape)]
    in_specs += [_const_spec(wts[k].shape) for k in ("win", "qg", "wq", "wqr", "kvg")]
    in_specs += [tab_spec] * 3
    args = [x, g, wts["win"], wts["qg"], wts["wq"], wts["wqr"], wts["kvg"], cq, sq, cks]
    out_shape = [jax.ShapeDtypeStruct((n, hp), BF16), jax.ShapeDtypeStruct((n, KV_CACHE_DIM), F32),
                 jax.ShapeDtypeStruct((n, LRU_WIDTH), F32), jax.ShapeDtypeStruct((n, LRU_WIDTH), F32)]
    out_specs = [row_spec(hp), row_spec(KV_CACHE_DIM), row_spec(LRU_WIDTH), row_spec(LRU_WIDTH)]
    if prompt:
        extra = ("wuk", "wuv_t")
        n_v = MLA_HEADS * V_HEAD_DIM
        out_shape += [jax.ShapeDtypeStruct((n, hp), BF16), jax.ShapeDtypeStruct((n_v, n), BF16)]
        out_specs += [row_spec(hp), pl.BlockSpec((n_v, tn), lambda i: (0, i))]
    else:
        extra = ("wabs",)
        out_shape += [jax.ShapeDtypeStruct((n, MLA_HEADS * KV_LORA_RANK), BF16)]
        out_specs += [row_spec(MLA_HEADS * KV_LORA_RANK)]
    in_specs += [_const_spec(wts[k].shape) for k in extra]
    args += [wts[k] for k in extra]
    return pl.pallas_call(
        functools.partial(_in_proj_kernel, prompt=prompt),
        grid=(n // tn,), in_specs=in_specs, out_specs=out_specs, out_shape=out_shape,
        compiler_params=_params(("parallel",)), name="in_proj_p" if prompt else "in_proj_s",
    )(*args)


HEADS_PER_STEP = 2
SCORE_SLOTS = 2


def _attn_prompt_kernel(q_ref, k_ref, vt_ref, o_ref, s_scr, *, leaf, seq):
    col_max = lambda x: jnp.max(x, axis=0, keepdims=True)
    col_sum = lambda x: jnp.sum(x, axis=0, keepdims=True)
    tri_key = lax.broadcasted_iota(jnp.int32, (leaf, leaf), 0)
    tri_qry = lax.broadcasted_iota(jnp.int32, (leaf, leaf), 1)
    causal = lambda x: jnp.where(tri_key <= tri_qry, x, -jnp.inf)

    def lead(x, width, fill):
        return x if width == 0 else jnp.concatenate([jnp.full((x.shape[0], width), fill, F32), x], axis=1)

    heads = range(HEADS_PER_STEP)
    hcols = [slice(h * HEAD_PAD, (h + 1) * HEAD_PAD) for h in heads]
    vrows = [slice(h * V_HEAD_DIM, (h + 1) * V_HEAD_DIM) for h in heads]
    out = []
    for h in heads:
        slot = h % SCORE_SLOTS
        m = jnp.full((1, seq), -jnp.inf, F32)
        for k0 in range(0, seq, leaf):
            keys = slice(k0, k0 + leaf)
            st = _dot_nt(k_ref[keys, hcols[h]], q_ref[k0:, hcols[h]])
            st = jnp.concatenate([causal(st[:, :leaf]), st[:, leaf:]], axis=1) if k0 + leaf < seq else causal(st)
            s_scr[slot, keys, k0:] = st
            m = jnp.maximum(m, lead(col_max(st), k0, -jnp.inf))
        l = jnp.zeros((1, seq), F32)
        acc = jnp.zeros((V_HEAD_DIM, seq), F32)
        for k0 in range(0, seq, leaf):
            keys = slice(k0, k0 + leaf)
            pt = jnp.exp2(s_scr[slot, keys, k0:] - m[:, k0:])
            l = l + lead(col_sum(pt), k0, 0.0)
            acc = acc + lead(_dot(vt_ref[vrows[h], keys], pt.astype(BF16)), k0, 0.0)
        out.append(acc / l)
    o_ref[...] = jnp.transpose(jnp.concatenate(out, axis=0)).astype(BF16)


def _attn_prompt(q, k, vt, *, batch, seq, leaf):
    n = q.shape[0]
    v_blk = HEADS_PER_STEP * V_HEAD_DIM
    assert seq % leaf == 0 and leaf % LANES == 0 and v_blk % LANES == 0 and MLA_HEADS % HEADS_PER_STEP == 0
    row_blk = pl.BlockSpec((seq, HEADS_PER_STEP * HEAD_PAD), lambda b, h: (b, h))
    return pl.pallas_call(
        functools.partial(_attn_prompt_kernel, leaf=leaf, seq=seq),
        grid=(batch, MLA_HEADS // HEADS_PER_STEP),
        in_specs=[row_blk, row_blk, pl.BlockSpec((v_blk, seq), lambda b, h: (h, b))],
        out_specs=pl.BlockSpec((seq, v_blk), lambda b, h: (b, h)),
        out_shape=jax.ShapeDtypeStruct((n, MLA_HEADS * V_HEAD_DIM), BF16),
        scratch_shapes=[pltpu.VMEM((SCORE_SLOTS, seq, seq), F32)],
        compiler_params=_params(("parallel", "parallel")),
        name="attn_prompt",
    )(q, k, vt)


def _attn_sample_kernel(pt_ref, qa_ref, qr_ref, new_ref, cache_ref, ctx_ref, ring, sem, kt_scr, s_scr,
                        *, pages, n_chunks, slots, pv_chunk):
    b = pl.program_id(0)
    nb = pl.num_programs(0)
    keys = pages * PAGE_SIZE
    group = 4

    def fetch(bb, chunk, slot):
        for i in range(pages):
            pid = pt_ref[bb, chunk * pages + i]
            pltpu.make_async_copy(cache_ref.at[pid], ring.at[slot, i], sem.at[slot]).start()

    def wait(slot):
        for i in range(pages):
            pltpu.make_async_copy(cache_ref.at[0], ring.at[slot, i], sem.at[slot]).wait()

    @pl.when(b == 0)
    def _():
        for c in range(slots):
            fetch(0, c, c)

    qa = qa_ref[0]
    qr = qr_ref[0]
    rows = qa.shape[0]

    def lane_fold(x, op):
        out = x[:, 0:LANES]
        for j in range(1, x.shape[1] // LANES):
            out = op(out, x[:, j * LANES:(j + 1) * LANES])
        return out

    def score_group(g, mrun):
        c0 = g * group
        for k in range(group):
            wait(lax.rem(c0 + k, slots))
        for k in range(group):
            slot = lax.rem(c0 + k, slots)
            kt = jnp.concatenate([ring[slot, i] for i in range(pages)], axis=1).astype(BF16)
            off = pl.multiple_of((c0 + k) * keys, keys)
            kt_scr[:, pl.ds(off, keys)] = kt
            s = _dot(qa, kt[0:KV_LORA_RANK, :]) + _dot(qr, kt[KV_LORA_RANK:KV_CACHE_DIM, :])
            s_scr[:, pl.ds(off, keys)] = s
            mrun = jnp.maximum(mrun, lane_fold(s, jnp.maximum))
        nxt = c0 + slots
        wrap = nxt >= n_chunks
        bb = jnp.where(wrap, b + 1, b)
        bb = jnp.where(bb < nb, bb, 0)
        cc = jnp.where(wrap, nxt - n_chunks, nxt)
        for k in range(group):
            fetch(bb, cc + k, lax.rem(c0 + k, slots))
        return mrun

    mrun = lax.fori_loop(0, n_chunks // group, score_group, jnp.full((rows, LANES), -jnp.inf, F32))

    @pl.when(b == nb - 1)
    def _():
        for c in range(slots):
            wait(c)

    new = new_ref[0]
    n_new = new.shape[0]
    nc = new[:, 0:KV_LORA_RANK].astype(BF16).astype(F32)
    nr = new[:, KV_LORA_RANK:KV_CACHE_DIM].astype(BF16).astype(F32)
    qaf = qa.astype(F32)
    qrf = qr.astype(F32)
    tok = lax.shift_right_logical(lax.broadcasted_iota(jnp.int32, (rows, 1), 0), int(math.log2(MLA_HEADS)))
    s_new = []
    for t in range(n_new):
        st = (jnp.sum(qaf * nc[t:t + 1, :], axis=-1, keepdims=True)
              + jnp.sum(qrf * nr[t:t + 1, :], axis=-1, keepdims=True))
        s_new.append(jnp.where(tok >= t, st, -jnp.inf))

    m = jnp.max(mrun, axis=-1, keepdims=True)
    for st in s_new:
        m = jnp.maximum(m, st)

    def pv_step(j, carry):
        acc, lrun = carry
        off = pl.multiple_of(j * pv_chunk, pv_chunk)
        p = jnp.exp2(s_scr[:, pl.ds(off, pv_chunk)] - m)
        acc = acc + _dot_nt(p.astype(BF16), kt_scr[0:KV_LORA_RANK, pl.ds(off, pv_chunk)])
        return acc, lrun + lane_fold(p, jnp.add)

    acc, lrun = lax.fori_loop(0, (n_chunks * keys) // pv_chunk, pv_step,
                              (jnp.zeros((rows, KV_LORA_RANK), F32), jnp.zeros((rows, LANES), F32)), unroll=True)
    l = jnp.sum(lrun, axis=-1, keepdims=True)
    for t, st in enumerate(s_new):
        pt = jnp.exp2(st - m)
        l = l + pt
        acc = acc + pt.astype(BF16).astype(F32) * nc[t:t + 1, :]
    ctx_ref[0] = (acc / l).astype(BF16)


def _attn_sample(page_table, qabs, qrope, new_rows, cache_t, *, pages, slots, pv_chunk):
    nb, rows, _ = qabs.shape
    n_pages = page_table.shape[1]
    n_chunks = n_pages // pages
    n_keys = n_pages * PAGE_SIZE
    assert n_chunks * pages == n_pages and n_chunks % slots == 0 and slots % 4 == 0 and n_keys % pv_chunk == 0
    t_new = new_rows.shape[1]
    grid_spec = pltpu.PrefetchScalarGridSpec(
        num_scalar_prefetch=1, grid=(nb,),
        in_specs=[pl.BlockSpec((1, rows, KV_LORA_RANK), lambda b, pt: (b, 0, 0)),
                  pl.BlockSpec((1, rows, QK_ROPE_DIM), lambda b, pt: (b, 0, 0)),
                  pl.BlockSpec((1, t_new, KV_CACHE_DIM), lambda b, pt: (b, 0, 0)),
                  pl.BlockSpec(memory_space=pl.ANY)],
        out_specs=pl.BlockSpec((1, rows, KV_LORA_RANK), lambda b, pt: (b, 0, 0)),
        scratch_shapes=[pltpu.VMEM((slots, pages, KV_CACHE_DIM, PAGE_SIZE), F32),
                        pltpu.SemaphoreType.DMA((slots,)),
                        pltpu.VMEM((KV_CACHE_DIM, n_keys), BF16),
                        pltpu.VMEM((rows, n_keys), F32)])
    return pl.pallas_call(
        functools.partial(_attn_sample_kernel, pages=pages, n_chunks=n_chunks, slots=slots, pv_chunk=pv_chunk),
        grid_spec=grid_spec,
        out_shape=jax.ShapeDtypeStruct((nb, rows, KV_LORA_RANK), BF16),
        compiler_params=_params(("arbitrary",)), name="attn_sample",
    )(page_table, qabs, qrope, new_rows, cache_t)


def _lru_gates(y, wa_ref, ba_ref, wx_ref, bx_ref, lam_ref):
    yb = y.astype(BF16)
    r = jax.nn.sigmoid(_dot(yb, wa_ref[...]) + ba_ref[...])
    i = jax.nn.sigmoid(_dot(yb, wx_ref[...]) + bx_ref[...])
    nl = -lam_ref[...]
    softplus = jnp.maximum(nl, 0.0) + jnp.log1p(jnp.exp(-jnp.abs(nl)))
    log_a = -LRU_C * r * softplus
    a = jnp.exp(log_a)
    th = jnp.tanh(log_a)
    u = (y * i) * jnp.sqrt(-2.0 * th / (1.0 - th))
    return a, u


def _lru_prompt_kernel(lx_ref, lg_ref, cw_ref, cb_ref, wa_ref, ba_ref, wx_ref, bx_ref, lam_ref,
                       rec_ref, ht_ref, xp_scr, a_scr, u_scr, hs_scr, h_scr, *, tt, batch, pitch):
    ti = pl.program_id(0)
    sub = SUBLANES

    @pl.when(ti == 0)
    def _():
        xp_scr[:, 0:sub, :] = jnp.zeros((batch, sub, LRU_WIDTH), F32)
        h_scr[...] = jnp.zeros((batch, LRU_WIDTH), F32)

    cw = cw_ref[...]
    ys = []
    for b in range(batch):
        x = lx_ref[b]
        xp_scr[b, sub:sub + tt, :] = x
        y = cb_ref[...] + cw[3:4, :] * x
        for j in range(CONV_WIDTH - 1):
            y = y + cw[j:j + 1, :] * xp_scr[b, sub - 3 + j:sub - 3 + j + tt, :]
        ys.append(y)
    a, u = _lru_gates(jnp.concatenate(ys, axis=0), wa_ref, ba_ref, wx_ref, bx_ref, lam_ref)
    n_col = LRU_WIDTH // LANES
    cols = [slice(c * LANES, (c + 1) * LANES) for c in range(n_col)]
    for c in range(n_col):
        for b in range(batch):
            a_scr[c, b * pitch:b * pitch + tt, :] = a[b * tt:(b + 1) * tt, cols[c]]
            u_scr[c, b * pitch:b * pitch + tt, :] = u[b * tt:(b + 1) * tt, cols[c]]

    def step(t, h):
        rows = pl.ds(t, batch, stride=pitch)
        h = tuple(a_scr[c, rows, :] * h[c] + u_scr[c, rows, :] for c in range(n_col))
        for c in range(n_col):
            hs_scr[c, rows, :] = h[c]
        return h

    h0 = h_scr[...]
    h_last = lax.fori_loop(0, tt, step, tuple(h0[:, cols[c]] for c in range(n_col)), unroll=8)
    h_last = jnp.concatenate(h_last, axis=1)
    h_scr[...] = h_last
    ht_ref[...] = h_last
    for b in range(batch):
        hs = jnp.concatenate([hs_scr[c, b * pitch:b * pitch + tt, :] for c in range(n_col)], axis=1)
        rec_ref[b] = (hs * jax.nn.gelu(lg_ref[b])).astype(BF16)
        xp_scr[b, 0:sub, :] = xp_scr[b, tt:tt + sub, :]


def _lru_prompt(lx, lg, wts, *, batch, seq, tt):
    n = lx.shape[0]
    pitch = tt + SUBLANES
    blk = pl.BlockSpec((batch, tt, LRU_WIDTH), lambda t: (0, t, 0))
    names = ("conv_w", "conv_b", "wa", "ba", "wx", "bx", "lam")
    to3 = lambda a: a.reshape(batch, seq, LRU_WIDTH)
    rec, ht = pl.pallas_call(
        functools.partial(_lru_prompt_kernel, tt=tt, batch=batch, pitch=pitch),
        grid=(seq // tt,),
        in_specs=[blk, blk] + [_const_spec(wts[k].shape) for k in names],
        out_specs=[blk, pl.BlockSpec((batch, LRU_WIDTH), lambda t: (0, 0))],
        out_shape=[jax.ShapeDtypeStruct((batch, seq, LRU_WIDTH), BF16), jax.ShapeDtypeStruct((batch, LRU_WIDTH), F32)],
        scratch_shapes=[pltpu.VMEM((batch, tt + SUBLANES, LRU_WIDTH), F32)]
        + [pltpu.VMEM((LRU_WIDTH // LANES, batch * pitch, LANES), F32)] * 3 + [pltpu.VMEM((batch, LRU_WIDTH), F32)],
        compiler_params=_params(("arbitrary",)), name="lru_prompt",
    )(to3(lx), to3(lg), *[wts[k] for k in names])
    return rec.reshape(n, LRU_WIDTH), ht


def _lru_sample_kernel(lx_ref, lg_ref, conv_ref, h0_ref, cw_ref, cb_ref, wa_ref, ba_ref, wx_ref, bx_ref, lam_ref,
                       rec_ref, ht_ref):
    steps = lx_ref.shape[0]
    xs = [conv_ref[j] for j in range(CONV_WIDTH - 1)] + [lx_ref[t] for t in range(steps)]
    cw = cw_ref[...]
    h = h0_ref[...]
    for t in range(steps):
        y = cb_ref[...]
        for j in range(CONV_WIDTH):
            y = y + cw[j:j + 1, :] * xs[t + j]
        a, u = _lru_gates(y, wa_ref, ba_ref, wx_ref, bx_ref, lam_ref)
        h = a * h + u
        rec_ref[t] = (h * jax.nn.gelu(lg_ref[t])).astype(BF16)
    ht_ref[...] = h


def _lru_sample(lx_t, lg_t, conv_t, h0, wts):
    steps, nb, _ = lx_t.shape
    names = ("conv_w", "conv_b", "wa", "ba", "wx", "bx", "lam")
    return pl.pallas_call(
        _lru_sample_kernel,
        out_shape=[jax.ShapeDtypeStruct((steps, nb, LRU_WIDTH), BF16), jax.ShapeDtypeStruct((nb, LRU_WIDTH), F32)],
        compiler_params=pltpu.CompilerParams(vmem_limit_bytes=VMEM_LIMIT), name="lru_sample",
    )(lx_t, lg_t, conv_t, h0, *[wts[k] for k in names])


def _mem_kv_kernel(m_ref, g_ref, wk_ref, wv_ref, k4_ref, v4_ref, kb_ref, vb_ref):
    m = _rms(m_ref[0], g_ref[...]).astype(BF16)
    for w_ref, o4_ref, ob_ref in ((wk_ref, k4_ref, kb_ref), (wv_ref, v4_ref, vb_ref)):
        y = _dot(m, w_ref[...])
        ob_ref[0] = y.astype(BF16)
        for h in range(MEM_HEADS):
            o4_ref[0, :, h, :] = y[:, h * MEM_HEAD_DIM:(h + 1) * MEM_HEAD_DIM]


def _mem_kv(mem, g, wk, wv):
    nb, n_mem, _ = mem.shape
    blk3 = pl.BlockSpec((1, n_mem, D_MODEL), lambda i: (i, 0, 0))
    blk4 = pl.BlockSpec((1, n_mem, MEM_HEADS, MEM_HEAD_DIM), lambda i: (i, 0, 0, 0))
    shp4 = jax.ShapeDtypeStruct((nb, n_mem, MEM_HEADS, MEM_HEAD_DIM), F32)
    shp3 = jax.ShapeDtypeStruct((nb, n_mem, D_MODEL), BF16)
    return pl.pallas_call(
        _mem_kv_kernel, grid=(nb,),
        in_specs=[blk3, _const_spec(g.shape), _const_spec(wk.shape), _const_spec(wv.shape)],
        out_specs=[blk4, blk4, blk3, blk3], out_shape=[shp4, shp4, shp3, shp3],
        compiler_params=_params(("parallel",)), name="mem_kv",
    )(mem, g, wk, wv)


def _mix_out_kernel(x_ref, attn_ref, rec_ref, *rest, absorbed):
    if absorbed:
        wuv_ref, woa_ref, wor_ref, g_ref, wq_ref, x1_ref, qm_ref = rest
        attn = _dot(attn_ref[...], wuv_ref[...]).astype(BF16)
    else:
        woa_ref, wor_ref, g_ref, wq_ref, x1_ref, qm_ref = rest
        attn = attn_ref[...]
    x1 = x_ref[...] + _dot(attn, woa_ref[...]) + _dot(rec_ref[...], wor_ref[...])
    x1_ref[...] = x1
    qm_ref[...] = _dot(_rms(x1, g_ref[...]).astype(BF16), wq_ref[...]).astype(BF16)


def _mix_out(x, attn, rec, wts, *, absorbed, tn):
    n = x.shape[0]
    row_spec = lambda w: pl.BlockSpec((tn, w), lambda i: (i, 0))
    names = (("wuv_bd",) if absorbed else ()) + ("woa", "wor", "g_mem", "w_mem_q")
    return pl.pallas_call(
        functools.partial(_mix_out_kernel, absorbed=absorbed), grid=(n // tn,),
        in_specs=[row_spec(D_MODEL), row_spec(attn.shape[1]), row_spec(LRU_WIDTH)]
        + [_const_spec(wts[k].shape) for k in names],
        out_specs=[row_spec(D_MODEL), row_spec(D_MODEL)],
        out_shape=[jax.ShapeDtypeStruct((n, D_MODEL), F32), jax.ShapeDtypeStruct((n, D_MODEL), BF16)],
        compiler_params=_params(("parallel",)), name="mix_out_s" if absorbed else "mix_out_p",
    )(x, attn, rec, *[wts[k] for k in names])


def _mem_attend(q, k_ref, v_ref):
    scale = MEM_HEAD_DIM ** -0.5
    out = []
    for h in range(MEM_HEADS):
        lo, hi = h * MEM_HEAD_DIM, (h + 1) * MEM_HEAD_DIM
        s = _dot_nt(q[:, lo:hi], k_ref[0, :, lo:hi]) * scale
        e = jnp.exp(s - jnp.max(s, axis=-1, keepdims=True))
        p = e / jnp.sum(e, axis=-1, keepdims=True)
        out.append(_dot(p.astype(BF16), v_ref[0, :, lo:hi]).astype(BF16))
    return jnp.concatenate(out, axis=1)


def _mem_attn_cache_kernel(q_ref, k_ref, v_ref, o_ref, *, steps):
    n_mem = k_ref.shape[1]
    for g in range(q_ref.shape[0]):
        q = q_ref[g]
        k2 = k_ref[g].reshape(n_mem * MEM_HEADS, MEM_HEAD_DIM).astype(BF16)
        v2 = v_ref[g].reshape(n_mem * MEM_HEADS, MEM_HEAD_DIM).astype(BF16)
        s = _dot_nt(q, k2) * (MEM_HEAD_DIM ** -0.5)
        row_head = lax.shift_right_logical(lax.broadcasted_iota(jnp.int32, s.shape, 0), int(math.log2(steps)))
        col_head = lax.bitwise_and(lax.broadcasted_iota(jnp.int32, s.shape, 1), MEM_HEADS - 1)
        s = jnp.where(row_head == col_head, s, -jnp.inf)
        e = jnp.exp(s - jnp.max(s, axis=-1, keepdims=True))
        p = e / jnp.sum(e, axis=-1, keepdims=True)
        o_ref[g] = _dot(p.astype(BF16), v2).astype(BF16)


def _mem_attn_cache(q, k, v, *, steps, per_step=8):
    nb, rows, _ = q.shape
    qblk = pl.BlockSpec((per_step, rows, MEM_HEAD_DIM), lambda b: (b, 0, 0))
    kblk = pl.BlockSpec((per_step,) + k.shape[1:], lambda b: (b, 0, 0, 0))
    return pl.pallas_call(
        functools.partial(_mem_attn_cache_kernel, steps=steps), grid=(nb // per_step,),
        in_specs=[qblk, kblk, kblk], out_specs=qblk,
        out_shape=jax.ShapeDtypeStruct(q.shape, BF16),
        compiler_params=_params(("parallel",)), name="mem_attn_s",
    )(q, k, v)


def _tail_kernel(x1_ref, *rest, ff_chunk, attend):
    if attend:
        (qm_ref, k_ref, v_ref, qm_nxt_ref, k_nxt_ref, v_nxt_ref,
         wo_ref, g_ref, wup_ref, wdn_ref, gf_ref, y_ref, o_scr) = rest

        @pl.when(pl.program_id(0) == 0)
        def _():
            o_scr[...] = _mem_attend(qm_ref[...], k_ref, v_ref)

        o = o_scr[...]
    else:
        o_ref, wo_ref, g_ref, wup_ref, wdn_ref, gf_ref, y_ref = rest
        o = o_ref[...]
    x2 = x1_ref[...] + _dot(o, wo_ref[...])
    h = _rms(x2, g_ref[...]).astype(BF16)
    acc = x2
    for c in range(D_FF // ff_chunk):
        lo, hi = c * ff_chunk, (c + 1) * ff_chunk
        up = jnp.maximum(_dot(h, wup_ref[:, lo:hi]), 0.0)
        acc = acc + _dot((up * up).astype(BF16), wdn_ref[lo:hi, :])
    y_ref[...] = _rms(acc, gf_ref[...])
    if attend:
        o_scr[...] = _mem_attend(qm_nxt_ref[...], k_nxt_ref, v_nxt_ref)


def _tail(x1, o, wts, *, tn, name, mem_kv=None, seq=None):
    n = x1.shape[0]
    steps = n // tn
    blk = pl.BlockSpec((tn, D_MODEL), lambda i: (i, 0))
    names = ("w_mem_o", "g_mlp", "w_up", "w_down", "g_final")
    args, specs, scratch = [x1, o], [blk, blk], []
    if mem_kv is not None:
        per_seq = seq // tn
        nxt = lambda i: jnp.minimum(i + 1, steps - 1)
        kshape = (1,) + mem_kv[0].shape[1:]
        kblk = pl.BlockSpec(kshape, lambda i: (i // per_seq, 0, 0))
        kblk_nxt = pl.BlockSpec(kshape, lambda i: (nxt(i) // per_seq, 0, 0))
        args += [*mem_kv, o, *mem_kv]
        specs += [kblk, kblk, pl.BlockSpec((tn, D_MODEL), lambda i: (nxt(i), 0)), kblk_nxt, kblk_nxt]
        scratch = [pltpu.VMEM((tn, D_MODEL), BF16)]
    return pl.pallas_call(
        functools.partial(_tail_kernel, ff_chunk=1024, attend=mem_kv is not None), grid=(steps,),
        in_specs=specs + [_const_spec(wts[k].shape) for k in names],
        out_specs=blk, out_shape=jax.ShapeDtypeStruct((n, D_MODEL), F32), scratch_shapes=scratch,
        compiler_params=_params(("arbitrary",) if mem_kv is not None else ("parallel",)), name=name,
    )(*args, *[wts[k] for k in names])


def _rot_half_cols(w):
    half = QK_ROPE_DIM // 2
    return jnp.concatenate([-w[..., half:], w[..., :half]], axis=-1)


def _prep_weights(w_in, q_norm_g, w_q_up, kv_norm_g, w_uk, w_uv, conv_w, conv_b, lru_w_a, lru_b_a, lru_w_x, lru_b_x,
                  lru_lambda, w_out, norm_mem_g, mem_norm_g, w_mem_q, w_mem_k, w_mem_v, w_mem_o, norm_mlp_g,
                  w_up, w_down, final_norm_g):
    w = {}
    s1 = Q_LORA_RANK
    s2 = s1 + KV_LORA_RANK
    s3 = s2 + QK_ROPE_DIM
    s4 = s3 + LRU_WIDTH
    wt = w_in.T
    wkr = wt[s2:s3]
    half = QK_ROPE_DIM // 2
    wkr_rot = jnp.concatenate([-wkr[half:], wkr[:half]], axis=0)
    lane_pad = jnp.zeros((_KR_ROT_LANE - QK_ROPE_DIM, D_MODEL), F32)
    w["win"] = jnp.concatenate([wt[:s1], wt[s1:s2], wt[s3:s4], wt[s4:], wkr, lane_pad, wkr_rot, lane_pad],
                               axis=0).astype(BF16)
    w["qg"] = q_norm_g.reshape(1, -1)
    w["kvg"] = kv_norm_g.reshape(1, -1)
    wq3 = w_q_up.reshape(Q_LORA_RANK, MLA_HEADS, QK_NOPE_DIM + QK_ROPE_DIM)
    nope, ropew = wq3[..., :QK_NOPE_DIM], wq3[..., QK_NOPE_DIM:]
    tail_pad = jnp.zeros((Q_LORA_RANK, MLA_HEADS, HEAD_PAD - QK_NOPE_DIM - QK_ROPE_DIM), F32)
    w["wq"] = jnp.concatenate([nope, ropew, tail_pad], -1).reshape(Q_LORA_RANK, -1).astype(BF16)
    w["wqr"] = _rot_half_cols(ropew).reshape(Q_LORA_RANK, -1).astype(BF16)
    head_pad = jnp.zeros((KV_LORA_RANK, MLA_HEADS, HEAD_PAD - QK_NOPE_DIM), F32)
    w["wuk"] = jnp.concatenate([w_uk, head_pad], -1).reshape(KV_LORA_RANK, -1).astype(BF16)
    w["wuv_t"] = w_uv.reshape(KV_LORA_RANK, -1).T.astype(BF16)
    eye = jnp.eye(MLA_HEADS, dtype=F32)
    uk_t = jnp.transpose(w_uk, (1, 2, 0))
    uk_t = jnp.concatenate([uk_t, jnp.zeros((MLA_HEADS, HEAD_PAD - QK_NOPE_DIM, KV_LORA_RANK), F32)], 1)
    w["wabs"] = uk_t.astype(BF16)
    w["wuv_bd"] = jnp.einsum("chd,hg->hcgd", w_uv, eye).reshape(MLA_HEADS * KV_LORA_RANK, -1).astype(BF16)
    n_attn = MLA_HEADS * V_HEAD_DIM
    w["woa"] = w_out[:n_attn].astype(BF16)
    w["wor"] = w_out[n_attn:].astype(BF16)
    w["conv_w"] = conv_w
    w["conv_b"] = conv_b.reshape(1, -1)
    eye_l = jnp.eye(LRU_BLOCKS, dtype=F32)
    w["wa"] = jnp.einsum("nde,nm->ndme", lru_w_a, eye_l).reshape(LRU_WIDTH, LRU_WIDTH).astype(BF16)
    w["wx"] = jnp.einsum("nde,nm->ndme", lru_w_x, eye_l).reshape(LRU_WIDTH, LRU_WIDTH).astype(BF16)
    w["ba"] = lru_b_a.reshape(1, -1)
    w["bx"] = lru_b_x.reshape(1, -1)
    w["lam"] = lru_lambda.reshape(1, -1)
    w["g_mem"] = norm_mem_g.reshape(1, -1)
    w["g_memkv"] = mem_norm_g.reshape(1, -1)
    w["w_mem_q"] = w_mem_q.astype(BF16)
    w["w_mem_k"] = w_mem_k.astype(BF16)
    w["w_mem_v"] = w_mem_v.astype(BF16)
    w["w_mem_o"] = w_mem_o.astype(BF16)
    w["g_mlp"] = norm_mlp_g.reshape(1, -1)
    w["w_up"] = w_up.astype(BF16)
    w["w_down"] = w_down.astype(BF16)
    w["g_final"] = final_norm_g.reshape(1, -1)
    return w


def _rope_tables(pos):
    pos = np.asarray(pos, np.float64)
    inv = ROPE_THETA ** (-np.arange(0, QK_ROPE_DIM, 2, dtype=np.float64) / QK_ROPE_DIM)
    ang = pos[:, None] * inv[None, :]
    cos, sin = np.cos(ang), np.sin(ang)
    n = pos.shape[0]
    z = lambda k: np.zeros((n, k))
    q_tail = HEAD_PAD - QK_NOPE_DIM - QK_ROPE_DIM
    c_exp = MLA_SCALE * LOG2E
    cq = np.concatenate([np.ones((n, QK_NOPE_DIM)), cos, cos, z(q_tail)], 1) * c_exp
    sq = np.concatenate([z(QK_NOPE_DIM), sin, sin, z(q_tail)], 1) * c_exp
    k_gap = z(_KR_ROT_LANE - QK_ROPE_DIM)
    cks = np.concatenate([cos, cos, k_gap, sin, sin, k_gap], 1)
    return tuple(jnp.asarray(t, F32) for t in (cq, sq, cks))


def kernel(x_prompt, x_sample, cache_mla, cache_mem_k, cache_mem_v, state_lru_h, state_conv, page_table, mem_prompt, norm_mix_g, w_in, q_norm_g, w_q_up, kv_norm_g, w_uk, w_uv, conv_w, conv_b, lru_w_a, lru_b_a, lru_w_x, lru_b_x, lru_lambda, w_out, norm_mem_g, mem_norm_g, w_mem_q, w_mem_k, w_mem_v, w_mem_o, norm_mlp_g, w_up, w_down, final_norm_g):
    B, S, _ = x_prompt.shape
    Bd, T, _ = x_sample.shape
    assert w_in.shape[0] == 1, "single layer"
    wts = _prep_weights(w_in[0], q_norm_g[0], w_q_up[0], kv_norm_g[0], w_uk[0], w_uv[0], conv_w[0], conv_b[0],
                        lru_w_a[0], lru_b_a[0], lru_w_x[0], lru_b_x[0], lru_lambda[0], w_out[0], norm_mem_g[0],
                        mem_norm_g[0], w_mem_q[0], w_mem_k[0], w_mem_v[0], w_mem_o[0], norm_mlp_g[0],
                        w_up[0], w_down[0], final_norm_g)
    g_mix = norm_mix_g[0].reshape(1, -1)
    tn = 512

    xp = x_prompt.reshape(B * S, D_MODEL)
    q_p, rows_p, lx_p, lg_p, k_p, v_p = _in_proj(xp, g_mix, wts, _rope_tables(np.arange(S)),
                                                 prompt=True, tn=tn, pos_blocks=S // tn)
    attn_p = _attn_prompt(q_p, k_p, v_p, batch=B, seq=S, leaf=256)
    rec_p, ht_p = _lru_prompt(lx_p, lg_p, wts, batch=B, seq=S, tt=128)
    x1_p, qm_p = _mix_out(xp, attn_p, rec_p, wts, absorbed=False, tn=tn)
    n_mem = mem_prompt.shape[1]
    mk_p, mv_p, mkb_p, mvb_p = _mem_kv(mem_prompt, wts["g_memkv"], wts["w_mem_k"], wts["w_mem_v"])
    y_p = _tail(x1_p, qm_p, wts, tn=tn, name="tail_p", mem_kv=(mkb_p, mvb_p), seq=S)

    ns = Bd * T
    xs = x_sample.reshape(ns, D_MODEL)
    tabs_s = _rope_tables(np.tile(PAST_LEN + np.arange(T), Bd))
    q_s, rows_s, lx_s, lg_s, qabs_s = _in_proj(xs, g_mix, wts, tabs_s, prompt=False, tn=ns, pos_blocks=1)
    rows_q = T * MLA_HEADS
    qr_s = q_s.reshape(ns * MLA_HEADS, HEAD_PAD)[:, QK_NOPE_DIM:QK_NOPE_DIM + QK_ROPE_DIM]
    ctx_s = _attn_sample(page_table, qabs_s.reshape(Bd, rows_q, KV_LORA_RANK), qr_s.reshape(Bd, rows_q, QK_ROPE_DIM),
                         rows_s.reshape(Bd, T, KV_CACHE_DIM), jnp.swapaxes(cache_mla[0], 1, 2),
                         pages=8, slots=16, pv_chunk=2048)
    to_time_major = lambda a: jnp.transpose(a.reshape(Bd, -1, LRU_WIDTH), (1, 0, 2))
    rec_t, ht_s = _lru_sample(to_time_major(lx_s), to_time_major(lg_s), to_time_major(state_conv[0]),
                              state_lru_h[0], wts)
    rec_s = jnp.transpose(rec_t, (1, 0, 2)).reshape(ns, LRU_WIDTH)
    x1_s, qm_s = _mix_out(xs, ctx_s.reshape(ns, MLA_HEADS * KV_LORA_RANK), rec_s, wts, absorbed=True, tn=ns)
    head_major = lambda a: jnp.transpose(a.reshape(Bd, T, MEM_HEADS, MEM_HEAD_DIM), (0, 2, 1, 3))
    o_s = _mem_attn_cache(head_major(qm_s).reshape(Bd, MEM_HEADS * T, MEM_HEAD_DIM), cache_mem_k[0], cache_mem_v[0],
                          steps=T)
    o_s = jnp.transpose(o_s.reshape(Bd, MEM_HEADS, T, MEM_HEAD_DIM), (0, 2, 1, 3))
    y_s = _tail(x1_s, o_s.reshape(ns, D_MODEL), wts, tn=ns, name="tail_s")

    lx_p3 = lx_p.reshape(B, S, LRU_WIDTH)
    lx_s3 = lx_s.reshape(Bd, T, LRU_WIDTH)
    keep = CONV_WIDTH - 1
    return (y_p.reshape(B, S, D_MODEL), y_s.reshape(Bd, T, D_MODEL),
            rows_p.reshape(1, B, S, KV_CACHE_DIM), rows_s.reshape(1, Bd, T, KV_CACHE_DIM),
            mk_p.reshape(1, B, n_mem, MEM_HEADS, MEM_HEAD_DIM), mv_p.reshape(1, B, n_mem, MEM_HEADS, MEM_HEAD_DIM),
            ht_p.reshape(1, B, LRU_WIDTH), ht_s.reshape(1, Bd, LRU_WIDTH),
            lx_p3[:, S - keep:].reshape(1, B, keep, LRU_WIDTH), lx_s3[:, T - keep:].reshape(1, Bd, keep, LRU_WIDTH))
```

```python
import functools
import math

import jax
import jax.numpy as jnp
import numpy as np
from jax import lax
from jax.experimental import pallas as pl
from jax.experimental.pallas import tpu as pltpu

F32 = jnp.float32
BF16 = jnp.bfloat16

D_MODEL = 1024
PAST_LEN = 16384
PAGE_SIZE = 128
MLA_HEADS = 8
QK_NOPE_DIM = 64
QK_ROPE_DIM = 32
V_HEAD_DIM = 64
Q_LORA_RANK = 384
KV_LORA_RANK = 256
KV_CACHE_DIM = KV_LORA_RANK + QK_ROPE_DIM
ROPE_THETA = 10000.0
MLA_SCALE = (QK_NOPE_DIM + QK_ROPE_DIM) ** -0.5
LRU_WIDTH = D_MODEL // 2
LRU_BLOCKS = 8
LRU_C = 8.0
CONV_WIDTH = 4
MEM_HEADS = 4
MEM_HEAD_DIM = D_MODEL // MEM_HEADS
D_FF = 4 * D_MODEL
EPS = 1e-6

LANES = 128
SUBLANES = 8
HEAD_PAD = LANES
LOG2E = math.log2(math.e)
VMEM_LIMIT = 48 * 1024 * 1024

_C_QLAT = 0
_C_KV = _C_QLAT + Q_LORA_RANK
_C_LX = _C_KV + KV_LORA_RANK
_C_LG = _C_LX + LRU_WIDTH
_C_KR = _C_LG + LRU_WIDTH
_KR_ROT_LANE = LANES // 2
IN_EXT = _C_KR + LANES


def _rms(x, g):
    ms = jnp.mean(x * x, axis=-1, keepdims=True)
    return x * lax.rsqrt(ms + EPS) * g


def _dot(a, b):
    return jnp.dot(a, b, preferred_element_type=F32)


def _dot_nt(a, b):
    return lax.dot_general(a, b, (((1,), (1,)), ((), ())), preferred_element_type=F32)


def _const_spec(shape):
    nd = len(shape)
    return pl.BlockSpec(shape, lambda *_: (0,) * nd, pipeline_mode=pl.Buffered(1))


def _params(sem, flags=None):
    return pltpu.CompilerParams(dimension_semantics=sem, vmem_limit_bytes=VMEM_LIMIT, flags=flags)


def _in_proj_kernel(x_ref, g_ref, win_ref, qg_ref, wq_ref, wqr_ref, kvg_ref,
                    cq_ref, sq_ref, cks_ref, *rest, prompt):
    if prompt:
        wuk_ref, wuv_ref, q_ref, rows_ref, lx_ref, lg_ref, k_ref, v_ref = rest
    else:
        wabs_ref, q_ref, rows_ref, lx_ref, lg_ref, qabs_ref = rest
    h = _rms(x_ref[...], g_ref[...]).astype(BF16)
    z = _dot_nt(h, win_ref[...])
    qa = _rms(z[:, _C_QLAT:_C_KV], qg_ref[...]).astype(BF16)
    q1 = _dot(qa, wq_ref[...])
    q2c = _dot(qa, wqr_ref[...])
    per_tile = LANES // QK_ROPE_DIM
    q2 = []
    for hd in range(MLA_HEADS):
        blk = q2c[:, (hd // per_tile) * LANES:(hd // per_tile + 1) * LANES]
        shift = (QK_NOPE_DIM - QK_ROPE_DIM * (hd % per_tile)) % LANES
        q2.append(pltpu.roll(blk, shift, 1) if shift else blk)
    cq = jnp.tile(cq_ref[...], (1, MLA_HEADS))
    sq = jnp.tile(sq_ref[...], (1, MLA_HEADS))
    qb = (q1 * cq + jnp.concatenate(q2, axis=1) * sq).astype(BF16)
    q_ref[...] = qb
    c = _rms(z[:, _C_KV:_C_LX], kvg_ref[...])
    t = z[:, _C_KR:IN_EXT] * cks_ref[...]
    kr = t + pltpu.roll(t, _KR_ROT_LANE, 1)
    rows_ref[:, 0:KV_LORA_RANK] = c
    rows_ref[:, KV_LORA_RANK:KV_CACHE_DIM] = kr[:, 0:QK_ROPE_DIM]
    lx_ref[...] = z[:, _C_LX:_C_LG]
    lg_ref[...] = z[:, _C_LG:_C_KR]
    if prompt:
        cb = c.astype(BF16)
        lane = lax.broadcasted_iota(jnp.int32, kr.shape, 1)
        kr_hi = jnp.where(lane >= QK_NOPE_DIM, kr, 0.0)
        k_ref[...] = (_dot(cb, wuk_ref[...]) + jnp.tile(kr_hi, (1, MLA_HEADS))).astype(BF16)
        v_ref[...] = _dot_nt(wuv_ref[...], cb).astype(BF16)
    else:
        for hd in range(MLA_HEADS):
            qabs_ref[:, hd * KV_LORA_RANK:(hd + 1) * KV_LORA_RANK] = _dot(
                qb[:, hd * HEAD_PAD:(hd + 1) * HEAD_PAD], wabs_ref[hd]).astype(BF16)


def _in_proj(x, g, wts, tabs, *, prompt, tn, pos_blocks):
    n = x.shape[0]
    cq, sq, cks = tabs
    assert QK_NOPE_DIM == _KR_ROT_LANE and QK_NOPE_DIM + QK_ROPE_DIM <= HEAD_PAD
    tab_spec = pl.BlockSpec((tn, LANES), lambda i: (i % pos_blocks, 0))
    row_spec = lambda w: pl.BlockSpec((tn, w), lambda i: (i, 0))
    hp = MLA_HEADS * HEAD_PAD
    in_specs = [row_spec(D_MODEL), _const_spec(g.shape)]
    in_specs += [_const_spec(wts[k].shape) for k in ("win", "qg", "wq", "wqr", "kvg")]
    in_specs += [tab_spec] * 3
    args = [x, g, wts["win"], wts["qg"], wts["wq"], wts["wqr"], wts["kvg"], cq, sq, cks]
    out_shape = [jax.ShapeDtypeStruct((n, hp), BF16), jax.ShapeDtypeStruct((n, KV_CACHE_DIM), F32),
                 jax.ShapeDtypeStruct((n, LRU_WIDTH), F32), jax.ShapeDtypeStruct((n, LRU_WIDTH), F32)]
    out_specs = [row_spec(hp), row_spec(KV_CACHE_DIM), row_spec(LRU_WIDTH), row_spec(LRU_WIDTH)]
    if prompt:
        extra = ("wuk", "wuv_t")
        n_v = MLA_HEADS * V_HEAD_DIM
        out_shape += [jax.ShapeDtypeStruct((n, hp), BF16), jax.ShapeDtypeStruct((n_v, n), BF16)]
        out_specs += [row_spec(hp), pl.BlockSpec((n_v, tn), lambda i: (0, i))]
    else:
        extra = ("wabs",)
        out_shape += [jax.ShapeDtypeStruct((n, MLA_HEADS * KV_LORA_RANK), BF16)]
        out_specs += [row_spec(MLA_HEADS * KV_LORA_RANK)]
    in_specs += [_const_spec(wts[k].shape) for k in extra]
    args += [wts[k] for k in extra]
    return pl.pallas_call(
        functools.partial(_in_proj_kernel, prompt=prompt),
        grid=(n // tn,), in_specs=in_specs, out_specs=out_specs, out_shape=out_shape,
        compiler_params=_params(("parallel",)), name="in_proj_p" if prompt else "in_proj_s",
    )(*args)


HEADS_PER_STEP = 2
SCORE_SLOTS = 2


def _attn_prompt_kernel(q_ref, k_ref, vt_ref, o_ref, s_scr, *, leaf, seq):
    col_max = lambda x: jnp.max(x, axis=0, keepdims=True)
    col_sum = lambda x: jnp.sum(x, axis=0, keepdims=True)
    tri_key = lax.broadcasted_iota(jnp.int32, (leaf, leaf), 0)
    tri_qry = lax.broadcasted_iota(jnp.int32, (leaf, leaf), 1)
    causal = lambda x: jnp.where(tri_key <= tri_qry, x, -jnp.inf)

    def lead(x, width, fill):
        return x if width == 0 else jnp.concatenate([jnp.full((x.shape[0], width), fill, F32), x], axis=1)

    heads = range(HEADS_PER_STEP)
    hcols = [slice(h * HEAD_PAD, (h + 1) * HEAD_PAD) for h in heads]
    vrows = [slice(h * V_HEAD_DIM, (h + 1) * V_HEAD_DIM) for h in heads]
    out = []
    for h in heads:
        slot = h % SCORE_SLOTS
        m = jnp.full((1, seq), -jnp.inf, F32)
        for k0 in range(0, seq, leaf):
            keys = slice(k0, k0 + leaf)
            st = _dot_nt(k_ref[keys, hcols[h]], q_ref[k0:, hcols[h]])
            st = jnp.concatenate([causal(st[:, :leaf]), st[:, leaf:]], axis=1) if k0 + leaf < seq else causal(st)
            s_scr[slot, keys, k0:] = st
            m = jnp.maximum(m, lead(col_max(st), k0, -jnp.inf))
        l = jnp.zeros((1, seq), F32)
        acc = jnp.zeros((V_HEAD_DIM, seq), F32)
        for k0 in range(0, seq, leaf):
            keys = slice(k0, k0 + leaf)
            pt = jnp.exp2(s_scr[slot, keys, k0:] - m[:, k0:])
            l = l + lead(col_sum(pt), k0, 0.0)
            acc = acc + lead(_dot(vt_ref[vrows[h], keys], pt.astype(BF16)), k0, 0.0)
        out.append(acc / l)
    o_ref[...] = jnp.transpose(jnp.concatenate(out, axis=0)).astype(BF16)


def _attn_prompt(q, k, vt, *, batch, seq, leaf):
    n = q.shape[0]
    v_blk = HEADS_PER_STEP * V_HEAD_DIM
    assert seq % leaf == 0 and leaf % LANES == 0 and v_blk % LANES == 0 and MLA_HEADS % HEADS_PER_STEP == 0
    row_blk = pl.BlockSpec((seq, HEADS_PER_STEP * HEAD_PAD), lambda b, h: (b, h))
    return pl.pallas_call(
        functools.partial(_attn_prompt_kernel, leaf=leaf, seq=seq),
        grid=(batch, MLA_HEADS // HEADS_PER_STEP),
        in_specs=[row_blk, row_blk, pl.BlockSpec((v_blk, seq), lambda b, h: (h, b))],
        out_specs=pl.BlockSpec((seq, v_blk), lambda b, h: (b, h)),
        out_shape=jax.ShapeDtypeStruct((n, MLA_HEADS * V_HEAD_DIM), BF16),
        scratch_shapes=[pltpu.VMEM((SCORE_SLOTS, seq, seq), F32)],
        compiler_params=_params(("parallel", "parallel")),
        name="attn_prompt",
    )(q, k, vt)


def _attn_sample_kernel(pt_ref, qa_ref, qr_ref, new_ref, cache_ref, ctx_ref, ring, sem, kt_scr, s_scr,
                        *, pages, n_chunks, slots, pv_chunk):
    b = pl.program_id(0)
    nb = pl.num_programs(0)
    keys = pages * PAGE_SIZE
    group = 4

    def fetch(bb, chunk, slot):
        for i in range(pages):
            pid = pt_ref[bb, chunk * pages + i]
            pltpu.make_async_copy(cache_ref.at[pid], ring.at[slot, i], sem.at[slot]).start()

    def wait(slot):
        for i in range(pages):
            pltpu.make_async_copy(cache_ref.at[0], ring.at[slot, i], sem.at[slot]).wait()

    @pl.when(b == 0)
    def _():
        for c in range(slots):
            fetch(0, c, c)

    qa = qa_ref[0]
    qr = qr_ref[0]
    rows = qa.shape[0]

    def lane_fold(x, op):
        out = x[:, 0:LANES]
        for j in range(1, x.shape[1] // LANES):
            out = op(out, x[:, j * LANES:(j + 1) * LANES])
        return out

    def score_group(g, mrun):
        c0 = g * group
        for k in range(group):
            wait(lax.rem(c0 + k, slots))
        for k in range(group):
            slot = lax.rem(c0 + k, slots)
            kt = jnp.concatenate([ring[slot, i] for i in range(pages)], axis=1).astype(BF16)
            off = pl.multiple_of((c0 + k) * keys, keys)
            kt_scr[:, pl.ds(off, keys)] = kt
            s = _dot(qa, kt[0:KV_LORA_RANK, :]) + _dot(qr, kt[KV_LORA_RANK:KV_CACHE_DIM, :])
            s_scr[:, pl.ds(off, keys)] = s
            mrun = jnp.maximum(mrun, lane_fold(s, jnp.maximum))
        nxt = c0 + slots
        wrap = nxt >= n_chunks
        bb = jnp.where(wrap, b + 1, b)
        bb = jnp.where(bb < nb, bb, 0)
        cc = jnp.where(wrap, nxt - n_chunks, nxt)
        for k in range(group):
            fetch(bb, cc + k, lax.rem(c0 + k, slots))
        return mrun

    mrun = lax.fori_loop(0, n_chunks // group, score_group, jnp.full((rows, LANES), -jnp.inf, F32))

    @pl.when(b == nb - 1)
    def _():
        for c in range(slots):
            wait(c)

    new = new_ref[0]
    n_new = new.shape[0]
    nc = new[:, 0:KV_LORA_RANK].astype(BF16).astype(F32)
    nr = new[:, KV_LORA_RANK:KV_CACHE_DIM].astype(BF16).astype(F32)
    qaf = qa.astype(F32)
    qrf = qr.astype(F32)
    tok = lax.shift_right_logical(lax.broadcasted_iota(jnp.int32, (rows, 1), 0), int(math.log2(MLA_HEADS)))
    s_new = []
    for t in range(n_new):
        st = (jnp.sum(qaf * nc[t:t + 1, :], axis=-1, keepdims=True)
              + jnp.sum(qrf * nr[t:t + 1, :], axis=-1, keepdims=True))
        s_new.append(jnp.where(tok >= t, st, -jnp.inf))

    m = jnp.max(mrun, axis=-1, keepdims=True)
    for st in s_new:
        m = jnp.maximum(m, st)

    def pv_step(j, carry):
        acc, lrun = carry
        off = pl.multiple_of(j * pv_chunk, pv_chunk)
        p = jnp.exp2(s_scr[:, pl.ds(off, pv_chunk)] - m)
        acc = acc + _dot_nt(p.astype(BF16), kt_scr[0:KV_LORA_RANK, pl.ds(off, pv_chunk)])
        return acc, lrun + lane_fold(p, jnp.add)

    acc, lrun = lax.fori_loop(0, (n_chunks * keys) // pv_chunk, pv_step,
                              (jnp.zeros((rows, KV_LORA_RANK), F32), jnp.zeros((rows, LANES), F32)), unroll=True)
    l = jnp.sum(lrun, axis=-1, keepdims=True)
    for t, st in enumerate(s_new):
        pt = jnp.exp2(st - m)
        l = l + pt
        acc = acc + pt.astype(BF16).astype(F32) * nc[t:t + 1, :]
    ctx_ref[0] = (acc / l).astype(BF16)


def _attn_sample(page_table, qabs, qrope, new_rows, cache_t, *, pages, slots, pv_chunk):
    nb, rows, _ = qabs.shape
    n_pages = page_table.shape[1]
    n_chunks = n_pages // pages
    n_keys = n_pages * PAGE_SIZE
    assert n_chunks * pages == n_pages and n_chunks % slots == 0 and slots % 4 == 0 and n_keys % pv_chunk == 0
    t_new = new_rows.shape[1]
    grid_spec = pltpu.PrefetchScalarGridSpec(
        num_scalar_prefetch=1, grid=(nb,),
        in_specs=[pl.BlockSpec((1, rows, KV_LORA_RANK), lambda b, pt: (b, 0, 0)),
                  pl.BlockSpec((1, rows, QK_ROPE_DIM), lambda b, pt: (b, 0, 0)),
                  pl.BlockSpec((1, t_new, KV_CACHE_DIM), lambda b, pt: (b, 0, 0)),
                  pl.BlockSpec(memory_space=pl.ANY)],
        out_specs=pl.BlockSpec((1, rows, KV_LORA_RANK), lambda b, pt: (b, 0, 0)),
        scratch_shapes=[pltpu.VMEM((slots, pages, KV_CACHE_DIM, PAGE_SIZE), F32),
                        pltpu.SemaphoreType.DMA((slots,)),
                        pltpu.VMEM((KV_CACHE_DIM, n_keys), BF16),
                        pltpu.VMEM((rows, n_keys), F32)])
    return pl.pallas_call(
        functools.partial(_attn_sample_kernel, pages=pages, n_chunks=n_chunks, slots=slots, pv_chunk=pv_chunk),
        grid_spec=grid_spec,
        out_shape=jax.ShapeDtypeStruct((nb, rows, KV_LORA_RANK), BF16),
        compiler_params=_params(("arbitrary",)), name="attn_sample",
    )(page_table, qabs, qrope, new_rows, cache_t)


def _lru_gates(y, wa_ref, ba_ref, wx_ref, bx_ref, lam_ref):
    yb = y.astype(BF16)
    r = jax.nn.sigmoid(_dot(yb, wa_ref[...]) + ba_ref[...])
    i = jax.nn.sigmoid(_dot(yb, wx_ref[...]) + bx_ref[...])
    nl = -lam_ref[...]
    softplus = jnp.maximum(nl, 0.0) + jnp.log1p(jnp.exp(-jnp.abs(nl)))
    log_a = -LRU_C * r * softplus
    a = jnp.exp(log_a)
    th = jnp.tanh(log_a)
    u = (y * i) * jnp.sqrt(-2.0 * th / (1.0 - th))
    return a, u


def _lru_prompt_kernel(lx_ref, lg_ref, cw_ref, cb_ref, wa_ref, ba_ref, wx_ref, bx_ref, lam_ref,
                       rec_ref, ht_ref, xp_scr, a_scr, u_scr, hs_scr, h_scr, *, tt, batch, pitch):
    ti = pl.program_id(0)
    sub = SUBLANES

    @pl.when(ti == 0)
    def _():
        xp_scr[:, 0:sub, :] = jnp.zeros((batch, sub, LRU_WIDTH), F32)
        h_scr[...] = jnp.zeros((batch, LRU_WIDTH), F32)

    cw = cw_ref[...]
    ys = []
    for b in range(batch):
        x = lx_ref[b]
        xp_scr[b, sub:sub + tt, :] = x
        y = cb_ref[...] + cw[3:4, :] * x
        for j in range(CONV_WIDTH - 1):
            y = y + cw[j:j + 1, :] * xp_scr[b, sub - 3 + j:sub - 3 + j + tt, :]
        ys.append(y)
    a, u = _lru_gates(jnp.concatenate(ys, axis=0), wa_ref, ba_ref, wx_ref, bx_ref, lam_ref)
    n_col = LRU_WIDTH // LANES
    cols = [slice(c * LANES, (c + 1) * LANES) for c in range(n_col)]
    for c in range(n_col):
        for b in range(batch):
            a_scr[c, b * pitch:b * pitch + tt, :] = a[b * tt:(b + 1) * tt, cols[c]]
            u_scr[c, b * pitch:b * pitch + tt, :] = u[b * tt:(b + 1) * tt, cols[c]]

    def step(t, h):
        rows = pl.ds(t, batch, stride=pitch)
        h = tuple(a_scr[c, rows, :] * h[c] + u_scr[c, rows, :] for c in range(n_col))
        for c in range(n_col):
            hs_scr[c, rows, :] = h[c]
        return h

    h0 = h_scr[...]
    h_last = lax.fori_loop(0, tt, step, tuple(h0[:, cols[c]] for c in range(n_col)), unroll=8)
    h_last = jnp.concatenate(h_last, axis=1)
    h_scr[...] = h_last
    ht_ref[...] = h_last
    for b in range(batch):
        hs = jnp.concatenate([hs_scr[c, b * pitch:b * pitch + tt, :] for c in range(n_col)], axis=1)
        rec_ref[b] = (hs * jax.nn.gelu(lg_ref[b])).astype(BF16)
        xp_scr[b, 0:sub, :] = xp_scr[b, tt:tt + sub, :]


def _lru_prompt(lx, lg, wts, *, batch, seq, tt):
    n = lx.shape[0]
    pitch = tt + SUBLANES
    blk = pl.BlockSpec((batch, tt, LRU_WIDTH), lambda t: (0, t, 0))
    names = ("conv_w", "conv_b", "wa", "ba", "wx", "bx", "lam")
    to3 = lambda a: a.reshape(batch, seq, LRU_WIDTH)
    rec, ht = pl.pallas_call(
        functools.partial(_lru_prompt_kernel, tt=tt, batch=batch, pitch=pitch),
        grid=(seq // tt,),
        in_specs=[blk, blk] + [_const_spec(wts[k].shape) for k in names],
        out_specs=[blk, pl.BlockSpec((batch, LRU_WIDTH), lambda t: (0, 0))],
        out_shape=[jax.ShapeDtypeStruct((batch, seq, LRU_WIDTH), BF16), jax.ShapeDtypeStruct((batch, LRU_WIDTH), F32)],
        scratch_shapes=[pltpu.VMEM((batch, tt + SUBLANES, LRU_WIDTH), F32)]
        + [pltpu.VMEM((LRU_WIDTH // LANES, batch * pitch, LANES), F32)] * 3 + [pltpu.VMEM((batch, LRU_WIDTH), F32)],
        compiler_params=_params(("arbitrary",)), name="lru_prompt",
    )(to3(lx), to3(lg), *[wts[k] for k in names])
    return rec.reshape(n, LRU_WIDTH), ht


def _lru_sample_kernel(lx_ref, lg_ref, conv_ref, h0_ref, cw_ref, cb_ref, wa_ref, ba_ref, wx_ref, bx_ref, lam_ref,
                       rec_ref, ht_ref):
    steps = lx_ref.shape[0]
    xs = [conv_ref[j] for j in range(CONV_WIDTH - 1)] + [lx_ref[t] for t in range(steps)]
    cw = cw_ref[...]
    h = h0_ref[...]
    for t in range(steps):
        y = cb_ref[...]
        for j in range(CONV_WIDTH):
            y = y + cw[j:j + 1, :] * xs[t + j]
        a, u = _lru_gates(y, wa_ref, ba_ref, wx_ref, bx_ref, lam_ref)
        h = a * h + u
        rec_ref[t] = (h * jax.nn.gelu(lg_ref[t])).astype(BF16)
    ht_ref[...] = h


def _lru_sample(lx_t, lg_t, conv_t, h0, wts):
    steps, nb, _ = lx_t.shape
    names = ("conv_w", "conv_b", "wa", "ba", "wx", "bx", "lam")
    return pl.pallas_call(
        _lru_sample_kernel,
        out_shape=[jax.ShapeDtypeStruct((steps, nb, LRU_WIDTH), BF16), jax.ShapeDtypeStruct((nb, LRU_WIDTH), F32)],
        compiler_params=pltpu.CompilerParams(vmem_limit_bytes=VMEM_LIMIT), name="lru_sample",
    )(lx_t, lg_t, conv_t, h0, *[wts[k] for k in names])


def _mem_kv_kernel(m_ref, g_ref, wk_ref, wv_ref, k4_ref, v4_ref, kb_ref, vb_ref):
    m = _rms(m_ref[0], g_ref[...]).astype(BF16)
    for w_ref, o4_ref, ob_ref in ((wk_ref, k4_ref, kb_ref), (wv_ref, v4_ref, vb_ref)):
        y = _dot(m, w_ref[...])
        ob_ref[0] = y.astype(BF16)
        for h in range(MEM_HEADS):
            o4_ref[0, :, h, :] = y[:, h * MEM_HEAD_DIM:(h + 1) * MEM_HEAD_DIM]


def _mem_kv(mem, g, wk, wv):
    nb, n_mem, _ = mem.shape
    blk3 = pl.BlockSpec((1, n_mem, D_MODEL), lambda i: (i, 0, 0))
    blk4 = pl.BlockSpec((1, n_mem, MEM_HEADS, MEM_HEAD_DIM), lambda i: (i, 0, 0, 0))
    shp4 = jax.ShapeDtypeStruct((nb, n_mem, MEM_HEADS, MEM_HEAD_DIM), F32)
    shp3 = jax.ShapeDtypeStruct((nb, n_mem, D_MODEL), BF16)
    return pl.pallas_call(
        _mem_kv_kernel, grid=(nb,),
        in_specs=[blk3, _const_spec(g.shape), _const_spec(wk.shape), _const_spec(wv.shape)],
        out_specs=[blk4, blk4, blk3, blk3], out_shape=[shp4, shp4, shp3, shp3],
        compiler_params=_params(("parallel",)), name="mem_kv",
    )(mem, g, wk, wv)


def _mix_out_kernel(x_ref, attn_ref, rec_ref, *rest, absorbed):
    if absorbed:
        wuv_ref, woa_ref, wor_ref, g_ref, wq_ref, x1_ref, qm_ref = rest
        attn = _dot(attn_ref[...], wuv_ref[...]).astype(BF16)
    else:
        woa_ref, wor_ref, g_ref, wq_ref, x1_ref, qm_ref = rest
        attn = attn_ref[...]
    x1 = x_ref[...] + _dot(attn, woa_ref[...]) + _dot(rec_ref[...], wor_ref[...])
    x1_ref[...] = x1
    qm_ref[...] = _dot(_rms(x1, g_ref[...]).astype(BF16), wq_ref[...]).astype(BF16)


def _mix_out(x, attn, rec, wts, *, absorbed, tn):
    n = x.shape[0]
    row_spec = lambda w: pl.BlockSpec((tn, w), lambda i: (i, 0))
    names = (("wuv_bd",) if absorbed else ()) + ("woa", "wor", "g_mem", "w_mem_q")
    return pl.pallas_call(
        functools.partial(_mix_out_kernel, absorbed=absorbed), grid=(n // tn,),
        in_specs=[row_spec(D_MODEL), row_spec(attn.shape[1]), row_spec(LRU_WIDTH)]
        + [_const_spec(wts[k].shape) for k in names],
        out_specs=[row_spec(D_MODEL), row_spec(D_MODEL)],
        out_shape=[jax.ShapeDtypeStruct((n, D_MODEL), F32), jax.ShapeDtypeStruct((n, D_MODEL), BF16)],
        compiler_params=_params(("parallel",)), name="mix_out_s" if absorbed else "mix_out_p",
    )(x, attn, rec, *[wts[k] for k in names])


def _mem_attend(q, k_ref, v_ref):
    scale = MEM_HEAD_DIM ** -0.5
    out = []
    for h in range(MEM_HEADS):
        lo, hi = h * MEM_HEAD_DIM, (h + 1) * MEM_HEAD_DIM
        s = _dot_nt(q[:, lo:hi], k_ref[0, :, lo:hi]) * scale
        e = jnp.exp(s - jnp.max(s, axis=-1, keepdims=True))
        p = e / jnp.sum(e, axis=-1, keepdims=True)
        out.append(_dot(p.astype(BF16), v_ref[0, :, lo:hi]).astype(BF16))
    return jnp.concatenate(out, axis=1)


def _mem_attn_cache_kernel(q_ref, k_ref, v_ref, o_ref, *, steps):
    n_mem = k_ref.shape[1]
    for g in range(q_ref.shape[0]):
        q = q_ref[g]
        k2 = k_ref[g].reshape(n_mem * MEM_HEADS, MEM_HEAD_DIM).astype(BF16)
        v2 = v_ref[g].reshape(n_mem * MEM_HEADS, MEM_HEAD_DIM).astype(BF16)
        s = _dot_nt(q, k2) * (MEM_HEAD_DIM ** -0.5)
        row_head = lax.shift_right_logical(lax.broadcasted_iota(jnp.int32, s.shape, 0), int(math.log2(steps)))
        col_head = lax.bitwise_and(lax.broadcasted_iota(jnp.int32, s.shape, 1), MEM_HEADS - 1)
        s = jnp.where(row_head == col_head, s, -jnp.inf)
        e = jnp.exp(s - jnp.max(s, axis=-1, keepdims=True))
        p = e / jnp.sum(e, axis=-1, keepdims=True)
        o_ref[g] = _dot(p.astype(BF16), v2).astype(BF16)


def _mem_attn_cache(q, k, v, *, steps, per_step=8):
    nb, rows, _ = q.shape
    qblk = pl.BlockSpec((per_step, rows, MEM_HEAD_DIM), lambda b: (b, 0, 0))
    kblk = pl.BlockSpec((per_step,) + k.shape[1:], lambda b: (b, 0, 0, 0))
    return pl.pallas_call(
        functools.partial(_mem_attn_cache_kernel, steps=steps), grid=(nb // per_step,),
        in_specs=[qblk, kblk, kblk], out_specs=qblk,
        out_shape=jax.ShapeDtypeStruct(q.shape, BF16),
        compiler_params=_params(("parallel",)), name="mem_attn_s",
    )(q, k, v)


def _tail_kernel(x1_ref, *rest, ff_chunk, attend):
    if attend:
        (qm_ref, k_ref, v_ref, qm_nxt_ref, k_nxt_ref, v_nxt_ref,
         wo_ref, g_ref, wup_ref, wdn_ref, gf_ref, y_ref, o_scr) = rest

        @pl.when(pl.program_id(0) == 0)
        def _():
            o_scr[...] = _mem_attend(qm_ref[...], k_ref, v_ref)

        o = o_scr[...]
    else:
        o_ref, wo_ref, g_ref, wup_ref, wdn_ref, gf_ref, y_ref = rest
        o = o_ref[...]
    x2 = x1_ref[...] + _dot(o, wo_ref[...])
    h = _rms(x2, g_ref[...]).astype(BF16)
    acc = x2
    for c in range(D_FF // ff_chunk):
        lo, hi = c * ff_chunk, (c + 1) * ff_chunk
        up = jnp.maximum(_dot(h, wup_ref[:, lo:hi]), 0.0)
        acc = acc + _dot((up * up).astype(BF16), wdn_ref[lo:hi, :])
    y_ref[...] = _rms(acc, gf_ref[...])
    if attend:
        o_scr[...] = _mem_attend(qm_nxt_ref[...], k_nxt_ref, v_nxt_ref)


def _tail(x1, o, wts, *, tn, name, mem_kv=None, seq=None):
    n = x1.shape[0]
    steps = n // tn
    blk = pl.BlockSpec((tn, D_MODEL), lambda i: (i, 0))
    names = ("w_mem_o", "g_mlp", "w_up", "w_down", "g_final")
    args, specs, scratch = [x1, o], [blk, blk], []
    if mem_kv is not None:
        per_seq = seq // tn
        nxt = lambda i: jnp.minimum(i + 1, steps - 1)
        kshape = (1,) + mem_kv[0].shape[1:]
        kblk = pl.BlockSpec(kshape, lambda i: (i // per_seq, 0, 0))
        kblk_nxt = pl.BlockSpec(kshape, lambda i: (nxt(i) // per_seq, 0, 0))
        args += [*mem_kv, o, *mem_kv]
        specs += [kblk, kblk, pl.BlockSpec((tn, D_MODEL), lambda i: (nxt(i), 0)), kblk_nxt, kblk_nxt]
        scratch = [pltpu.VMEM((tn, D_MODEL), BF16)]
    return pl.pallas_call(
        functools.partial(_tail_kernel, ff_chunk=1024, attend=mem_kv is not None), grid=(steps,),
        in_specs=specs + [_const_spec(wts[k].shape) for k in names],
        out_specs=blk, out_shape=jax.ShapeDtypeStruct((n, D_MODEL), F32), scratch_shapes=scratch,
        compiler_params=_params(("arbitrary",) if mem_kv is not None else ("parallel",)), name=name,
    )(*args, *[wts[k] for k in names])


def _rot_half_cols(w):
    half = QK_ROPE_DIM // 2
    return jnp.concatenate([-w[..., half:], w[..., :half]], axis=-1)


def _prep_weights(w_in, q_norm_g, w_q_up, kv_norm_g, w_uk, w_uv, conv_w, conv_b, lru_w_a, lru_b_a, lru_w_x, lru_b_x,
                  lru_lambda, w_out, norm_mem_g, mem_norm_g, w_mem_q, w_mem_k, w_mem_v, w_mem_o, norm_mlp_g,
                  w_up, w_down, final_norm_g):
    w = {}
    s1 = Q_LORA_RANK
    s2 = s1 + KV_LORA_RANK
    s3 = s2 + QK_ROPE_DIM
    s4 = s3 + LRU_WIDTH
    wt = w_in.T
    wkr = wt[s2:s3]
    half = QK_ROPE_DIM // 2
    wkr_rot = jnp.concatenate([-wkr[half:], wkr[:half]], axis=0)
    lane_pad = jnp.zeros((_KR_ROT_LANE - QK_ROPE_DIM, D_MODEL), F32)
    w["win"] = jnp.concatenate([wt[:s1], wt[s1:s2], wt[s3:s4], wt[s4:], wkr, lane_pad, wkr_rot, lane_pad],
                               axis=0).astype(BF16)
    w["qg"] = q_norm_g.reshape(1, -1)
    w["kvg"] = kv_norm_g.reshape(1, -1)
    wq3 = w_q_up.reshape(Q_LORA_RANK, MLA_HEADS, QK_NOPE_DIM + QK_ROPE_DIM)
    nope, ropew = wq3[..., :QK_NOPE_DIM], wq3[..., QK_NOPE_DIM:]
    tail_pad = jnp.zeros((Q_LORA_RANK, MLA_HEADS, HEAD_PAD - QK_NOPE_DIM - QK_ROPE_DIM), F32)
    w["wq"] = jnp.concatenate([nope, ropew, tail_pad], -1).reshape(Q_LORA_RANK, -1).astype(BF16)
    w["wqr"] = _rot_half_cols(ropew).reshape(Q_LORA_RANK, -1).astype(BF16)
    head_pad = jnp.zeros((KV_LORA_RANK, MLA_HEADS, HEAD_PAD - QK_NOPE_DIM), F32)
    w["wuk"] = jnp.concatenate([w_uk, head_pad], -1).reshape(KV_LORA_RANK, -1).astype(BF16)
    w["wuv_t"] = w_uv.reshape(KV_LORA_RANK, -1).T.astype(BF16)
    eye = jnp.eye(MLA_HEADS, dtype=F32)
    uk_t = jnp.transpose(w_uk, (1, 2, 0))
    uk_t = jnp.concatenate([uk_t, jnp.zeros((MLA_HEADS, HEAD_PAD - QK_NOPE_DIM, KV_LORA_RANK), F32)], 1)
    w["wabs"] = uk_t.astype(BF16)
    w["wuv_bd"] = jnp.einsum("chd,hg->hcgd", w_uv, eye).reshape(MLA_HEADS * KV_LORA_RANK, -1).astype(BF16)
    n_attn = MLA_HEADS * V_HEAD_DIM
    w["woa"] = w_out[:n_attn].astype(BF16)
    w["wor"] = w_out[n_attn:].astype(BF16)
    w["conv_w"] = conv_w
    w["conv_b"] = conv_b.reshape(1, -1)
    eye_l = jnp.eye(LRU_BLOCKS, dtype=F32)
    w["wa"] = jnp.einsum("nde,nm->ndme", lru_w_a, eye_l).reshape(LRU_WIDTH, LRU_WIDTH).astype(BF16)
    w["wx"] = jnp.einsum("nde,nm->ndme", lru_w_x, eye_l).reshape(LRU_WIDTH, LRU_WIDTH).astype(BF16)
    w["ba"] = lru_b_a.reshape(1, -1)
    w["bx"] = lru_b_x.reshape(1, -1)
    w["lam"] = lru_lambda.reshape(1, -1)
    w["g_mem"] = norm_mem_g.reshape(1, -1)
    w["g_memkv"] = mem_norm_g.reshape(1, -1)
    w["w_mem_q"] = w_mem_q.astype(BF16)
    w["w_mem_k"] = w_mem_k.astype(BF16)
    w["w_mem_v"] = w_mem_v.astype(BF16)
    w["w_mem_o"] = w_mem_o.astype(BF16)
    w["g_mlp"] = norm_mlp_g.reshape(1, -1)
    w["w_up"] = w_up.astype(BF16)
    w["w_down"] = w_down.astype(BF16)
    w["g_final"] = final_norm_g.reshape(1, -1)
    return w


def _rope_tables(pos):
    pos = np.asarray(pos, np.float64)
    inv = ROPE_THETA ** (-np.arange(0, QK_ROPE_DIM, 2, dtype=np.float64) / QK_ROPE_DIM)
    ang = pos[:, None] * inv[None, :]
    cos, sin = np.cos(ang), np.sin(ang)
    n = pos.shape[0]
    z = lambda k: np.zeros((n, k))
    q_tail = HEAD_PAD - QK_NOPE_DIM - QK_ROPE_DIM
    c_exp = MLA_SCALE * LOG2E
    cq = np.concatenate([np.ones((n, QK_NOPE_DIM)), cos, cos, z(q_tail)], 1) * c_exp
    sq = np.concatenate([z(QK_NOPE_DIM), sin, sin, z(q_tail)], 1) * c_exp
    k_gap = z(_KR_ROT_LANE - QK_ROPE_DIM)
    cks = np.concatenate([cos, cos, k_gap, sin, sin, k_gap], 1)
    return tuple(jnp.asarray(t, F32) for t in (cq, sq, cks))


def kernel(x_prompt, x_sample, cache_mla, cache_mem_k, cache_mem_v, state_lru_h, state_conv, page_table, mem_prompt, norm_mix_g, w_in, q_norm_g, w_q_up, kv_norm_g, w_uk, w_uv, conv_w, conv_b, lru_w_a, lru_b_a, lru_w_x, lru_b_x, lru_lambda, w_out, norm_mem_g, mem_norm_g, w_mem_q, w_mem_k, w_mem_v, w_mem_o, norm_mlp_g, w_up, w_down, final_norm_g):
    B, S, _ = x_prompt.shape
    Bd, T, _ = x_sample.shape
    assert w_in.shape[0] == 1, "single layer"
    wts = _prep_weights(w_in[0], q_norm_g[0], w_q_up[0], kv_norm_g[0], w_uk[0], w_uv[0], conv_w[0], conv_b[0],
                        lru_w_a[0], lru_b_a[0], lru_w_x[0], lru_b_x[0], lru_lambda[0], w_out[0], norm_mem_g[0],
                        mem_norm_g[0], w_mem_q[0], w_mem_k[0], w_mem_v[0], w_mem_o[0], norm_mlp_g[0],
                        w_up[0], w_down[0], final_norm_g)
    g_mix = norm_mix_g[0].reshape(1, -1)
    tn = 512

    xp = x_prompt.reshape(B * S, D_MODEL)
    q_p, rows_p, lx_p, lg_p, k_p, v_p = _in_proj(xp, g_mix, wts, _rope_tables(np.arange(S)),
                                                 prompt=True, tn=2 * tn, pos_blocks=S // (2 * tn))
    attn_p = _attn_prompt(q_p, k_p, v_p, batch=B, seq=S, leaf=256)
    rec_p, ht_p = _lru_prompt(lx_p, lg_p, wts, batch=B, seq=S, tt=128)
    x1_p, qm_p = _mix_out(xp, attn_p, rec_p, wts, absorbed=False, tn=2 * tn)
    n_mem = mem_prompt.shape[1]
    mk_p, mv_p, mkb_p, mvb_p = _mem_kv(mem_prompt, wts["g_memkv"], wts["w_mem_k"], wts["w_mem_v"])
    y_p = _tail(x1_p, qm_p, wts, tn=tn, name="tail_p", mem_kv=(mkb_p, mvb_p), seq=S)

    ns = Bd * T
    xs = x_sample.reshape(ns, D_MODEL)
    tabs_s = _rope_tables(np.tile(PAST_LEN + np.arange(T), Bd))
    q_s, rows_s, lx_s, lg_s, qabs_s = _in_proj(xs, g_mix, wts, tabs_s, prompt=False, tn=ns, pos_blocks=1)
    rows_q = T * MLA_HEADS
    qr_s = q_s.reshape(ns * MLA_HEADS, HEAD_PAD)[:, QK_NOPE_DIM:QK_NOPE_DIM + QK_ROPE_DIM]
    ctx_s = _attn_sample(page_table, qabs_s.reshape(Bd, rows_q, KV_LORA_RANK), qr_s.reshape(Bd, rows_q, QK_ROPE_DIM),
                         rows_s.reshape(Bd, T, KV_CACHE_DIM), jnp.swapaxes(cache_mla[0], 1, 2),
                         pages=8, slots=16, pv_chunk=2048)
    to_time_major = lambda a: jnp.transpose(a.reshape(Bd, -1, LRU_WIDTH), (1, 0, 2))
    rec_t, ht_s = _lru_sample(to_time_major(lx_s), to_time_major(lg_s), to_time_major(state_conv[0]),
                              state_lru_h[0], wts)
    rec_s = jnp.transpose(rec_t, (1, 0, 2)).reshape(ns, LRU_WIDTH)
    x1_s, qm_s = _mix_out(xs, ctx_s.reshape(ns, MLA_HEADS * KV_LORA_RANK), rec_s, wts, absorbed=True, tn=ns)
    head_major = lambda a: jnp.transpose(a.reshape(Bd, T, MEM_HEADS, MEM_HEAD_DIM), (0, 2, 1, 3))
    o_s = _mem_attn_cache(head_major(qm_s).reshape(Bd, MEM_HEADS * T, MEM_HEAD_DIM), cache_mem_k[0], cache_mem_v[0],
                          steps=T)
    o_s = jnp.transpose(o_s.reshape(Bd, MEM_HEADS, T, MEM_HEAD_DIM), (0, 2, 1, 3))
    y_s = _tail(x1_s, o_s.reshape(ns, D_MODEL), wts, tn=ns, name="tail_s")

    lx_p3 = lx_p.reshape(B, S, LRU_WIDTH)
    lx_s3 = lx_s.reshape(Bd, T, LRU_WIDTH)
    keep = CONV_WIDTH - 1
    return (y_p.reshape(B, S, D_MODEL), y_s.reshape(Bd, T, D_MODEL),
            rows_p.reshape(1, B, S, KV_CACHE_DIM), rows_s.reshape(1, Bd, T, KV_CACHE_DIM),
            mk_p.reshape(1, B, n_mem, MEM_HEADS, MEM_HEAD_DIM), mv_p.reshape(1, B, n_mem, MEM_HEADS, MEM_HEAD_DIM),
            ht_p.reshape(1, B, LRU_WIDTH), ht_s.reshape(1, Bd, LRU_WIDTH),
            lx_p3[:, S - keep:].reshape(1, B, keep, LRU_WIDTH), lx_s3[:, T - keep:].reshape(1, Bd, keep, LRU_WIDTH))
```

```python
import functools
import math

import jax
import jax.numpy as jnp
import numpy as np
from jax import lax
from jax.experimental import pallas as pl
from jax.experimental.pallas import tpu as pltpu

F32 = jnp.float32
BF16 = jnp.bfloat16

D_MODEL = 1024
PAST_LEN = 16384
PAGE_SIZE = 128
MLA_HEADS = 8
QK_NOPE_DIM = 64
QK_ROPE_DIM = 32
V_HEAD_DIM = 64
Q_LORA_RANK = 384
KV_LORA_RANK = 256
KV_CACHE_DIM = KV_LORA_RANK + QK_ROPE_DIM
ROPE_THETA = 10000.0
MLA_SCALE = (QK_NOPE_DIM + QK_ROPE_DIM) ** -0.5
LRU_WIDTH = D_MODEL // 2
LRU_BLOCKS = 8
LRU_C = 8.0
CONV_WIDTH = 4
MEM_HEADS = 4
MEM_HEAD_DIM = D_MODEL // MEM_HEADS
D_FF = 4 * D_MODEL
EPS = 1e-6

LANES = 128
SUBLANES = 8
HEAD_PAD = LANES
LOG2E = math.log2(math.e)
VMEM_LIMIT = 48 * 1024 * 1024

_C_QLAT = 0
_C_KV = _C_QLAT + Q_LORA_RANK
_C_LX = _C_KV + KV_LORA_RANK
_C_LG = _C_LX + LRU_WIDTH
_C_KR = _C_LG + LRU_WIDTH
_KR_ROT_LANE = LANES // 2
IN_EXT = _C_KR + LANES


def _rms(x, g):
    ms = jnp.mean(x * x, axis=-1, keepdims=True)
    return x * lax.rsqrt(ms + EPS) * g


def _dot(a, b):
    return jnp.dot(a, b, preferred_element_type=F32)


def _dot_nt(a, b):
    return lax.dot_general(a, b, (((1,), (1,)), ((), ())), preferred_element_type=F32)


def _const_spec(shape):
    nd = len(shape)
    return pl.BlockSpec(shape, lambda *_: (0,) * nd, pipeline_mode=pl.Buffered(1))


def _params(sem, flags=None):
    return pltpu.CompilerParams(dimension_semantics=sem, vmem_limit_bytes=VMEM_LIMIT, flags=flags)


def _in_proj_kernel(x_ref, g_ref, win_ref, qg_ref, wq_ref, wqr_ref, kvg_ref,
                    cq_ref, sq_ref, cks_ref, *rest, prompt):
    if prompt:
        wuk_ref, wuv_ref, q_ref, rows_ref, lx_ref, lg_ref, k_ref, v_ref = rest
    else:
        wabs_ref, q_ref, rows_ref, lx_ref, lg_ref, qabs_ref = rest
    h = _rms(x_ref[...], g_ref[...]).astype(BF16)
    z = _dot_nt(h, win_ref[...])
    qa = _rms(z[:, _C_QLAT:_C_KV], qg_ref[...]).astype(BF16)
    q1 = _dot(qa, wq_ref[...])
    q2c = _dot(qa, wqr_ref[...])
    per_tile = LANES // QK_ROPE_DIM
    q2 = []
    for hd in range(MLA_HEADS):
        blk = q2c[:, (hd // per_tile) * LANES:(hd // per_tile + 1) * LANES]
        shift = (QK_NOPE_DIM - QK_ROPE_DIM * (hd % per_tile)) % LANES
        q2.append(pltpu.roll(blk, shift, 1) if shift else blk)
    cq = jnp.tile(cq_ref[...], (1, MLA_HEADS))
    sq = jnp.tile(sq_ref[...], (1, MLA_HEADS))
    qb = (q1 * cq + jnp.concatenate(q2, axis=1) * sq).astype(BF16)
    q_ref[...] = qb
    c = _rms(z[:, _C_KV:_C_LX], kvg_ref[...])
    t = z[:, _C_KR:IN_EXT] * cks_ref[...]
    kr = t + pltpu.roll(t, _KR_ROT_LANE, 1)
    rows_ref[:, 0:KV_LORA_RANK] = c
    rows_ref[:, KV_LORA_RANK:KV_CACHE_DIM] = kr[:, 0:QK_ROPE_DIM]
    lx_ref[...] = z[:, _C_LX:_C_LG]
    lg_ref[...] = z[:, _C_LG:_C_KR]
    if prompt:
        cb = c.astype(BF16)
        lane = lax.broadcasted_iota(jnp.int32, kr.shape, 1)
        kr_hi = jnp.where(lane >= QK_NOPE_DIM, kr, 0.0)
        k_ref[...] = (_dot(cb, wuk_ref[...]) + jnp.tile(kr_hi, (1, MLA_HEADS))).astype(BF16)
        v_ref[...] = _dot_nt(wuv_ref[...], cb).astype(BF16)
    else:
        for hd in range(MLA_HEADS):
            qabs_ref[:, hd * KV_LORA_RANK:(hd + 1) * KV_LORA_RANK] = _dot(
                qb[:, hd * HEAD_PAD:(hd + 1) * HEAD_PAD], wabs_ref[hd]).astype(BF16)


def _in_proj(x, g, wts, tabs, *, prompt, tn, pos_blocks):
    n = x.shape[0]
    cq, sq, cks = tabs
    assert QK_NOPE_DIM == _KR_ROT_LANE and QK_NOPE_DIM + QK_ROPE_DIM <= HEAD_PAD
    tab_spec = pl.BlockSpec((tn, LANES), lambda i: (i % pos_blocks, 0))
    row_spec = lambda w: pl.BlockSpec((tn, w), lambda i: (i, 0))
    hp = MLA_HEADS * HEAD_PAD
    in_specs = [row_spec(D_MODEL), _const_spec(g.shape)]
    in_specs += [_const_spec(wts[k].shape) for k in ("win", "qg", "wq", "wqr", "kvg")]
    in_specs += [tab_spec] * 3
    args = [x, g, wts["win"], wts["qg"], wts["wq"], wts["wqr"], wts["kvg"], cq, sq, cks]
    out_shape = [jax.ShapeDtypeStruct((n, hp), BF16), jax.ShapeDtypeStruct((n, KV_CACHE_DIM), F32),
                 jax.ShapeDtypeStruct((n, LRU_WIDTH), F32), jax.ShapeDtypeStruct((n, LRU_WIDTH), F32)]
    out_specs = [row_spec(hp), row_spec(KV_CACHE_DIM), row_spec(LRU_WIDTH), row_spec(LRU_WIDTH)]
    if prompt:
        extra = ("wuk", "wuv_t")
        n_v = MLA_HEADS * V_HEAD_DIM
        out_shape += [jax.ShapeDtypeStruct((n, hp), BF16), jax.ShapeDtypeStruct((n_v, n), BF16)]
        out_specs += [row_spec(hp), pl.BlockSpec((n_v, tn), lambda i: (0, i))]
    else:
        extra = ("wabs",)
        out_shape += [jax.ShapeDtypeStruct((n, MLA_HEADS * KV_LORA_RANK), BF16)]
        out_specs += [row_spec(MLA_HEADS * KV_LORA_RANK)]
    in_specs += [_const_spec(wts[k].shape) for k in extra]
    args += [wts[k] for k in extra]
    return pl.pallas_call(
        functools.partial(_in_proj_kernel, prompt=prompt),
        grid=(n // tn,), in_specs=in_specs, out_specs=out_specs, out_shape=out_shape,
        compiler_params=_params(("parallel",)), name="in_proj_p" if prompt else "in_proj_s",
    )(*args)


HEADS_PER_STEP = 2
SCORE_SLOTS = 2


def _attn_prompt_kernel(q_ref, k_ref, vt_ref, o_ref, s_scr, *, leaf, seq):
    col_max = lambda x: jnp.max(x, axis=0, keepdims=True)
    col_sum = lambda x: jnp.sum(x, axis=0, keepdims=True)
    tri_key = lax.broadcasted_iota(jnp.int32, (leaf, leaf), 0)
    tri_qry = lax.broadcasted_iota(jnp.int32, (leaf, leaf), 1)
    causal = lambda x: jnp.where(tri_key <= tri_qry, x, -jnp.inf)

    def lead(x, width, fill):
        return x if width == 0 else jnp.concatenate([jnp.full((x.shape[0], width), fill, F32), x], axis=1)

    heads = range(HEADS_PER_STEP)
    hcols = [slice(h * HEAD_PAD, (h + 1) * HEAD_PAD) for h in heads]
    vrows = [slice(h * V_HEAD_DIM, (h + 1) * V_HEAD_DIM) for h in heads]
    out = []
    for h in heads:
        slot = h % SCORE_SLOTS
        m = jnp.full((1, seq), -jnp.inf, F32)
        for k0 in range(0, seq, leaf):
            keys = slice(k0, k0 + leaf)
            st = _dot_nt(k_ref[keys, hcols[h]], q_ref[k0:, hcols[h]])
            st = jnp.concatenate([causal(st[:, :leaf]), st[:, leaf:]], axis=1) if k0 + leaf < seq else causal(st)
            s_scr[slot, keys, k0:] = st
            m = jnp.maximum(m, lead(col_max(st), k0, -jnp.inf))
        l = jnp.zeros((1, seq), F32)
        acc = jnp.zeros((V_HEAD_DIM, seq), F32)
        for k0 in range(0, seq, leaf):
            keys = slice(k0, k0 + leaf)
            pt = jnp.exp2(s_scr[slot, keys, k0:] - m[:, k0:])
            l = l + lead(col_sum(pt), k0, 0.0)
            acc = acc + lead(_dot(vt_ref[vrows[h], keys], pt.astype(BF16)), k0, 0.0)
        out.append(acc / l)
    o_ref[...] = jnp.transpose(jnp.concatenate(out, axis=0)).astype(BF16)


def _attn_prompt(q, k, vt, *, batch, seq, leaf):
    n = q.shape[0]
    v_blk = HEADS_PER_STEP * V_HEAD_DIM
    assert seq % leaf == 0 and leaf % LANES == 0 and v_blk % LANES == 0 and MLA_HEADS % HEADS_PER_STEP == 0
    row_blk = pl.BlockSpec((seq, HEADS_PER_STEP * HEAD_PAD), lambda b, h: (b, h))
    return pl.pallas_call(
        functools.partial(_attn_prompt_kernel, leaf=leaf, seq=seq),
        grid=(batch, MLA_HEADS // HEADS_PER_STEP),
        in_specs=[row_blk, row_blk, pl.BlockSpec((v_blk, seq), lambda b, h: (h, b))],
        out_specs=pl.BlockSpec((seq, v_blk), lambda b, h: (b, h)),
        out_shape=jax.ShapeDtypeStruct((n, MLA_HEADS * V_HEAD_DIM), BF16),
        scratch_shapes=[pltpu.VMEM((SCORE_SLOTS, seq, seq), F32)],
        compiler_params=_params(("parallel", "parallel")),
        name="attn_prompt",
    )(q, k, vt)


def _attn_sample_kernel(pt_ref, qa_ref, qr_ref, new_ref, cache_ref, ctx_ref, ring, sem, kt_scr, s_scr,
                        *, pages, n_chunks, slots, pv_chunk):
    b = pl.program_id(0)
    nb = pl.num_programs(0)
    keys = pages * PAGE_SIZE
    group = 4

    def fetch(bb, chunk, slot):
        for i in range(pages):
            pid = pt_ref[bb, chunk * pages + i]
            pltpu.make_async_copy(cache_ref.at[pid], ring.at[slot, i], sem.at[slot]).start(priority=i % 2)

    def wait(slot):
        for i in range(pages):
            pltpu.make_async_copy(cache_ref.at[0], ring.at[slot, i], sem.at[slot]).wait()

    @pl.when(b == 0)
    def _():
        for c in range(slots):
            fetch(0, c, c)

    qa = qa_ref[0]
    qr = qr_ref[0]
    rows = qa.shape[0]

    def lane_fold(x, op):
        out = x[:, 0:LANES]
        for j in range(1, x.shape[1] // LANES):
            out = op(out, x[:, j * LANES:(j + 1) * LANES])
        return out

    def score_group(g, mrun):
        c0 = g * group
        for k in range(group):
            wait(lax.rem(c0 + k, slots))
        for k in range(group):
            slot = lax.rem(c0 + k, slots)
            kt = jnp.concatenate([ring[slot, i] for i in range(pages)], axis=1).astype(BF16)
            off = pl.multiple_of((c0 + k) * keys, keys)
            kt_scr[:, pl.ds(off, keys)] = kt
            s = _dot(qa, kt[0:KV_LORA_RANK, :]) + _dot(qr, kt[KV_LORA_RANK:KV_CACHE_DIM, :])
            s_scr[:, pl.ds(off, keys)] = s
            mrun = jnp.maximum(mrun, lane_fold(s, jnp.maximum))
        nxt = c0 + slots
        wrap = nxt >= n_chunks
        bb = jnp.where(wrap, b + 1, b)
        bb = jnp.where(bb < nb, bb, 0)
        cc = jnp.where(wrap, nxt - n_chunks, nxt)
        for k in range(group):
            fetch(bb, cc + k, lax.rem(c0 + k, slots))
        return mrun

    mrun = lax.fori_loop(0, n_chunks // group, score_group, jnp.full((rows, LANES), -jnp.inf, F32))

    @pl.when(b == nb - 1)
    def _():
        for c in range(slots):
            wait(c)

    new = new_ref[0]
    n_new = new.shape[0]
    nc = new[:, 0:KV_LORA_RANK].astype(BF16).astype(F32)
    nr = new[:, KV_LORA_RANK:KV_CACHE_DIM].astype(BF16).astype(F32)
    qaf = qa.astype(F32)
    qrf = qr.astype(F32)
    tok = lax.shift_right_logical(lax.broadcasted_iota(jnp.int32, (rows, 1), 0), int(math.log2(MLA_HEADS)))
    s_new = []
    for t in range(n_new):
        st = (jnp.sum(qaf * nc[t:t + 1, :], axis=-1, keepdims=True)
              + jnp.sum(qrf * nr[t:t + 1, :], axis=-1, keepdims=True))
        s_new.append(jnp.where(tok >= t, st, -jnp.inf))

    m = jnp.max(mrun, axis=-1, keepdims=True)
    for st in s_new:
        m = jnp.maximum(m, st)

    def pv_step(j, carry):
        acc, lrun = carry
        off = pl.multiple_of(j * pv_chunk, pv_chunk)
        p = jnp.exp2(s_scr[:, pl.ds(off, pv_chunk)] - m)
        acc = acc + _dot_nt(p.astype(BF16), kt_scr[0:KV_LORA_RANK, pl.ds(off, pv_chunk)])
        return acc, lrun + lane_fold(p, jnp.add)

    acc, lrun = lax.fori_loop(0, (n_chunks * keys) // pv_chunk, pv_step,
                              (jnp.zeros((rows, KV_LORA_RANK), F32), jnp.zeros((rows, LANES), F32)), unroll=True)
    l = jnp.sum(lrun, axis=-1, keepdims=True)
    for t, st in enumerate(s_new):
        pt = jnp.exp2(st - m)
        l = l + pt
        acc = acc + pt.astype(BF16).astype(F32) * nc[t:t + 1, :]
    ctx_ref[0] = (acc / l).astype(BF16)


def _attn_sample(page_table, qabs, qrope, new_rows, cache_t, *, pages, slots, pv_chunk):
    nb, rows, _ = qabs.shape
    n_pages = page_table.shape[1]
    n_chunks = n_pages // pages
    n_keys = n_pages * PAGE_SIZE
    assert n_chunks * pages == n_pages and n_chunks % slots == 0 and slots % 4 == 0 and n_keys % pv_chunk == 0
    t_new = new_rows.shape[1]
    grid_spec = pltpu.PrefetchScalarGridSpec(
        num_scalar_prefetch=1, grid=(nb,),
        in_specs=[pl.BlockSpec((1, rows, KV_LORA_RANK), lambda b, pt: (b, 0, 0)),
                  pl.BlockSpec((1, rows, QK_ROPE_DIM), lambda b, pt: (b, 0, 0)),
                  pl.BlockSpec((1, t_new, KV_CACHE_DIM), lambda b, pt: (b, 0, 0)),
                  pl.BlockSpec(memory_space=pl.ANY)],
        out_specs=pl.BlockSpec((1, rows, KV_LORA_RANK), lambda b, pt: (b, 0, 0)),
        scratch_shapes=[pltpu.VMEM((slots, pages, KV_CACHE_DIM, PAGE_SIZE), F32),
                        pltpu.SemaphoreType.DMA((slots,)),
                        pltpu.VMEM((KV_CACHE_DIM, n_keys), BF16),
                        pltpu.VMEM((rows, n_keys), F32)])
    return pl.pallas_call(
        functools.partial(_attn_sample_kernel, pages=pages, n_chunks=n_chunks, slots=slots, pv_chunk=pv_chunk),
        grid_spec=grid_spec,
        out_shape=jax.ShapeDtypeStruct((nb, rows, KV_LORA_RANK), BF16),
        compiler_params=_params(("arbitrary",)), name="attn_sample",
    )(page_table, qabs, qrope, new_rows, cache_t)


def _lru_gates(y, wa_ref, ba_ref, wx_ref, bx_ref, lam_ref):
    yb = y.astype(BF16)
    r = jax.nn.sigmoid(_dot(yb, wa_ref[...]) + ba_ref[...])
    i = jax.nn.sigmoid(_dot(yb, wx_ref[...]) + bx_ref[...])
    nl = -lam_ref[...]
    softplus = jnp.maximum(nl, 0.0) + jnp.log1p(jnp.exp(-jnp.abs(nl)))
    log_a = -LRU_C * r * softplus
    a = jnp.exp(log_a)
    th = jnp.tanh(log_a)
    u = (y * i) * jnp.sqrt(-2.0 * th / (1.0 - th))
    return a, u


def _lru_prompt_kernel(lx_ref, lg_ref, cw_ref, cb_ref, wa_ref, ba_ref, wx_ref, bx_ref, lam_ref,
                       rec_ref, ht_ref, xp_scr, a_scr, u_scr, hs_scr, h_scr, *, tt, batch, pitch):
    ti = pl.program_id(0)
    sub = SUBLANES

    @pl.when(ti == 0)
    def _():
        xp_scr[:, 0:sub, :] = jnp.zeros((batch, sub, LRU_WIDTH), F32)
        h_scr[...] = jnp.zeros((batch, LRU_WIDTH), F32)

    cw = cw_ref[...]
    ys = []
    for b in range(batch):
        x = lx_ref[b]
        xp_scr[b, sub:sub + tt, :] = x
        y = cb_ref[...] + cw[3:4, :] * x
        for j in range(CONV_WIDTH - 1):
            y = y + cw[j:j + 1, :] * xp_scr[b, sub - 3 + j:sub - 3 + j + tt, :]
        ys.append(y)
    a, u = _lru_gates(jnp.concatenate(ys, axis=0), wa_ref, ba_ref, wx_ref, bx_ref, lam_ref)
    n_col = LRU_WIDTH // LANES
    cols = [slice(c * LANES, (c + 1) * LANES) for c in range(n_col)]
    for c in range(n_col):
        for b in range(batch):
            a_scr[c, b * pitch:b * pitch + tt, :] = a[b * tt:(b + 1) * tt, cols[c]]
            u_scr[c, b * pitch:b * pitch + tt, :] = u[b * tt:(b + 1) * tt, cols[c]]

    def step(t, h):
        rows = pl.ds(t, batch, stride=pitch)
        h = tuple(a_scr[c, rows, :] * h[c] + u_scr[c, rows, :] for c in range(n_col))
        for c in range(n_col):
            hs_scr[c, rows, :] = h[c]
        return h

    h0 = h_scr[...]
    h_last = lax.fori_loop(0, tt, step, tuple(h0[:, cols[c]] for c in range(n_col)), unroll=8)
    h_last = jnp.concatenate(h_last, axis=1)
    h_scr[...] = h_last
    ht_ref[...] = h_last
    for b in range(batch):
        hs = jnp.concatenate([hs_scr[c, b * pitch:b * pitch + tt, :] for c in range(n_col)], axis=1)
        rec_ref[b] = (hs * jax.nn.gelu(lg_ref[b])).astype(BF16)
        xp_scr[b, 0:sub, :] = xp_scr[b, tt:tt + sub, :]


def _lru_prompt(lx, lg, wts, *, batch, seq, tt):
    n = lx.shape[0]
    pitch = tt + SUBLANES
    blk = pl.BlockSpec((batch, tt, LRU_WIDTH), lambda t: (0, t, 0))
    names = ("conv_w", "conv_b", "wa", "ba", "wx", "bx", "lam")
    to3 = lambda a: a.reshape(batch, seq, LRU_WIDTH)
    rec, ht = pl.pallas_call(
        functools.partial(_lru_prompt_kernel, tt=tt, batch=batch, pitch=pitch),
        grid=(seq // tt,),
        in_specs=[blk, blk] + [_const_spec(wts[k].shape) for k in names],
        out_specs=[blk, pl.BlockSpec((batch, LRU_WIDTH), lambda t: (0, 0))],
        out_shape=[jax.ShapeDtypeStruct((batch, seq, LRU_WIDTH), BF16), jax.ShapeDtypeStruct((batch, LRU_WIDTH), F32)],
        scratch_shapes=[pltpu.VMEM((batch, tt + SUBLANES, LRU_WIDTH), F32)]
        + [pltpu.VMEM((LRU_WIDTH // LANES, batch * pitch, LANES), F32)] * 3 + [pltpu.VMEM((batch, LRU_WIDTH), F32)],
        compiler_params=_params(("arbitrary",)), name="lru_prompt",
    )(to3(lx), to3(lg), *[wts[k] for k in names])
    return rec.reshape(n, LRU_WIDTH), ht


def _lru_sample_kernel(lx_ref, lg_ref, conv_ref, h0_ref, cw_ref, cb_ref, wa_ref, ba_ref, wx_ref, bx_ref, lam_ref,
                       rec_ref, ht_ref):
    steps = lx_ref.shape[0]
    xs = [conv_ref[j] for j in range(CONV_WIDTH - 1)] + [lx_ref[t] for t in range(steps)]
    cw = cw_ref[...]
    h = h0_ref[...]
    for t in range(steps):
        y = cb_ref[...]
        for j in range(CONV_WIDTH):
            y = y + cw[j:j + 1, :] * xs[t + j]
        a, u = _lru_gates(y, wa_ref, ba_ref, wx_ref, bx_ref, lam_ref)
        h = a * h + u
        rec_ref[t] = (h * jax.nn.gelu(lg_ref[t])).astype(BF16)
    ht_ref[...] = h


def _lru_sample(lx_t, lg_t, conv_t, h0, wts):
    steps, nb, _ = lx_t.shape
    names = ("conv_w", "conv_b", "wa", "ba", "wx", "bx", "lam")
    return pl.pallas_call(
        _lru_sample_kernel,
        out_shape=[jax.ShapeDtypeStruct((steps, nb, LRU_WIDTH), BF16), jax.ShapeDtypeStruct((nb, LRU_WIDTH), F32)],
        compiler_params=pltpu.CompilerParams(vmem_limit_bytes=VMEM_LIMIT), name="lru_sample",
    )(lx_t, lg_t, conv_t, h0, *[wts[k] for k in names])


def _mem_kv_kernel(m_ref, g_ref, wk_ref, wv_ref, k4_ref, v4_ref, kb_ref, vb_ref):
    m = _rms(m_ref[0], g_ref[...]).astype(BF16)
    for w_ref, o4_ref, ob_ref in ((wk_ref, k4_ref, kb_ref), (wv_ref, v4_ref, vb_ref)):
        y = _dot(m, w_ref[...])
        ob_ref[0] = y.astype(BF16)
        for h in range(MEM_HEADS):
            o4_ref[0, :, h, :] = y[:, h * MEM_HEAD_DIM:(h + 1) * MEM_HEAD_DIM]


def _mem_kv(mem, g, wk, wv):
    nb, n_mem, _ = mem.shape
    blk3 = pl.BlockSpec((1, n_mem, D_MODEL), lambda i: (i, 0, 0))
    blk4 = pl.BlockSpec((1, n_mem, MEM_HEADS, MEM_HEAD_DIM), lambda i: (i, 0, 0, 0))
    shp4 = jax.ShapeDtypeStruct((nb, n_mem, MEM_HEADS, MEM_HEAD_DIM), F32)
    shp3 = jax.ShapeDtypeStruct((nb, n_mem, D_MODEL), BF16)
    return pl.pallas_call(
        _mem_kv_kernel, grid=(nb,),
        in_specs=[blk3, _const_spec(g.shape), _const_spec(wk.shape), _const_spec(wv.shape)],
        out_specs=[blk4, blk4, blk3, blk3], out_shape=[shp4, shp4, shp3, shp3],
        compiler_params=_params(("parallel",)), name="mem_kv",
    )(mem, g, wk, wv)


def _mix_out_kernel(x_ref, attn_ref, rec_ref, *rest, absorbed):
    if absorbed:
        wuv_ref, woa_ref, wor_ref, g_ref, wq_ref, x1_ref, qm_ref = rest
        attn = _dot(attn_ref[...], wuv_ref[...]).astype(BF16)
    else:
        woa_ref, wor_ref, g_ref, wq_ref, x1_ref, qm_ref = rest
        attn = attn_ref[...]
    x1 = x_ref[...] + _dot(attn, woa_ref[...]) + _dot(rec_ref[...], wor_ref[...])
    x1_ref[...] = x1
    qm_ref[...] = _dot(_rms(x1, g_ref[...]).astype(BF16), wq_ref[...]).astype(BF16)


def _mix_out(x, attn, rec, wts, *, absorbed, tn):
    n = x.shape[0]
    row_spec = lambda w: pl.BlockSpec((tn, w), lambda i: (i, 0))
    names = (("wuv_bd",) if absorbed else ()) + ("woa", "wor", "g_mem", "w_mem_q")
    return pl.pallas_call(
        functools.partial(_mix_out_kernel, absorbed=absorbed), grid=(n // tn,),
        in_specs=[row_spec(D_MODEL), row_spec(attn.shape[1]), row_spec(LRU_WIDTH)]
        + [_const_spec(wts[k].shape) for k in names],
        out_specs=[row_spec(D_MODEL), row_spec(D_MODEL)],
        out_shape=[jax.ShapeDtypeStruct((n, D_MODEL), F32), jax.ShapeDtypeStruct((n, D_MODEL), BF16)],
        compiler_params=_params(("parallel",)), name="mix_out_s" if absorbed else "mix_out_p",
    )(x, attn, rec, *[wts[k] for k in names])


def _mem_attend(q, k_ref, v_ref):
    scale = MEM_HEAD_DIM ** -0.5
    out = []
    for h in range(MEM_HEADS):
        lo, hi = h * MEM_HEAD_DIM, (h + 1) * MEM_HEAD_DIM
        s = _dot_nt(q[:, lo:hi], k_ref[0, :, lo:hi]) * scale
        e = jnp.exp(s - jnp.max(s, axis=-1, keepdims=True))
        p = e / jnp.sum(e, axis=-1, keepdims=True)
        out.append(_dot(p.astype(BF16), v_ref[0, :, lo:hi]).astype(BF16))
    return jnp.concatenate(out, axis=1)


def _mem_attn_cache_kernel(q_ref, k_ref, v_ref, o_ref, *, steps):
    n_mem = k_ref.shape[1]
    for g in range(q_ref.shape[0]):
        q = q_ref[g]
        k2 = k_ref[g].reshape(n_mem * MEM_HEADS, MEM_HEAD_DIM).astype(BF16)
        v2 = v_ref[g].reshape(n_mem * MEM_HEADS, MEM_HEAD_DIM).astype(BF16)
        s = _dot_nt(q, k2) * (MEM_HEAD_DIM ** -0.5)
        row_head = lax.shift_right_logical(lax.broadcasted_iota(jnp.int32, s.shape, 0), int(math.log2(steps)))
        col_head = lax.bitwise_and(lax.broadcasted_iota(jnp.int32, s.shape, 1), MEM_HEADS - 1)
        s = jnp.where(row_head == col_head, s, -jnp.inf)
        e = jnp.exp(s - jnp.max(s, axis=-1, keepdims=True))
        p = e / jnp.sum(e, axis=-1, keepdims=True)
        o_ref[g] = _dot(p.astype(BF16), v2).astype(BF16)


def _mem_attn_cache(q, k, v, *, steps, per_step=8):
    nb, rows, _ = q.shape
    qblk = pl.BlockSpec((per_step, rows, MEM_HEAD_DIM), lambda b: (b, 0, 0))
    kblk = pl.BlockSpec((per_step,) + k.shape[1:], lambda b: (b, 0, 0, 0))
    return pl.pallas_call(
        functools.partial(_mem_attn_cache_kernel, steps=steps), grid=(nb // per_step,),
        in_specs=[qblk, kblk, kblk], out_specs=qblk,
        out_shape=jax.ShapeDtypeStruct(q.shape, BF16),
        compiler_params=_params(("parallel",)), name="mem_attn_s",
    )(q, k, v)


def _tail_kernel(x1_ref, *rest, ff_chunk, attend):
    if attend:
        (qm_ref, k_ref, v_ref, qm_nxt_ref, k_nxt_ref, v_nxt_ref,
         wo_ref, g_ref, wup_ref, wdn_ref, gf_ref, y_ref, o_scr) = rest

        @pl.when(pl.program_id(0) == 0)
        def _():
            o_scr[...] = _mem_attend(qm_ref[...], k_ref, v_ref)

        o = o_scr[...]
    else:
        o_ref, wo_ref, g_ref, wup_ref, wdn_ref, gf_ref, y_ref = rest
        o = o_ref[...]
    x2 = x1_ref[...] + _dot(o, wo_ref[...])
    h = _rms(x2, g_ref[...]).astype(BF16)
    acc = x2
    for c in range(D_FF // ff_chunk):
        lo, hi = c * ff_chunk, (c + 1) * ff_chunk
        up = jnp.maximum(_dot(h, wup_ref[:, lo:hi]), 0.0)
        acc = acc + _dot((up * up).astype(BF16), wdn_ref[lo:hi, :])
    y_ref[...] = _rms(acc, gf_ref[...])
    if attend:
        o_scr[...] = _mem_attend(qm_nxt_ref[...], k_nxt_ref, v_nxt_ref)


def _tail(x1, o, wts, *, tn, name, mem_kv=None, seq=None):
    n = x1.shape[0]
    steps = n // tn
    blk = pl.BlockSpec((tn, D_MODEL), lambda i: (i, 0))
    names = ("w_mem_o", "g_mlp", "w_up", "w_down", "g_final")
    args, specs, scratch = [x1, o], [blk, blk], []
    if mem_kv is not None:
        per_seq = seq // tn
        nxt = lambda i: jnp.minimum(i + 1, steps - 1)
        kshape = (1,) + mem_kv[0].shape[1:]
        kblk = pl.BlockSpec(kshape, lambda i: (i // per_seq, 0, 0))
        kblk_nxt = pl.BlockSpec(kshape, lambda i: (nxt(i) // per_seq, 0, 0))
        args += [*mem_kv, o, *mem_kv]
        specs += [kblk, kblk, pl.BlockSpec((tn, D_MODEL), lambda i: (nxt(i), 0)), kblk_nxt, kblk_nxt]
        scratch = [pltpu.VMEM((tn, D_MODEL), BF16)]
    return pl.pallas_call(
        functools.partial(_tail_kernel, ff_chunk=1024, attend=mem_kv is not None), grid=(steps,),
        in_specs=specs + [_const_spec(wts[k].shape) for k in names],
        out_specs=blk, out_shape=jax.ShapeDtypeStruct((n, D_MODEL), F32), scratch_shapes=scratch,
        compiler_params=_params(("arbitrary",) if mem_kv is not None else ("parallel",)), name=name,
    )(*args, *[wts[k] for k in names])


def _rot_half_cols(w):
    half = QK_ROPE_DIM // 2
    return jnp.concatenate([-w[..., half:], w[..., :half]], axis=-1)


def _prep_weights(w_in, q_norm_g, w_q_up, kv_norm_g, w_uk, w_uv, conv_w, conv_b, lru_w_a, lru_b_a, lru_w_x, lru_b_x,
                  lru_lambda, w_out, norm_mem_g, mem_norm_g, w_mem_q, w_mem_k, w_mem_v, w_mem_o, norm_mlp_g,
                  w_up, w_down, final_norm_g):
    w = {}
    s1 = Q_LORA_RANK
    s2 = s1 + KV_LORA_RANK
    s3 = s2 + QK_ROPE_DIM
    s4 = s3 + LRU_WIDTH
    wt = w_in.T
    wkr = wt[s2:s3]
    half = QK_ROPE_DIM // 2
    wkr_rot = jnp.concatenate([-wkr[half:], wkr[:half]], axis=0)
    lane_pad = jnp.zeros((_KR_ROT_LANE - QK_ROPE_DIM, D_MODEL), F32)
    w["win"] = jnp.concatenate([wt[:s1], wt[s1:s2], wt[s3:s4], wt[s4:], wkr, lane_pad, wkr_rot, lane_pad],
                               axis=0).astype(BF16)
    w["qg"] = q_norm_g.reshape(1, -1)
    w["kvg"] = kv_norm_g.reshape(1, -1)
    wq3 = w_q_up.reshape(Q_LORA_RANK, MLA_HEADS, QK_NOPE_DIM + QK_ROPE_DIM)
    nope, ropew = wq3[..., :QK_NOPE_DIM], wq3[..., QK_NOPE_DIM:]
    tail_pad = jnp.zeros((Q_LORA_RANK, MLA_HEADS, HEAD_PAD - QK_NOPE_DIM - QK_ROPE_DIM), F32)
    w["wq"] = jnp.concatenate([nope, ropew, tail_pad], -1).reshape(Q_LORA_RANK, -1).astype(BF16)
    w["wqr"] = _rot_half_cols(ropew).reshape(Q_LORA_RANK, -1).astype(BF16)
    head_pad = jnp.zeros((KV_LORA_RANK, MLA_HEADS, HEAD_PAD - QK_NOPE_DIM), F32)
    w["wuk"] = jnp.concatenate([w_uk, head_pad], -1).reshape(KV_LORA_RANK, -1).astype(BF16)
    w["wuv_t"] = w_uv.reshape(KV_LORA_RANK, -1).T.astype(BF16)
    eye = jnp.eye(MLA_HEADS, dtype=F32)
    uk_t = jnp.transpose(w_uk, (1, 2, 0))
    uk_t = jnp.concatenate([uk_t, jnp.zeros((MLA_HEADS, HEAD_PAD - QK_NOPE_DIM, KV_LORA_RANK), F32)], 1)
    w["wabs"] = uk_t.astype(BF16)
    w["wuv_bd"] = jnp.einsum("chd,hg->hcgd", w_uv, eye).reshape(MLA_HEADS * KV_LORA_RANK, -1).astype(BF16)
    n_attn = MLA_HEADS * V_HEAD_DIM
    w["woa"] = w_out[:n_attn].astype(BF16)
    w["wor"] = w_out[n_attn:].astype(BF16)
    w["conv_w"] = conv_w
    w["conv_b"] = conv_b.reshape(1, -1)
    eye_l = jnp.eye(LRU_BLOCKS, dtype=F32)
    w["wa"] = jnp.einsum("nde,nm->ndme", lru_w_a, eye_l).reshape(LRU_WIDTH, LRU_WIDTH).astype(BF16)
    w["wx"] = jnp.einsum("nde,nm->ndme", lru_w_x, eye_l).reshape(LRU_WIDTH, LRU_WIDTH).astype(BF16)
    w["ba"] = lru_b_a.reshape(1, -1)
    w["bx"] = lru_b_x.reshape(1, -1)
    w["lam"] = lru_lambda.reshape(1, -1)
    w["g_mem"] = norm_mem_g.reshape(1, -1)
    w["g_memkv"] = mem_norm_g.reshape(1, -1)
    w["w_mem_q"] = w_mem_q.astype(BF16)
    w["w_mem_k"] = w_mem_k.astype(BF16)
    w["w_mem_v"] = w_mem_v.astype(BF16)
    w["w_mem_o"] = w_mem_o.astype(BF16)
    w["g_mlp"] = norm_mlp_g.reshape(1, -1)
    w["w_up"] = w_up.astype(BF16)
    w["w_down"] = w_down.astype(BF16)
    w["g_final"] = final_norm_g.reshape(1, -1)
    return w


def _rope_tables(pos):
    pos = np.asarray(pos, np.float64)
    inv = ROPE_THETA ** (-np.arange(0, QK_ROPE_DIM, 2, dtype=np.float64) / QK_ROPE_DIM)
    ang = pos[:, None] * inv[None, :]
    cos, sin = np.cos(ang), np.sin(ang)
    n = pos.shape[0]
    z = lambda k: np.zeros((n, k))
    q_tail = HEAD_PAD - QK_NOPE_DIM - QK_ROPE_DIM
    c_exp = MLA_SCALE * LOG2E
    cq = np.concatenate([np.ones((n, QK_NOPE_DIM)), cos, cos, z(q_tail)], 1) * c_exp
    sq = np.concatenate([z(QK_NOPE_DIM), sin, sin, z(q_tail)], 1) * c_exp
    k_gap = z(_KR_ROT_LANE - QK_ROPE_DIM)
    cks = np.concatenate([cos, cos, k_gap, sin, sin, k_gap], 1)
    return tuple(jnp.asarray(t, F32) for t in (cq, sq, cks))


def kernel(x_prompt, x_sample, cache_mla, cache_mem_k, cache_mem_v, state_lru_h, state_conv, page_table, mem_prompt, norm_mix_g, w_in, q_norm_g, w_q_up, kv_norm_g, w_uk, w_uv, conv_w, conv_b, lru_w_a, lru_b_a, lru_w_x, lru_b_x, lru_lambda, w_out, norm_mem_g, mem_norm_g, w_mem_q, w_mem_k, w_mem_v, w_mem_o, norm_mlp_g, w_up, w_down, final_norm_g):
    B, S, _ = x_prompt.shape
    Bd, T, _ = x_sample.shape
    assert w_in.shape[0] == 1, "single layer"
    wts = _prep_weights(w_in[0], q_norm_g[0], w_q_up[0], kv_norm_g[0], w_uk[0], w_uv[0], conv_w[0], conv_b[0],
                        lru_w_a[0], lru_b_a[0], lru_w_x[0], lru_b_x[0], lru_lambda[0], w_out[0], norm_mem_g[0],
                        mem_norm_g[0], w_mem_q[0], w_mem_k[0], w_mem_v[0], w_mem_o[0], norm_mlp_g[0],
                        w_up[0], w_down[0], final_norm_g)
    g_mix = norm_mix_g[0].reshape(1, -1)
    tn = 512

    xp = x_prompt.reshape(B * S, D_MODEL)
    q_p, rows_p, lx_p, lg_p, k_p, v_p = _in_proj(xp, g_mix, wts, _rope_tables(np.arange(S)),
                                                 prompt=True, tn=2 * tn, pos_blocks=S // (2 * tn))
    attn_p = _attn_prompt(q_p, k_p, v_p, batch=B, seq=S, leaf=256)
    rec_p, ht_p = _lru_prompt(lx_p, lg_p, wts, batch=B, seq=S, tt=128)
    x1_p, qm_p = _mix_out(xp, attn_p, rec_p, wts, absorbed=False, tn=2 * tn)
    n_mem = mem_prompt.shape[1]
    mk_p, mv_p, mkb_p, mvb_p = _mem_kv(mem_prompt, wts["g_memkv"], wts["w_mem_k"], wts["w_mem_v"])
    y_p = _tail(x1_p, qm_p, wts, tn=tn, name="tail_p", mem_kv=(mkb_p, mvb_p), seq=S)

    ns = Bd * T
    xs = x_sample.reshape(ns, D_MODEL)
    tabs_s = _rope_tables(np.tile(PAST_LEN + np.arange(T), Bd))
    q_s, rows_s, lx_s, lg_s, qabs_s = _in_proj(xs, g_mix, wts, tabs_s, prompt=False, tn=ns, pos_blocks=1)
    rows_q = T * MLA_HEADS
    qr_s = q_s.reshape(ns * MLA_HEADS, HEAD_PAD)[:, QK_NOPE_DIM:QK_NOPE_DIM + QK_ROPE_DIM]
    ctx_s = _attn_sample(page_table, qabs_s.reshape(Bd, rows_q, KV_LORA_RANK), qr_s.reshape(Bd, rows_q, QK_ROPE_DIM),
                         rows_s.reshape(Bd, T, KV_CACHE_DIM), jnp.swapaxes(cache_mla[0], 1, 2),
                         pages=8, slots=16, pv_chunk=2048)
    to_time_major = lambda a: jnp.transpose(a.reshape(Bd, -1, LRU_WIDTH), (1, 0, 2))
    rec_t, ht_s = _lru_sample(to_time_major(lx_s), to_time_major(lg_s), to_time_major(state_conv[0]),
                              state_lru_h[0], wts)
    rec_s = jnp.transpose(rec_t, (1, 0, 2)).reshape(ns, LRU_WIDTH)
    x1_s, qm_s = _mix_out(xs, ctx_s.reshape(ns, MLA_HEADS * KV_LORA_RANK), rec_s, wts, absorbed=True, tn=ns)
    head_major = lambda a: jnp.transpose(a.reshape(Bd, T, MEM_HEADS, MEM_HEAD_DIM), (0, 2, 1, 3))
    o_s = _mem_attn_cache(head_major(qm_s).reshape(Bd, MEM_HEADS * T, MEM_HEAD_DIM), cache_mem_k[0], cache_mem_v[0],
                          steps=T)
    o_s = jnp.transpose(o_s.reshape(Bd, MEM_HEADS, T, MEM_HEAD_DIM), (0, 2, 1, 3))
    y_s = _tail(x1_s, o_s.reshape(ns, D_MODEL), wts, tn=ns, name="tail_s")

    lx_p3 = lx_p.reshape(B, S, LRU_WIDTH)
    lx_s3 = lx_s.reshape(Bd, T, LRU_WIDTH)
    keep = CONV_WIDTH - 1
    return (y_p.reshape(B, S, D_MODEL), y_s.reshape(Bd, T, D_MODEL),
            rows_p.reshape(1, B, S, KV_CACHE_DIM), rows_s.reshape(1, Bd, T, KV_CACHE_DIM),
            mk_p.reshape(1, B, n_mem, MEM_HEADS, MEM_HEAD_DIM), mv_p.reshape(1, B, n_mem, MEM_HEADS, MEM_HEAD_DIM),
            ht_p.reshape(1, B, LRU_WIDTH), ht_s.reshape(1, Bd, LRU_WIDTH),
            lx_p3[:, S - keep:].reshape(1, B, keep, LRU_WIDTH), lx_s3[:, T - keep:].reshape(1, Bd, keep, LRU_WIDTH))
```
